```python
import math
import jax
import jax.numpy as jnp
from jax import lax
import numpy as np

D_MODEL = 1024
BATCH = 16
SEQ = 2048
DEPTH = 4

GRID_W = 64
CTX_LEN = 256
N_MIXERS = 4
Q_BLOCK = 128
RMS_EPS = 1e-6
ROPE_BASE = 10000.0

DA_HEADS = 8
DA_HEAD_DIM = 64
DA_V_DIM = 2 * DA_HEAD_DIM

GM_CHUNK = 128
GM_WIDTH = D_MODEL
GM_GROUPS = 8
GM_GROUP_DIM = GM_WIDTH // GM_GROUPS

SSM_INNER = 2 * D_MODEL
SSM_HEAD_DIM = 64
SSM_HEADS = SSM_INNER // SSM_HEAD_DIM
SSM_GROUPS = 4
SSM_HEADS_PER_GROUP = SSM_HEADS // SSM_GROUPS
SSM_STATE = 128
SSM_CONV = 5
SSM_CHUNK = 128
SSM_CONV_DIM = SSM_INNER + 2 * SSM_GROUPS * SSM_STATE

MLA_HEADS = 16
MLA_Q_RANK = 384
MLA_KV_RANK = 256
MLA_NOPE = 64
MLA_ROPE = 32
MLA_V = 64
MLA_QK = MLA_NOPE + MLA_ROPE

MOE_GROUPS = 4
MOE_PER_GROUP = 8
MOE_EXPERTS = MOE_GROUPS * MOE_PER_GROUP
MOE_TOP_K = 2
MOE_FF = 512

N_DA = (DEPTH + 3) // N_MIXERS
N_GM = (DEPTH + 2) // N_MIXERS
N_SSM = (DEPTH + 1) // N_MIXERS
N_MLA = DEPTH // N_MIXERS

kernel_name = 'hybrid_interleaved_diffusion_trunk'


def rms_norm(x, g):
    xf = x.astype(jnp.float32)
    y = xf * lax.rsqrt(jnp.mean(xf * xf, axis=-1, keepdims=True) + RMS_EPS)
    return (y * g.astype(jnp.float32)).astype(x.dtype)


def modulate(x, shift, scale):
    return x * (1 + scale) + shift


def axial_rope_tables(n_tokens, rot_dim):
    rows = n_tokens // GRID_W
    row = jnp.repeat(jnp.arange(rows, dtype=jnp.float32), GRID_W)
    col = jnp.tile(jnp.arange(GRID_W, dtype=jnp.float32), rows)
    n_freq = rot_dim // 4
    inv_freq = ROPE_BASE ** (-jnp.arange(n_freq, dtype=jnp.float32) / n_freq)
    ang = jnp.concatenate([row[:, None] * inv_freq, col[:, None] * inv_freq], axis=-1)
    return jnp.cos(ang), jnp.sin(ang)


def apply_rope(x, cos, sin):
    half = x.shape[-1] // 2
    x1, x2 = x[..., :half], x[..., half:]
    c = cos[None, :, None, :].astype(x.dtype)
    s = sin[None, :, None, :].astype(x.dtype)
    return jnp.concatenate([x1 * c - x2 * s, x1 * s + x2 * c], axis=-1)


def over_query_blocks(fn, q):
    b, t = q.shape[:2]
    nb = t // Q_BLOCK
    qb = jnp.moveaxis(q.reshape((b, nb, Q_BLOCK) + q.shape[2:]), 1, 0)
    out = lax.map(fn, qb)
    return jnp.moveaxis(out, 0, 1).reshape(b, t, out.shape[-1])


def softmax_attention(q, k, v, scale):
    b, tq = q.shape[:2]
    s = jnp.einsum('bqhd,bkhd->bhqk', q, k).astype(jnp.float32) * scale
    p = jax.nn.softmax(s, axis=-1).astype(v.dtype)
    return jnp.einsum('bhqk,bkhe->bqhe', p, v).reshape(b, tq, -1)


def diff_attention(xl, xc, w_in, w_out, q_g, k_g, lam_q1, lam_k1, lam_q2, lam_k2, sub_g, layer_idx, want_ctx):
    H, d = DA_HEADS, DA_HEAD_DIM
    f32 = jnp.float32

    def project(x):
        b, t, _ = x.shape
        q, k, v = jnp.split(x @ w_in, 3, axis=-1)
        q = rms_norm(q.reshape(b, t, 2 * H, d), q_g)
        k = rms_norm(k.reshape(b, t, 2 * H, d), k_g)
        return q, k, v.reshape(b, t, H, DA_V_DIM)

    ql, kl, vl = project(xl)
    qc, kc, vc = project(xc)
    cos, sin = axial_rope_tables(xl.shape[1], d)
    ql, kl = apply_rope(ql, cos, sin), apply_rope(kl, cos, sin)

    lam_init = 0.8 - 0.6 * math.exp(-0.3 * layer_idx)
    lam = (jnp.exp(jnp.sum(lam_q1.astype(f32) * lam_k1.astype(f32)))
           - jnp.exp(jnp.sum(lam_q2.astype(f32) * lam_k2.astype(f32))) + lam_init)
    scale = d ** -0.5

    def attend(q, k, v):
        b, tq = q.shape[:2]
        tk = k.shape[1]
        q = q.reshape(b, tq, H, 2, d)
        k = k.reshape(b, tk, H, 2, d)
        s = jnp.einsum('bqhmd,bkhmd->bhmqk', q, k).astype(f32) * scale
        p = jax.nn.softmax(s, axis=-1)
        a = (p[:, :, 0] - lam * p[:, :, 1]).astype(v.dtype)
        o = jnp.einsum('bhqk,bkhe->bqhe', a, v)
        o = rms_norm(o, sub_g) * (1.0 - lam_init)
        return o.reshape(b, tq, H * DA_V_DIM)

    k_all = jnp.concatenate([kc, kl], axis=1)
    v_all = jnp.concatenate([vc, vl], axis=1)
    out_l = over_query_blocks(lambda qb: attend(qb, k_all, v_all), ql) @ w_out
    out_c = attend(qc, kc, vc) @ w_out if want_ctx else None
    return out_l, out_c


def chunk_gmlp(xl, xc, w_in, v_g, w_s, b_s, w_out, want_ctx):
    def mix(x):
        b, t, _ = x.shape
        u, v = jnp.split(jax.nn.gelu(x @ w_in, approximate=False), 2, axis=-1)
        v = rms_norm(v, v_g).reshape(b, t // GM_CHUNK, GM_CHUNK, GM_GROUPS, GM_GROUP_DIM)
        s = jnp.einsum('gpq,bcqgd->bcpgd', w_s, v) + b_s.T[None, None, :, :, None]
        return (u * s.reshape(b, t, GM_WIDTH)) @ w_out
    return mix(xl), (mix(xc) if want_ctx else None)


def depthwise_conv_centred(x, w, bias):
    pad = (SSM_CONV - 1) // 2
    y = lax.conv_general_dilated(x, w[:, None, :].astype(x.dtype), window_strides=(1,),
                                 padding=[(pad, pad)], dimension_numbers=('NWC', 'WIO', 'NWC'),
                                 feature_group_count=x.shape[-1])
    return y + bias


def segsum(a):
    t = a.shape[-1]
    cs = jnp.cumsum(a, axis=-1)
    diff = cs[..., :, None] - cs[..., None, :]
    mask = jnp.arange(t)[:, None] >= jnp.arange(t)[None, :]
    return jnp.where(mask, diff, -jnp.inf)


def ssd_chunked(x, a, bm, cm, h0):
    b, t = x.shape[:2]
    nc = t // SSM_CHUNK
    x = x.reshape((b, nc, SSM_CHUNK) + x.shape[2:])
    bm = bm.reshape((b, nc, SSM_CHUNK) + bm.shape[2:])
    cm = cm.reshape((b, nc, SSM_CHUNK) + cm.shape[2:])
    a = jnp.moveaxis(a.reshape((b, nc, SSM_CHUNK) + a.shape[2:]), 2, -1)
    a_cum = jnp.cumsum(a, axis=-1)
    decay_in = jnp.exp(segsum(a))
    cb = jnp.einsum('bclgn,bcsgn->bcgls', cm, bm)
    y_diag = jnp.einsum('bcgels,bcsgep->bclgep', cb[:, :, :, None] * decay_in, x)
    decay_to_end = jnp.exp(a_cum[..., -1:] - a_cum)
    xd = x * jnp.moveaxis(decay_to_end, -1, 2)[..., None]
    states = jnp.einsum('bclgn,bclgep->bcgepn', bm, xd)
    states = jnp.concatenate([h0[:, None].astype(states.dtype), states], axis=1)
    chunk_a = jnp.pad(jnp.moveaxis(a_cum[..., -1], 1, -1), [(0, 0), (0, 0), (0, 0), (1, 0)])
    decay_chunk = jnp.exp(segsum(chunk_a))
    states = jnp.einsum('bgezc,bcgepn->bzgepn', decay_chunk, states)
    y_off = (jnp.einsum('bclgn,bcgepn->bclgep', cm, states[:, :-1])
             * jnp.moveaxis(jnp.exp(a_cum), -1, 2)[..., None])
    y = (y_diag + y_off).reshape((b, t) + y_diag.shape[3:])
    return y, states[:, -1]


def mamba2_bidir(xl, xc, w_in, conv_w, conv_b, dt_bias, a_log, d_skip, out_g, w_out, want_ctx):
    G, E, P, N = SSM_GROUPS, SSM_HEADS_PER_GROUP, SSM_HEAD_DIM, SSM_STATE
    f32 = jnp.float32
    A = -jnp.exp(a_log.astype(f32)).reshape(2, G, E)

    def prepare(x):
        b, t, _ = x.shape
        z, xbc, dt = jnp.split(x @ w_in, [SSM_INNER, SSM_INNER + SSM_CONV_DIM], axis=-1)
        xbc = jax.nn.silu(depthwise_conv_centred(xbc, conv_w, conv_b))
        xs, bm, cm = jnp.split(xbc, [SSM_INNER, SSM_INNER + G * N], axis=-1)
        dt = jax.nn.softplus(dt.astype(f32).reshape(b, t, 2, SSM_HEADS) + dt_bias.astype(f32))
        return (z, xs.reshape(b, t, G, E, P), bm.reshape(b, t, G, N), cm.reshape(b, t, G, N),
                dt.reshape(b, t, 2, G, E))

    def run(xs, bm, cm, dt, direction, h0):
        dtd = dt[:, :, direction]
        a = dtd * A[direction]
        xdt = xs * dtd[..., None]
        if direction == 1:
            xdt, a, bm, cm = (jnp.flip(v, axis=1) for v in (xdt, a, bm, cm))
        y, h_last = ssd_chunked(xdt, a, bm, cm, h0)
        if direction == 1:
            y = jnp.flip(y, axis=1)
        return y, h_last

    def finish(y_f, y_b, xs, z):
        b, t = xs.shape[:2]
        y = y_f + y_b + d_skip.reshape(G, E)[:, :, None] * xs
        y = rms_norm(y.reshape(b, t, SSM_INNER) * jax.nn.silu(z), out_g)
        return y @ w_out

    zc, xsc, bc, cc, dtc = prepare(xc)
    zl, xsl, bl, cl, dtl = prepare(xl)
    h0 = jnp.zeros((xc.shape[0], G, E, P, N), f32)
    yc_f, hc_f = run(xsc, bc, cc, dtc, 0, h0)
    yc_b, hc_b = run(xsc, bc, cc, dtc, 1, h0)
    yl_f, _ = run(xsl, bl, cl, dtl, 0, hc_f)
    yl_b, _ = run(xsl, bl, cl, dtl, 1, hc_b)
    out_l = finish(yl_f, yl_b, xsl, zl)
    out_c = finish(yc_f, yc_b, xsc, zc) if want_ctx else None
    return out_l, out_c


def mla_attention(xl, xc, w_in, q_norm_g, kv_norm_g, w_uq, w_ukv, q_g, k_g, w_out, want_ctx):
    H = MLA_HEADS

    def project(x):
        b, t, _ = x.shape
        cq, ckv, k_pe = jnp.split(x @ w_in, [MLA_Q_RANK, MLA_Q_RANK + MLA_KV_RANK], axis=-1)
        q = (rms_norm(cq, q_norm_g) @ w_uq).reshape(b, t, H, MLA_QK)
        kv = (rms_norm(ckv, kv_norm_g) @ w_ukv).reshape(b, t, H, MLA_NOPE + MLA_V)
        k_nope, v = jnp.split(kv, [MLA_NOPE], axis=-1)
        k = jnp.concatenate([k_nope, jnp.broadcast_to(k_pe[:, :, None, :], (b, t, H, MLA_ROPE))], axis=-1)
        return rms_norm(q, q_g), rms_norm(k, k_g), v

    ql, kl, vl = project(xl)
    qc, kc, vc = project(xc)
    cos, sin = axial_rope_tables(xl.shape[1], MLA_ROPE)

    def rope_tail(z):
        return jnp.concatenate([z[..., :MLA_NOPE], apply_rope(z[..., MLA_NOPE:], cos, sin)], axis=-1)

    ql, kl = rope_tail(ql), rope_tail(kl)
    scale = MLA_QK ** -0.5
    k_all = jnp.concatenate([kc, kl], axis=1)
    v_all = jnp.concatenate([vc, vl], axis=1)
    out_l = over_query_blocks(lambda qb: softmax_attention(qb, k_all, v_all, scale), ql) @ w_out
    out_c = softmax_attention(qc, kc, vc, scale) @ w_out if want_ctx else None
    return out_l, out_c


def hier_moe(x, w_group, b_group, w_router, b_router, w1, w3, w2):
    f32 = jnp.float32
    n_tok = x.shape[0]
    g_logits = (x @ w_group).astype(f32) + b_group
    _, g_idx = lax.top_k(g_logits, 1)
    p_group = jnp.take_along_axis(jax.nn.softmax(g_logits, axis=-1), g_idx, axis=-1)
    e_logits = ((x @ w_router).astype(f32) + b_router).reshape(n_tok, MOE_GROUPS, MOE_PER_GROUP)
    sel = jnp.broadcast_to(g_idx[:, :, None], (n_tok, 1, MOE_PER_GROUP))
    e_logits = jnp.take_along_axis(e_logits, sel, axis=1)[:, 0]
    e_top, e_idx = lax.top_k(e_logits, MOE_TOP_K)
    weights = jax.nn.softmax(e_top, axis=-1) * p_group
    expert = g_idx * MOE_PER_GROUP + e_idx
    gates = jnp.sum(jax.nn.one_hot(expert, MOE_EXPERTS, dtype=f32) * weights[..., None], axis=1)
    y = jnp.zeros_like(x)
    for e in range(MOE_EXPERTS):
        hid = jax.nn.silu(x @ w1[e]) * (x @ w3[e])
        y = y + gates[:, e:e + 1].astype(x.dtype) * (hid @ w2[e])
    return y


def setup_inputs(seed: int = 0) -> dict:
    key = jax.random.key(seed)
    ks = iter(jax.random.split(key, 64))
    f32 = jnp.float32
    D = D_MODEL

    def nrm(shape, scale):
        return scale * jax.random.normal(next(ks), shape, f32)

    def gain(shape):
        return 1.0 + 0.05 * jax.random.normal(next(ks), shape, f32)

    u = jax.random.uniform(next(ks), (N_SSM, 2, SSM_HEADS), f32)
    dt0 = jnp.exp(u * (math.log(0.1) - math.log(0.001)) + math.log(0.001))
    ssm_dt_bias = dt0 + jnp.log(-jnp.expm1(-dt0))
    ssm_a_log = jnp.log(jax.random.uniform(next(ks), (N_SSM, 2, SSM_HEADS), f32, 1.0, 16.0))

    return {
        'x': nrm((BATCH, SEQ, D), 1.0),
        'c': nrm((BATCH, D), 1.0),
        'ctx': nrm((BATCH, CTX_LEN, D), 1.0),
        'c_ctx': nrm((D,), 1.0),
        'ada_w': nrm((DEPTH, D, 6 * D), 0.5 * D ** -0.5),
        'ada_b': nrm((DEPTH, 6 * D), 0.01),
        'norm1_g': gain((DEPTH, D)),
        'norm2_g': gain((DEPTH, D)),
        'da_w_in': nrm((N_DA, D, 3 * 2 * DA_HEADS * DA_HEAD_DIM), D ** -0.5),
        'da_w_out': nrm((N_DA, DA_HEADS * DA_V_DIM, D), (DA_HEADS * DA_V_DIM) ** -0.5),
        'da_q_g': gain((N_DA, DA_HEAD_DIM)),
        'da_k_g': gain((N_DA, DA_HEAD_DIM)),
        'da_lam_q1': nrm((N_DA, DA_HEAD_DIM), 0.1),
        'da_lam_k1': nrm((N_DA, DA_HEAD_DIM), 0.1),
        'da_lam_q2': nrm((N_DA, DA_HEAD_DIM), 0.1),
        'da_lam_k2': nrm((N_DA, DA_HEAD_DIM), 0.1),
        'da_sub_g': gain((N_DA, DA_V_DIM)),
        'gm_w_in': nrm((N_GM, D, 2 * GM_WIDTH), D ** -0.5),
        'gm_v_g': gain((N_GM, GM_WIDTH)),
        'gm_w_s': nrm((N_GM, GM_GROUPS, GM_CHUNK, GM_CHUNK), GM_CHUNK ** -0.5),
        'gm_b_s': 1.0 + nrm((N_GM, GM_GROUPS, GM_CHUNK), 0.1),
        'gm_w_out': nrm((N_GM, GM_WIDTH, D), GM_WIDTH ** -0.5),
        'ssm_w_in': nrm((N_SSM, D, SSM_INNER + SSM_CONV_DIM + 2 * SSM_HEADS), D ** -0.5),
        'ssm_conv_w': nrm((N_SSM, SSM_CONV, SSM_CONV_DIM), SSM_CONV ** -0.5),
        'ssm_conv_b': nrm((N_SSM, SSM_CONV_DIM), 0.01),
        'ssm_dt_bias': ssm_dt_bias,
        'ssm_a_log': ssm_a_log,
        'ssm_d': 1.0 + nrm((N_SSM, SSM_HEADS), 0.1),
        'ssm_out_g': gain((N_SSM, SSM_INNER)),
        'ssm_w_out': nrm((N_SSM, SSM_INNER, D), SSM_INNER ** -0.5),
        'mla_w_in': nrm((N_MLA, D, MLA_Q_RANK + MLA_KV_RANK + MLA_ROPE), D ** -0.5),
        'mla_q_norm_g': gain((N_MLA, MLA_Q_RANK)),
        'mla_kv_norm_g': gain((N_MLA, MLA_KV_RANK)),
        'mla_w_uq': nrm((N_MLA, MLA_Q_RANK, MLA_HEADS * MLA_QK), MLA_Q_RANK ** -0.5),
        'mla_w_ukv': nrm((N_MLA, MLA_KV_RANK, MLA_HEADS * (MLA_NOPE + MLA_V)), MLA_KV_RANK ** -0.5),
        'mla_q_g': gain((N_MLA, MLA_QK)),
        'mla_k_g': gain((N_MLA, MLA_QK)),
        'mla_w_out': nrm((N_MLA, MLA_HEADS * MLA_V, D), (MLA_HEADS * MLA_V) ** -0.5),
        'moe_w_group': nrm((DEPTH, D, MOE_GROUPS), D ** -0.5),
        'moe_b_group': nrm((DEPTH, MOE_GROUPS), 0.01),
        'moe_w_router': nrm((DEPTH, D, MOE_EXPERTS), D ** -0.5),
        'moe_b_router': nrm((DEPTH, MOE_EXPERTS), 0.01),
        'moe_w1': nrm((DEPTH, MOE_EXPERTS, D, MOE_FF), D ** -0.5),
        'moe_w3': nrm((DEPTH, MOE_EXPERTS, D, MOE_FF), D ** -0.5),
        'moe_w2': nrm((DEPTH, MOE_EXPERTS, MOE_FF, D), MOE_FF ** -0.5),
    }


def reference(x, c, ctx, c_ctx, ada_w, ada_b, norm1_g, norm2_g,
              da_w_in, da_w_out, da_q_g, da_k_g, da_lam_q1, da_lam_k1, da_lam_q2, da_lam_k2, da_sub_g,
              gm_w_in, gm_v_g, gm_w_s, gm_b_s, gm_w_out,
              ssm_w_in, ssm_conv_w, ssm_conv_b, ssm_dt_bias, ssm_a_log, ssm_d, ssm_out_g, ssm_w_out,
              mla_w_in, mla_q_norm_g, mla_kv_norm_g, mla_w_uq, mla_w_ukv, mla_q_g, mla_k_g, mla_w_out,
              moe_w_group, moe_b_group, moe_w_router, moe_b_router, moe_w1, moe_w3, moe_w2):
    h, hc = x, ctx
    b, s, dm = x.shape
    for i in range(DEPTH):
        kind, j = i % N_MIXERS, i // N_MIXERS
        want_ctx = i < DEPTH - 1
        mod_l = jax.nn.silu(c) @ ada_w[i] + ada_b[i]
        mod_c = jax.nn.silu(c_ctx) @ ada_w[i] + ada_b[i]
        sh1, sc1, g1, sh2, sc2, g2 = jnp.split(mod_l[:, None, :], 6, axis=-1)
        csh1, csc1, cg1, csh2, csc2, cg2 = jnp.split(mod_c, 6, axis=-1)

        xl = modulate(rms_norm(h, norm1_g[i]), sh1, sc1)
        xc = modulate(rms_norm(hc, norm1_g[i]), csh1, csc1)
        if kind == 0:
            dl, dc = diff_attention(xl, xc, da_w_in[j], da_w_out[j], da_q_g[j], da_k_g[j], da_lam_q1[j],
                                    da_lam_k1[j], da_lam_q2[j], da_lam_k2[j], da_sub_g[j], i, want_ctx)
        elif kind == 1:
            dl, dc = chunk_gmlp(xl, xc, gm_w_in[j], gm_v_g[j], gm_w_s[j], gm_b_s[j], gm_w_out[j], want_ctx)
        elif kind == 2:
            dl, dc = mamba2_bidir(xl, xc, ssm_w_in[j], ssm_conv_w[j], ssm_conv_b[j], ssm_dt_bias[j],
                                  ssm_a_log[j], ssm_d[j], ssm_out_g[j], ssm_w_out[j], want_ctx)
        else:
            dl, dc = mla_attention(xl, xc, mla_w_in[j], mla_q_norm_g[j], mla_kv_norm_g[j], mla_w_uq[j],
                                   mla_w_ukv[j], mla_q_g[j], mla_k_g[j], mla_w_out[j], want_ctx)
        h = h + g1 * dl
        if want_ctx:
            hc = hc + cg1 * dc

        xl = modulate(rms_norm(h, norm2_g[i]), sh2, sc2).reshape(-1, dm)
        moe_args = (moe_w_group[i], moe_b_group[i], moe_w_router[i], moe_b_router[i],
                    moe_w1[i], moe_w3[i], moe_w2[i])
        if want_ctx:
            xc = modulate(rms_norm(hc, norm2_g[i]), csh2, csc2).reshape(-1, dm)
            y = hier_moe(jnp.concatenate([xl, xc], axis=0), *moe_args)
            h = h + g2 * y[:b * s].reshape(b, s, dm)
            hc = hc + cg2 * y[b * s:].reshape(hc.shape)
        else:
            h = h + g2 * hier_moe(xl, *moe_args).reshape(b, s, dm)
    return h
```

```python
import functools
import math

import numpy as np
import jax
import jax.numpy as jnp
from jax import lax
from jax.experimental import pallas as pl
from jax.experimental.pallas import tpu as pltpu

F32 = jnp.float32
BF = jnp.bfloat16
HIGHEST = lax.Precision.HIGHEST

D = 1024
CTX = 256
GRID_W = 64
EPS = 1e-6
ROPE_BASE = 10000.0
DEPTH = 4
LANES = 128
TM = 256

DA_HEADS = 8
DA_HEAD_DIM = 64

GM_CHUNK = 128
GM_GROUPS = 8

SSM_INNER = 2048
SSM_HEADS = 32
SSM_GROUPS = 4
SSM_STATE = 128
SSM_CONV = 5
SSM_CHUNK = 128
SSM_CONV_DIM = SSM_INNER + 2 * SSM_GROUPS * SSM_STATE

MLA_HEADS = 16
MLA_Q_RANK = 384
MLA_KV_RANK = 256
MLA_NOPE = 64
MLA_ROPE = 32
MLA_V = 64
MLA_QK = MLA_NOPE + MLA_ROPE

MOE_GROUPS = 4
MOE_PER_GROUP = 8
MOE_EXPERTS = 32
MOE_FF = 512
MOE_TM = 256

VMEM_LIMIT = 56 * 1024 * 1024


def _cparams(sem):
    return pltpu.CompilerParams(dimension_semantics=sem, vmem_limit_bytes=VMEM_LIMIT)


def _full_spec(arr):
    nd = arr.ndim
    return pl.BlockSpec(arr.shape, lambda *_: (0,) * nd)


def _tok_spec(width, col=0):
    return pl.BlockSpec((None, TM, width), lambda b, t: (b, t, col))


def _mod_spec():
    return pl.BlockSpec((None, None, 8, D), lambda b, t: (b, jnp.minimum(t, 1), 0, 0))


def _silu(x):
    return x * (1.0 / (1.0 + jnp.exp(-x)))


def _norm_mod(h, g, scale, shift):
    ms = jnp.mean(h * h, axis=-1, keepdims=True)
    y = h * lax.rsqrt(ms + EPS) * g
    return y * (1.0 + scale) + shift


def _ada_kernel(c_ref, w_ref, b_ref, o_ref):
    a = _silu(c_ref[...]).astype(BF)
    o_ref[...] = jnp.dot(a, w_ref[...].astype(BF), preferred_element_type=F32) + b_ref[...]


def _ada_all(cc, ada_w, ada_b):
    rows = cc.shape[0]
    tn = 1536
    return pl.pallas_call(
        _ada_kernel,
        grid=(DEPTH, 6 * D // tn),
        in_specs=[pl.BlockSpec((rows, D), lambda l, j: (0, 0)),
                  pl.BlockSpec((None, D, tn), lambda l, j: (l, 0, j)),
                  pl.BlockSpec((None, 1, tn), lambda l, j: (l, 0, j))],
        out_specs=pl.BlockSpec((None, rows, tn), lambda l, j: (l, 0, j)),
        out_shape=jax.ShapeDtypeStruct((DEPTH, rows, 6 * D), F32),
        compiler_params=_cparams(("arbitrary", "arbitrary")),
        name="ada_mod",
    )(cc, ada_w, ada_b.reshape(DEPTH, 1, 6 * D))


def _residual_and_route(delta, h_ref, mod, n2g_ref, wr_ref, br_ref, ho_ref, xo_ref, id_ref, wt_ref):
    hn = h_ref[...] + mod[2:3] * delta
    ho_ref[...] = hn
    x2 = _norm_mod(hn, n2g_ref[...], mod[4:5], mod[3:4])
    xo_ref[...] = x2.astype(BF)
    lg = jnp.dot(x2, wr_ref[...], preferred_element_type=F32, precision=HIGHEST) + br_ref[...]
    lane = lax.broadcasted_iota(jnp.int32, lg.shape, 1)
    neg = jnp.float32(-jnp.inf)
    big = jnp.int32(1 << 20)
    is_g = jnp.logical_and(lane >= MOE_EXPERTS, lane < MOE_EXPERTS + MOE_GROUPS)
    gl = jnp.where(is_g, lg, neg)
    gmax = jnp.max(gl, axis=-1, keepdims=True)
    gidx = jnp.min(jnp.where(gl == gmax, lane, big), axis=-1, keepdims=True) - MOE_EXPERTS
    pg = 1.0 / jnp.sum(jnp.where(is_g, jnp.exp(gl - gmax), 0.0), axis=-1, keepdims=True)
    lo = gidx * MOE_PER_GROUP
    in_grp = jnp.logical_and(lane >= lo, lane < lo + MOE_PER_GROUP)
    el = jnp.where(in_grp, lg, neg)
    e1 = jnp.max(el, axis=-1, keepdims=True)
    i1 = jnp.min(jnp.where(el == e1, lane, big), axis=-1, keepdims=True)
    el2 = jnp.where(lane == i1, neg, el)
    e2 = jnp.max(el2, axis=-1, keepdims=True)
    i2 = jnp.min(jnp.where(el2 == e2, lane, big), axis=-1, keepdims=True)
    t = jnp.exp(e2 - e1)
    w1 = pg / (1.0 + t)
    w2 = pg * t / (1.0 + t)
    id_ref[...] = jnp.where(lane == 0, i1, jnp.where(lane == 1, i2, 0))
    wt_ref[...] = jnp.where(lane == 0, w1, jnp.where(lane == 1, w2, 0.0))


def _tail_in_specs(n2g, wr, br):
    return [_full_spec(n2g), _full_spec(wr), _full_spec(br)]


def _tail_out(batch, t_all):
    specs = [_tok_spec(D), _tok_spec(D), _tok_spec(LANES), _tok_spec(LANES)]
    shapes = [jax.ShapeDtypeStruct((batch, t_all, D), F32),
              jax.ShapeDtypeStruct((batch, t_all, D), BF),
              jax.ShapeDtypeStruct((batch, t_all, LANES), jnp.int32),
              jax.ShapeDtypeStruct((batch, t_all, LANES), F32)]
    return specs, shapes


def _outproj_kernel(a_ref, w_ref, h_ref, mod_ref, n2g_ref, wr_ref, br_ref,
                    ho_ref, xo_ref, id_ref, wt_ref):
    delta = jnp.dot(a_ref[...], w_ref[...], preferred_element_type=F32)
    _residual_and_route(delta, h_ref, mod_ref[...], n2g_ref, wr_ref, br_ref,
                        ho_ref, xo_ref, id_ref, wt_ref)


def _outproj(a, w, h, mod, n2g, wr, br):
    batch, t_all, k = a.shape
    out_specs, out_shapes = _tail_out(batch, t_all)
    return pl.pallas_call(
        _outproj_kernel,
        grid=(batch, t_all // TM),
        in_specs=[_tok_spec(k), _full_spec(w), _tok_spec(D), _mod_spec()] + _tail_in_specs(n2g, wr, br),
        out_specs=out_specs,
        out_shape=out_shapes,
        compiler_params=_cparams(("parallel", "parallel")),
        name="outproj_route",
    )(a, w, h, mod, n2g, wr, br)


def _rope_tables(t_lat, rot_dim):
    rows = t_lat // GRID_W
    row = np.repeat(np.arange(rows, dtype=np.float32), GRID_W)
    col = np.tile(np.arange(GRID_W, dtype=np.float32), rows)
    n_freq = rot_dim // 4
    inv_freq = jnp.asarray(ROPE_BASE, F32) ** (-jnp.arange(n_freq, dtype=F32) / n_freq)
    ang = jnp.concatenate([jnp.asarray(row)[:, None] * inv_freq, jnp.asarray(col)[:, None] * inv_freq], axis=-1)
    return jnp.cos(ang), jnp.sin(ang)


def _with_ctx_identity(cos_l, sin_l):
    cos = jnp.concatenate([jnp.ones((CTX, LANES), F32), cos_l], axis=0)
    sin = jnp.concatenate([jnp.zeros((CTX, LANES), F32), sin_l], axis=0)
    return cos, sin


def _da_perm():
    perm = np.zeros(2 * DA_HEADS * DA_HEAD_DIM, np.int32)
    for h in range(DA_HEADS):
        for m in range(2):
            for i in range(DA_HEAD_DIM):
                new = h * LANES + (i // 32) * 64 + m * 32 + (i % 32)
                perm[new] = (2 * h + m) * DA_HEAD_DIM + i
    return perm


def _da_lane_of_dim():
    lane = np.arange(LANES)
    return (lane // 64) * 32 + lane % 32


def _da_proj_kernel(h_ref, mod_ref, n1g_ref, w_ref, gq_ref, gk_ref, cos_ref, sin_ref, gavg_ref,
                    q_ref, k_ref, v_ref):
    mod = mod_ref[...]
    xn = _norm_mod(h_ref[...], n1g_ref[...], mod[1:2], mod[0:1]).astype(BF)
    cos = cos_ref[...]
    sin = sin_ref[...]
    gavg = gavg_ref[...]
    nq = DA_HEADS * LANES
    scale = DA_HEAD_DIM ** -0.5
    for dst, off, g_ref, mult in ((q_ref, 0, gq_ref, scale), (k_ref, nq, gk_ref, 1.0)):
        y_all = jnp.dot(xn, w_ref[:, off:off + nq], preferred_element_type=F32)
        for j in range(DA_HEADS):
            y = y_all[:, j * LANES:(j + 1) * LANES]
            ms = jnp.dot(y * y, gavg, preferred_element_type=F32, precision=HIGHEST)
            y = y * lax.rsqrt(ms + EPS) * g_ref[...]
            y = y * cos + pltpu.roll(y, 64, 1) * sin
            dst[:, j * LANES:(j + 1) * LANES] = (y * mult).astype(BF)
    v_ref[...] = jnp.dot(xn, w_ref[:, 2 * nq:3 * nq], preferred_element_type=F32).astype(BF)


def _da_attn_kernel(lam_ref, q_ref, k_ref, v_ref, subg_ref, o_ref, *, lam_init):
    qi = pl.program_id(2)
    q = q_ref[...]
    k = k_ref[...]
    v = v_ref[...]
    t_all = k.shape[0]
    lane = lax.broadcasted_iota(jnp.int32, q.shape, 1)
    comp0 = ((lane // 32) % 2) == 0
    zero = jnp.zeros_like(q)
    col = lax.broadcasted_iota(jnp.int32, (TM, t_all), 1)
    valid = jnp.logical_or(qi > 0, col < CTX)

    def attend(qm):
        s = lax.dot_general(qm, k, (((1,), (1,)), ((), ())), preferred_element_type=F32)
        s = jnp.where(valid, s, -1e30)
        e = jnp.exp(s - jnp.max(s, axis=-1, keepdims=True))
        l = jnp.sum(e, axis=-1, keepdims=True)
        return jnp.dot(e.astype(BF), v, preferred_element_type=F32) / l

    o = attend(jnp.where(comp0, q, zero)) - lam_ref[0] * attend(jnp.where(comp0, zero, q))
    ms = jnp.mean(o * o, axis=-1, keepdims=True)
    o = o * lax.rsqrt(ms + EPS) * subg_ref[...] * (1.0 - lam_init)
    o_ref[...] = o.astype(BF)


def _diff_attention(h, mod, n1g, w_in, q_g, k_g, lam_q1, lam_k1, lam_q2, lam_k2, sub_g, layer_idx):
    batch, t_all, _ = h.shape
    nq = DA_HEADS * LANES
    perm = _da_perm()
    w = jnp.concatenate([w_in[:, :nq][:, perm], w_in[:, nq:2 * nq][:, perm], w_in[:, 2 * nq:]], axis=1).astype(BF)
    ldim = _da_lane_of_dim()
    gq = q_g[ldim].reshape(1, LANES)
    gk = k_g[ldim].reshape(1, LANES)
    cos, sin = _rope_tables(t_all - CTX, DA_HEAD_DIM)
    cos, sin = _with_ctx_identity(jnp.tile(cos, (1, 4)), jnp.concatenate([-sin, -sin, sin, sin], axis=1))
    comp = (np.arange(LANES) // 32) % 2
    gavg = jnp.asarray((comp[:, None] == comp[None, :]).astype(np.float32) / DA_HEAD_DIM)
    tab_spec = pl.BlockSpec((TM, LANES), lambda b, t: (t, 0))
    qkv_shape = jax.ShapeDtypeStruct((batch, t_all, nq), BF)
    q, k, v = pl.pallas_call(
        _da_proj_kernel,
        grid=(batch, t_all // TM),
        in_specs=[_tok_spec(D), _mod_spec(), _full_spec(n1g), _full_spec(w), _full_spec(gq), _full_spec(gk),
                  tab_spec, tab_spec, _full_spec(gavg)],
        out_specs=[_tok_spec(nq)] * 3,
        out_shape=[qkv_shape] * 3,
        compiler_params=_cparams(("parallel", "parallel")),
        name="da_proj",
    )(h, mod, n1g, w, gq, gk, cos, sin, gavg)

    lam_init = 0.8 - 0.6 * math.exp(-0.3 * layer_idx)
    lam = (jnp.exp(jnp.sum(lam_q1 * lam_k1)) - jnp.exp(jnp.sum(lam_q2 * lam_k2)) + lam_init).reshape(1)
    kv_spec = pl.BlockSpec((None, t_all, LANES), lambda b, hh, t: (b, 0, hh))
    qo_spec = pl.BlockSpec((None, TM, LANES), lambda b, hh, t: (b, t, hh))
    subg = sub_g.reshape(1, LANES)
    return pl.pallas_call(
        functools.partial(_da_attn_kernel, lam_init=lam_init),
        grid=(batch, DA_HEADS, t_all // TM),
        in_specs=[pl.BlockSpec(memory_space=pltpu.SMEM), qo_spec, kv_spec, kv_spec,
                  pl.BlockSpec((1, LANES), lambda b, hh, t: (0, 0))],
        out_specs=qo_spec,
        out_shape=jax.ShapeDtypeStruct((batch, t_all, nq), BF),
        compiler_params=_cparams(("parallel", "parallel", "arbitrary")),
        name="da_attn",
    )(lam, q, k, v, subg)


def _gmlp_kernel(h_ref, mod_ref, n1g_ref, win_ref, vg_ref, ws_ref, bs_ref, wout_ref,
                 n2g_ref, wr_ref, br_ref, ho_ref, xo_ref, id_ref, wt_ref):
    mod = mod_ref[...]
    xn = _norm_mod(h_ref[...], n1g_ref[...], mod[1:2], mod[0:1]).astype(BF)
    uv = jnp.dot(xn, win_ref[...], preferred_element_type=F32)
    uv = 0.5 * uv * (1.0 + lax.erf(uv * (2.0 ** -0.5)))
    u = uv[:, :D]
    v = uv[:, D:]
    v = (v * lax.rsqrt(jnp.mean(v * v, axis=-1, keepdims=True) + EPS) * vg_ref[...]).astype(BF)
    rows = []
    for c in range(TM // GM_CHUNK):
        cols = []
        for g in range(GM_GROUPS):
            vb = v[c * GM_CHUNK:(c + 1) * GM_CHUNK, g * LANES:(g + 1) * LANES]
            s = jnp.dot(ws_ref[g], vb, preferred_element_type=F32) + bs_ref[g]
            cols.append(s)
        rows.append(jnp.concatenate(cols, axis=1))
    s_all = jnp.concatenate(rows, axis=0)
    delta = jnp.dot((u * s_all).astype(BF), wout_ref[...], preferred_element_type=F32)
    _residual_and_route(delta, h_ref, mod, n2g_ref, wr_ref, br_ref, ho_ref, xo_ref, id_ref, wt_ref)


def _gmlp_layer(h, mod, n1g, w_in, v_g, w_s, b_s, w_out, n2g, wr, br):
    batch, t_all, _ = h.shape
    win = w_in.astype(BF)
    wout = w_out.astype(BF)
    ws = w_s.astype(BF)
    bs = jnp.broadcast_to(b_s[:, :, None], (GM_GROUPS, GM_CHUNK, LANES)).astype(F32)
    vg = v_g.reshape(1, D)
    out_specs, out_shapes = _tail_out(batch, t_all)
    return pl.pallas_call(
        _gmlp_kernel,
        grid=(batch, t_all // TM),
        in_specs=[_tok_spec(D), _mod_spec(), _full_spec(n1g), _full_spec(win), _full_spec(vg), _full_spec(ws),
                  _full_spec(bs), _full_spec(wout)] + _tail_in_specs(n2g, wr, br),
        out_specs=out_specs,
        out_shape=out_shapes,
        compiler_params=_cparams(("parallel", "parallel")),
        name="gmlp_layer",
    )(h, mod, n1g, win, vg, ws, bs, wout, n2g, wr, br)


SSM_DT_PAD = LANES


def _ssm_proj_kernel(h_ref, mod_ref, n1g_ref, w_ref, dtb_ref, z_ref, xbc_ref, dt_ref):
    mod = mod_ref[...]
    xn = _norm_mod(h_ref[...], n1g_ref[...], mod[1:2], mod[0:1]).astype(BF)
    z_ref[...] = jnp.dot(xn, w_ref[:, :SSM_INNER], preferred_element_type=F32).astype(BF)
    c0 = SSM_INNER
    for j in range(SSM_CONV_DIM // 1024):
        xbc_ref[:, j * 1024:(j + 1) * 1024] = jnp.dot(
            xn, w_ref[:, c0 + j * 1024:c0 + (j + 1) * 1024], preferred_element_type=F32).astype(BF)
    c1 = SSM_INNER + SSM_CONV_DIM
    raw = jnp.dot(xn, w_ref[:, c1:c1 + SSM_DT_PAD], preferred_element_type=F32) + dtb_ref[...]
    dt_ref[...] = jnp.maximum(raw, 0.0) + jnp.log1p(jnp.exp(-jnp.abs(raw)))


CONV_PAD = 8
CONV_W = 512


def _conv_kernel(x_ref, w_ref, b_ref, o_ref, pad_ref):
    t_all = x_ref.shape[0]
    t_lat = t_all - CTX
    lat0 = CTX + 2 * CONV_PAD
    zeros = jnp.zeros((CONV_PAD, CONV_W), F32)
    pad_ref[0:CONV_PAD, :] = zeros
    pad_ref[CONV_PAD + CTX:lat0, :] = jnp.zeros((CONV_PAD, CONV_W), F32)
    pad_ref[lat0 + t_lat:lat0 + t_lat + CONV_PAD, :] = zeros
    pad_ref[CONV_PAD:CONV_PAD + CTX, :] = x_ref[0:CTX, :].astype(F32)
    pad_ref[lat0:lat0 + t_lat, :] = x_ref[CTX:t_all, :].astype(F32)
    half = (SSM_CONV - 1) // 2
    for start, n, dst in ((CONV_PAD, CTX, 0), (lat0, t_lat, CTX)):
        acc = jnp.zeros((n, CONV_W), F32) + b_ref[...]
        for kk in range(SSM_CONV):
            acc = acc + pad_ref[start + kk - half:start + kk - half + n, :] * w_ref[kk:kk + 1, :]
        o_ref[dst:dst + n, :] = _silu(acc).astype(BF)


def _ssd_kernel(dt_ref, a_ref, x_ref, b_ref, c_ref, y_ref, s_ref):
    direction = pl.program_id(1)
    step = pl.program_id(3)

    @pl.when(step == 0)
    def _():
        s_ref[...] = jnp.zeros_like(s_ref)

    q = SSM_CHUNK
    row = lax.broadcasted_iota(jnp.int32, (q, q), 0)
    col = lax.broadcasted_iota(jnp.int32, (q, q), 1)
    sign = 1 - 2 * direction
    tri_b = (col - row) * sign <= 0
    tri = tri_b.astype(F32)
    tri_t = ((row - col) * sign <= 0).astype(F32)
    eye = (row == col).astype(F32)
    lane_lo = col < 64

    dt_t = dt_ref[...]
    a_t = dt_t * a_ref[...]
    cs_rows = jnp.dot(a_t, tri_t, preferred_element_type=F32, precision=HIGHEST)
    tot = jnp.sum(a_t, axis=1, keepdims=True)
    bm = b_ref[...]
    cm = c_ref[...]
    cb = lax.dot_general(cm, bm, (((1,), (1,)), ((), ())), preferred_element_type=F32)
    bm_t = bm.astype(F32).T.astype(BF)

    for p in range(4):
        xp = x_ref[:, p * LANES:(p + 1) * LANES].astype(F32)
        cs_col, dt_col, mats = [], [], []
        for hh in range(2):
            e = 2 * p + hh
            cc = jnp.sum(tri * a_t[e:e + 1, :], axis=1, keepdims=True)
            cs_col.append(cc)
            dt_col.append(jnp.sum(eye * dt_t[e:e + 1, :], axis=1, keepdims=True))
            decay = jnp.exp(jnp.where(tri_b, cc - cs_rows[e:e + 1, :], -jnp.inf))
            mats.append((cb * decay).astype(BF))
        xdt = xp * jnp.where(lane_lo, dt_col[0], dt_col[1])
        xb = xdt.astype(BF)
        y_diag = jnp.where(lane_lo,
                           jnp.dot(mats[0], xb, preferred_element_type=F32),
                           jnp.dot(mats[1], xb, preferred_element_type=F32))
        s_prev = s_ref[p]
        y_off = jnp.dot(cm, s_prev.astype(BF), preferred_element_type=F32)
        y_off = y_off * jnp.where(lane_lo, jnp.exp(cs_col[0]), jnp.exp(cs_col[1]))
        y_ref[:, p * LANES:(p + 1) * LANES] = y_diag + y_off
        t0 = tot[2 * p:2 * p + 1, :]
        t1 = tot[2 * p + 1:2 * p + 2, :]
        to_end = jnp.where(lane_lo, jnp.exp(t0 - cs_col[0]), jnp.exp(t1 - cs_col[1]))
        upd = jnp.dot(bm_t, (xdt * to_end).astype(BF), preferred_element_type=F32)
        s_ref[p] = s_prev * jnp.where(lane_lo[0:1, :], jnp.exp(t0), jnp.exp(t1)) + upd


def _ssm_finish_kernel(yf_ref, yb_ref, xs_ref, z_ref, d_ref, og_ref, w_ref, h_ref, mod_ref,
                       n2g_ref, wr_ref, br_ref, ho_ref, xo_ref, id_ref, wt_ref):
    y = yf_ref[...] + yb_ref[...] + d_ref[...] * xs_ref[...].astype(F32)
    y = y * _silu(z_ref[...].astype(F32))
    y = y * lax.rsqrt(jnp.mean(y * y, axis=-1, keepdims=True) + EPS) * og_ref[...]
    delta = jnp.dot(y.astype(BF), w_ref[...], preferred_element_type=F32)
    _residual_and_route(delta, h_ref, mod_ref[...], n2g_ref, wr_ref, br_ref, ho_ref, xo_ref, id_ref, wt_ref)


def _mamba_layer(h, mod, n1g, w_in, conv_w, conv_b, dt_bias, a_log, d_skip, out_g, w_out, n2g, wr, br):
    batch, t_all, _ = h.shape
    n_dt = 2 * SSM_HEADS
    c1 = SSM_INNER + SSM_CONV_DIM
    w = jnp.concatenate([w_in, jnp.zeros((D, SSM_DT_PAD - n_dt), F32)], axis=1).astype(BF)
    dtb = jnp.concatenate([dt_bias.reshape(n_dt), jnp.zeros((SSM_DT_PAD - n_dt,), F32)]).reshape(1, SSM_DT_PAD)
    z, xbc, dt = pl.pallas_call(
        _ssm_proj_kernel,
        grid=(batch, t_all // TM),
        in_specs=[_tok_spec(D), _mod_spec(), _full_spec(n1g), _full_spec(w), _full_spec(dtb)],
        out_specs=[_tok_spec(SSM_INNER), _tok_spec(SSM_CONV_DIM), _tok_spec(SSM_DT_PAD)],
        out_shape=[jax.ShapeDtypeStruct((batch, t_all, SSM_INNER), BF),
                   jax.ShapeDtypeStruct((batch, t_all, SSM_CONV_DIM), BF),
                   jax.ShapeDtypeStruct((batch, t_all, SSM_DT_PAD), F32)],
        compiler_params=_cparams(("parallel", "parallel")),
        name="ssm_proj",
    )(h, mod, n1g, w, dtb)
    assert c1 + n_dt == w_in.shape[1]

    cw = jnp.concatenate([conv_w, jnp.zeros((8 - SSM_CONV, SSM_CONV_DIM), F32)], axis=0)
    cbias = conv_b.reshape(1, SSM_CONV_DIM)
    xbc = pl.pallas_call(
        _conv_kernel,
        grid=(batch, SSM_CONV_DIM // CONV_W),
        in_specs=[pl.BlockSpec((None, t_all, CONV_W), lambda b, j: (b, 0, j)),
                  pl.BlockSpec((8, CONV_W), lambda b, j: (0, j)),
                  pl.BlockSpec((1, CONV_W), lambda b, j: (0, j))],
        out_specs=pl.BlockSpec((None, t_all, CONV_W), lambda b, j: (b, 0, j)),
        out_shape=jax.ShapeDtypeStruct((batch, t_all, SSM_CONV_DIM), BF),
        scratch_shapes=[pltpu.VMEM((t_all + 3 * CONV_PAD, CONV_W), F32)],
        compiler_params=_cparams(("parallel", "parallel")),
        name="ssm_conv",
    )(xbc, cw, cbias)

    n_chunks = t_all // SSM_CHUNK
    ctx_chunks = CTX // SSM_CHUNK
    e_per_g = SSM_HEADS // SSM_GROUPS
    dt_t = jnp.swapaxes(dt[:, :, :n_dt], 1, 2)
    a_neg = -jnp.exp(a_log.astype(F32)).reshape(2, SSM_GROUPS, e_per_g, 1)
    a_neg = jnp.broadcast_to(a_neg, (2, SSM_GROUPS, e_per_g, SSM_CHUNK))

    def chunk_of(dr, s):
        rev = jnp.where(s < ctx_chunks, ctx_chunks - 1 - s, n_chunks - 1 + ctx_chunks - s)
        return jnp.where(dr == 0, s, rev)

    gw = e_per_g * 64
    y = pl.pallas_call(
        _ssd_kernel,
        grid=(batch, 2, SSM_GROUPS, n_chunks),
        in_specs=[pl.BlockSpec((None, e_per_g, SSM_CHUNK), lambda b, dr, g, s: (b, dr * SSM_GROUPS + g, chunk_of(dr, s))),
                  pl.BlockSpec((None, None, e_per_g, SSM_CHUNK), lambda b, dr, g, s: (dr, g, 0, 0)),
                  pl.BlockSpec((None, SSM_CHUNK, gw), lambda b, dr, g, s: (b, chunk_of(dr, s), g)),
                  pl.BlockSpec((None, SSM_CHUNK, SSM_STATE),
                               lambda b, dr, g, s: (b, chunk_of(dr, s), SSM_INNER // SSM_STATE + g)),
                  pl.BlockSpec((None, SSM_CHUNK, SSM_STATE),
                               lambda b, dr, g, s: (b, chunk_of(dr, s), SSM_INNER // SSM_STATE + SSM_GROUPS + g))],
        out_specs=pl.BlockSpec((None, None, SSM_CHUNK, gw), lambda b, dr, g, s: (dr, b, chunk_of(dr, s), g)),
        out_shape=jax.ShapeDtypeStruct((2, batch, t_all, SSM_INNER), F32),
        scratch_shapes=[pltpu.VMEM((4, SSM_STATE, LANES), F32)],
        compiler_params=_cparams(("parallel", "parallel", "parallel", "arbitrary")),
        name="ssd_scan",
    )(dt_t, a_neg, xbc, xbc, xbc)

    dvec = jnp.repeat(d_skip, SSM_INNER // SSM_HEADS).reshape(1, SSM_INNER)
    og = out_g.reshape(1, SSM_INNER)
    wo = w_out.astype(BF)
    out_specs, out_shapes = _tail_out(batch, t_all)
    y_spec = lambda dr: pl.BlockSpec((None, None, TM, SSM_INNER), lambda b, t: (dr, b, t, 0))
    return pl.pallas_call(
        _ssm_finish_kernel,
        grid=(batch, t_all // TM),
        in_specs=[y_spec(0), y_spec(1), _tok_spec(SSM_INNER), _tok_spec(SSM_INNER), _full_spec(dvec),
                  _full_spec(og), _full_spec(wo), _tok_spec(D), _mod_spec()] + _tail_in_specs(n2g, wr, br),
        out_specs=out_specs,
        out_shape=out_shapes,
        compiler_params=_cparams(("parallel", "parallel")),
        name="ssm_finish",
    )(y, y, xbc, z, dvec, og, wo, h, mod, n2g, wr, br)


def _mla_lane_src():
    src = np.full(LANES, -1, np.int32)
    src[0:16] = MLA_NOPE + np.arange(16)
    src[16:64] = np.arange(48)
    src[64:80] = MLA_NOPE + 16 + np.arange(16)
    src[80:96] = 48 + np.arange(16)
    return src


def _mla_proj_kernel(h_ref, mod_ref, n1g_ref, win_ref, qng_ref, kvng_ref, wuq_ref, wuk_ref, wuv_ref,
                     gq_ref, gk_ref, cos_ref, sin_ref, avg_ref, q_ref, k_ref, v_ref):
    mod = mod_ref[...]
    xn = _norm_mod(h_ref[...], n1g_ref[...], mod[1:2], mod[0:1]).astype(BF)
    lat = jnp.dot(xn, win_ref[...], preferred_element_type=F32)
    cq = lat[:, :MLA_Q_RANK]
    ckv = lat[:, MLA_Q_RANK:MLA_Q_RANK + MLA_KV_RANK]
    kpe = lat[:, MLA_Q_RANK + MLA_KV_RANK:]
    cq = (cq * lax.rsqrt(jnp.mean(cq * cq, axis=-1, keepdims=True) + EPS) * qng_ref[...]).astype(BF)
    ckv = (ckv * lax.rsqrt(jnp.mean(ckv * ckv, axis=-1, keepdims=True) + EPS) * kvng_ref[...]).astype(BF)
    cos = cos_ref[...]
    sin = sin_ref[...]
    avg = avg_ref[...]
    scale = MLA_QK ** -0.5
    qa = jnp.dot(cq, wuq_ref[...], preferred_element_type=F32)
    ka = jnp.dot(ckv, wuk_ref[...], preferred_element_type=F32)
    for j in range(MLA_HEADS):
        for dst, src, add, g_ref, mult in ((q_ref, qa, None, gq_ref, scale), (k_ref, ka, kpe, gk_ref, 1.0)):
            y = src[:, j * LANES:(j + 1) * LANES]
            if add is not None:
                y = y + add
            ms = jnp.dot(y * y, avg, preferred_element_type=F32, precision=HIGHEST)
            y = y * lax.rsqrt(ms + EPS) * g_ref[...]
            y = y * cos + pltpu.roll(y, 64, 1) * sin
            dst[:, j * LANES:(j + 1) * LANES] = (y * mult).astype(BF)
    v_ref[...] = jnp.dot(ckv, wuv_ref[...], preferred_element_type=F32).astype(BF)


def _mla_attn_kernel(q_ref, k_ref, v_ref, o_ref):
    qi = pl.program_id(2)
    v = v_ref[...]
    t_all = v.shape[0]
    col = lax.broadcasted_iota(jnp.int32, (TM, t_all), 1)
    valid = jnp.logical_or(qi > 0, col < CTX)
    outs = []
    for hh in range(2):
        q = q_ref[:, hh * LANES:(hh + 1) * LANES]
        k = k_ref[:, hh * LANES:(hh + 1) * LANES]
        s = lax.dot_general(q, k, (((1,), (1,)), ((), ())), preferred_element_type=F32)
        s = jnp.where(valid, s, -1e30)
        e = jnp.exp(s - jnp.max(s, axis=-1, keepdims=True))
        l = jnp.sum(e, axis=-1, keepdims=True)
        outs.append(jnp.dot(e.astype(BF), v, preferred_element_type=F32) / l)
    lane = lax.broadcasted_iota(jnp.int32, (TM, LANES), 1)
    o_ref[...] = jnp.where(lane < MLA_V, outs[0], outs[1]).astype(BF)


def _mla_attention(h, mod, n1g, w_in, q_norm_g, kv_norm_g, w_uq, w_ukv, q_g, k_g):
    batch, t_all, _ = h.shape
    src = _mla_lane_src()
    used = src >= 0
    srcc = np.where(used, src, 0)
    nh = MLA_HEADS * LANES
    kpe_cols = np.where(srcc >= MLA_NOPE, MLA_Q_RANK + MLA_KV_RANK + srcc - MLA_NOPE, 0)
    kpe_used = np.logical_and(used, src >= MLA_NOPE)
    w_kpe = jnp.where(jnp.asarray(kpe_used)[None, :], w_in[:, kpe_cols], 0.0)
    win = jnp.concatenate([w_in[:, :MLA_Q_RANK + MLA_KV_RANK], w_kpe], axis=1).astype(BF)
    q_cols = (np.arange(MLA_HEADS)[:, None] * MLA_QK + srcc[None, :]).reshape(-1)
    q_mask = jnp.asarray(np.tile(used, MLA_HEADS))[None, :]
    wuq = jnp.where(q_mask, w_uq[:, q_cols], 0.0).astype(BF)
    nope_used = np.logical_and(used, src < MLA_NOPE)
    k_cols = (np.arange(MLA_HEADS)[:, None] * (MLA_NOPE + MLA_V) + np.where(nope_used, srcc, 0)[None, :]).reshape(-1)
    k_mask = jnp.asarray(np.tile(nope_used, MLA_HEADS))[None, :]
    wuk = jnp.where(k_mask, w_ukv[:, k_cols], 0.0).astype(BF)
    v_cols = (np.arange(MLA_HEADS)[:, None] * (MLA_NOPE + MLA_V) + MLA_NOPE + np.arange(MLA_V)[None, :]).reshape(-1)
    wuv = w_ukv[:, v_cols].astype(BF)
    gq = jnp.where(jnp.asarray(used), q_g[srcc], 0.0).reshape(1, LANES)
    gk = jnp.where(jnp.asarray(used), k_g[srcc], 0.0).reshape(1, LANES)
    cos16, sin16 = _rope_tables(t_all - CTX, MLA_ROPE)
    t_lat = t_all - CTX
    cos_l = jnp.ones((t_lat, LANES), F32).at[:, 0:16].set(cos16).at[:, 64:80].set(cos16)
    sin_l = jnp.zeros((t_lat, LANES), F32).at[:, 0:16].set(-sin16).at[:, 64:80].set(sin16)
    cos, sin = _with_ctx_identity(cos_l, sin_l)
    avg = jnp.full((LANES, LANES), 1.0 / MLA_QK, F32)
    qng = q_norm_g.reshape(1, MLA_Q_RANK)
    kvng = kv_norm_g.reshape(1, MLA_KV_RANK)
    tab_spec = pl.BlockSpec((TM, LANES), lambda b, t: (t, 0))
    q, k, v = pl.pallas_call(
        _mla_proj_kernel,
        grid=(batch, t_all // TM),
        in_specs=[_tok_spec(D), _mod_spec(), _full_spec(n1g), _full_spec(win), _full_spec(qng), _full_spec(kvng),
                  _full_spec(wuq), _full_spec(wuk), _full_spec(wuv), _full_spec(gq), _full_spec(gk),
                  tab_spec, tab_spec, _full_spec(avg)],
        out_specs=[_tok_spec(nh), _tok_spec(nh), _tok_spec(MLA_HEADS * MLA_V)],
        out_shape=[jax.ShapeDtypeStruct((batch, t_all, nh), BF), jax.ShapeDtypeStruct((batch, t_all, nh), BF),
                   jax.ShapeDtypeStruct((batch, t_all, MLA_HEADS * MLA_V), BF)],
        compiler_params=_cparams(("parallel", "parallel")),
        name="mla_proj",
    )(h, mod, n1g, win, qng, kvng, wuq, wuk, wuv, gq, gk, cos, sin, avg)

    return pl.pallas_call(
        _mla_attn_kernel,
        grid=(batch, MLA_HEADS // 2, t_all // TM),
        in_specs=[pl.BlockSpec((None, TM, 2 * LANES), lambda b, p, t: (b, t, p)),
                  pl.BlockSpec((None, t_all, 2 * LANES), lambda b, p, t: (b, 0, p)),
                  pl.BlockSpec((None, t_all, LANES), lambda b, p, t: (b, 0, p))],
        out_specs=pl.BlockSpec((None, TM, LANES), lambda b, p, t: (b, t, p)),
        out_shape=jax.ShapeDtypeStruct((batch, t_all, MLA_HEADS * MLA_V), BF),
        compiler_params=_cparams(("parallel", "parallel", "arbitrary")),
        name="mla_attn",
    )(q, k, v)


def _moe_kernel(te_ref, nt_ref, x_ref, w1_ref, w3_ref, w2_ref, o_ref):
    i = pl.program_id(0)

    @pl.when(i < nt_ref[0])
    def _():
        x = x_ref[...]
        h1 = jnp.dot(x, w1_ref[...], preferred_element_type=F32)
        h3 = jnp.dot(x, w3_ref[...], preferred_element_type=F32)
        hid = (_silu(h1) * h3).astype(BF)
        o_ref[...] = jnp.dot(hid, w2_ref[...], preferred_element_type=F32)

    @pl.when(i >= nt_ref[0])
    def _():
        o_ref[...] = jnp.zeros_like(o_ref)


def _moe(xn, ids, wts, w1, w3, w2):
    batch, t_all, _ = xn.shape
    n_tok = batch * t_all
    n_pair = 2 * n_tok
    n_tiles = (n_pair + MOE_EXPERTS * (MOE_TM - 1) + MOE_TM - 1) // MOE_TM
    x = xn.reshape(n_tok, D)
    e = ids[..., :2].reshape(n_pair)
    w = wts[..., :2].reshape(n_tok, 2)
    onehot = (e[:, None] == jnp.arange(MOE_EXPERTS, dtype=jnp.int32)[None, :]).astype(jnp.int32)
    csum = jnp.cumsum(onehot, axis=0)
    rank = jnp.sum(onehot * (csum - 1), axis=1)
    counts = csum[-1]
    tiles_e = (counts + MOE_TM - 1) // MOE_TM
    tile_end = jnp.cumsum(tiles_e)
    tile_start = tile_end - tiles_e
    dest = tile_start[e] * MOE_TM + rank
    n_valid = tile_end[-1]
    tidx = jnp.arange(n_tiles, dtype=jnp.int32)
    te = jnp.minimum(jnp.searchsorted(tile_end, tidx, side="right"), MOE_EXPERTS - 1).astype(jnp.int32)
    te = jnp.where(tidx < n_valid, te, te[jnp.maximum(n_valid - 1, 0)])
    src = jnp.zeros((n_tiles * MOE_TM,), jnp.int32).at[dest].set(jnp.arange(n_pair, dtype=jnp.int32) // 2)
    xs = jnp.take(x, src, axis=0)
    out = pl.pallas_call(
        _moe_kernel,
        grid_spec=pltpu.PrefetchScalarGridSpec(
            num_scalar_prefetch=2,
            grid=(n_tiles,),
            in_specs=[pl.BlockSpec((MOE_TM, D), lambda i, te, nt: (i, 0)),
                      pl.BlockSpec((None, D, MOE_FF), lambda i, te, nt: (te[i], 0, 0)),
                      pl.BlockSpec((None, D, MOE_FF), lambda i, te, nt: (te[i], 0, 0)),
                      pl.BlockSpec((None, MOE_FF, D), lambda i, te, nt: (te[i], 0, 0))],
            out_specs=pl.BlockSpec((MOE_TM, D), lambda i, te, nt: (i, 0)),
        ),
        out_shape=jax.ShapeDtypeStruct((n_tiles * MOE_TM, D), F32),
        compiler_params=_cparams(("arbitrary",)),
        name="moe_experts",
    )(te, n_valid.reshape(1).astype(jnp.int32), xs, w1, w3, w2)
    o = jnp.take(out, dest, axis=0).reshape(n_tok, 2, D)
    y = o[:, 0] * w[:, 0:1] + o[:, 1] * w[:, 1:2]
    return y.reshape(batch, t_all, D)


def kernel(x, c, ctx, c_ctx, ada_w, ada_b, norm1_g, norm2_g, da_w_in, da_w_out, da_q_g, da_k_g, da_lam_q1, da_lam_k1, da_lam_q2, da_lam_k2, da_sub_g, gm_w_in, gm_v_g, gm_w_s, gm_b_s, gm_w_out, ssm_w_in, ssm_conv_w, ssm_conv_b, ssm_dt_bias, ssm_a_log, ssm_d, ssm_out_g, ssm_w_out, mla_w_in, mla_q_norm_g, mla_kv_norm_g, mla_w_uq, mla_w_ukv, mla_q_g, mla_k_g, mla_w_out, moe_w_group, moe_b_group, moe_w_router, moe_b_router, moe_w1, moe_w3, moe_w2):
    batch, seq, _ = x.shape
    assert ctx.shape[1] == CTX and seq % TM == 0 and seq % GRID_W == 0
    h = jnp.concatenate([ctx, x], axis=1)

    rows = ((batch + 1 + 7) // 8) * 8
    cc = jnp.concatenate([c, c_ctx[None, :], jnp.zeros((rows - batch - 1, D), F32)], axis=0)
    ada = _ada_all(cc, ada_w, ada_b)
    mod_l = ada[:, :batch].reshape(DEPTH, batch, 6, D)
    mod_c = jnp.broadcast_to(ada[:, batch].reshape(DEPTH, 1, 6, D), (DEPTH, batch, 6, D))
    mod_all = jnp.stack([mod_c, mod_l], axis=2)
    mod_all = jnp.concatenate([mod_all, jnp.zeros((DEPTH, batch, 2, 2, D), F32)], axis=3)

    pad = LANES - MOE_EXPERTS - MOE_GROUPS
    for i in range(DEPTH):
        kind = i % 4
        mod = mod_all[i]
        n1g = norm1_g[i].reshape(1, D)
        n2g = norm2_g[i].reshape(1, D)
        wr = jnp.concatenate([moe_w_router[i], moe_w_group[i], jnp.zeros((D, pad), F32)], axis=1)
        br = jnp.concatenate([moe_b_router[i], moe_b_group[i], jnp.zeros((pad,), F32)]).reshape(1, LANES)
        if kind == 0:
            a = _diff_attention(h, mod, n1g, da_w_in[0], da_q_g[0], da_k_g[0], da_lam_q1[0], da_lam_k1[0],
                                da_lam_q2[0], da_lam_k2[0], da_sub_g[0], i)
            h, xn, ids, wts = _outproj(a, da_w_out[0].astype(BF), h, mod, n2g, wr, br)
        elif kind == 1:
            h, xn, ids, wts = _gmlp_layer(h, mod, n1g, gm_w_in[0], gm_v_g[0], gm_w_s[0], gm_b_s[0], gm_w_out[0],
                                          n2g, wr, br)
        elif kind == 2:
            h, xn, ids, wts = _mamba_layer(h, mod, n1g, ssm_w_in[0], ssm_conv_w[0], ssm_conv_b[0], ssm_dt_bias[0],
                                           ssm_a_log[0], ssm_d[0], ssm_out_g[0], ssm_w_out[0], n2g, wr, br)
        else:
            a = _mla_attention(h, mod, n1g, mla_w_in[0], mla_q_norm_g[0], mla_kv_norm_g[0], mla_w_uq[0],
                               mla_w_ukv[0], mla_q_g[0], mla_k_g[0])
            h, xn, ids, wts = _outproj(a, mla_w_out[0].astype(BF), h, mod, n2g, wr, br)
        y = _moe(xn, ids, wts, moe_w1[i].astype(BF), moe_w3[i].astype(BF), moe_w2[i].astype(BF))
        gate2 = jnp.stack([mod[:, 0, 5], mod[:, 1, 5]], axis=1)
        g2 = jnp.concatenate([jnp.broadcast_to(gate2[:, 0:1], (batch, CTX, D)),
                              jnp.broadcast_to(gate2[:, 1:2], (batch, seq, D))], axis=1)
        h = h + g2 * y
    return h[:, CTX:, :]
```

```python
import functools
import math

import numpy as np
import jax
import jax.numpy as jnp
from jax import lax
from jax.experimental import pallas as pl
from jax.experimental.pallas import tpu as pltpu

F32 = jnp.float32
BF = jnp.bfloat16
HIGHEST = lax.Precision.HIGHEST

D = 1024
CTX = 256
GRID_W = 64
EPS = 1e-6
ROPE_BASE = 10000.0
DEPTH = 4
LANES = 128
TM = 256

DA_HEADS = 8
DA_HEAD_DIM = 64

GM_CHUNK = 128
GM_GROUPS = 8

SSM_INNER = 2048
SSM_HEADS = 32
SSM_GROUPS = 4
SSM_STATE = 128
SSM_CONV = 5
SSM_CHUNK = 128
SSM_CONV_DIM = SSM_INNER + 2 * SSM_GROUPS * SSM_STATE

MLA_HEADS = 16
MLA_Q_RANK = 384
MLA_KV_RANK = 256
MLA_NOPE = 64
MLA_ROPE = 32
MLA_V = 64
MLA_QK = MLA_NOPE + MLA_ROPE

MOE_GROUPS = 4
MOE_PER_GROUP = 8
MOE_EXPERTS = 32
MOE_FF = 512
MOE_TM = 256

VMEM_LIMIT = 56 * 1024 * 1024


def _cparams(sem):
    return pltpu.CompilerParams(dimension_semantics=sem, vmem_limit_bytes=VMEM_LIMIT)


def _full_spec(arr):
    nd = arr.ndim
    return pl.BlockSpec(arr.shape, lambda *_: (0,) * nd)


def _tok_spec(width, col=0):
    return pl.BlockSpec((None, TM, width), lambda b, t: (b, t, col))


def _mod_spec():
    return pl.BlockSpec((None, None, 8, D), lambda b, t: (b, jnp.minimum(t, 1), 0, 0))


def _silu(x):
    return x * (1.0 / (1.0 + jnp.exp(-x)))


def _norm_mod(h, g, scale, shift):
    ms = jnp.mean(h * h, axis=-1, keepdims=True)
    y = h * lax.rsqrt(ms + EPS) * g
    return y * (1.0 + scale) + shift


def _ada_kernel(c_ref, w_ref, b_ref, o_ref):
    a = _silu(c_ref[...]).astype(BF)
    o_ref[...] = jnp.dot(a, w_ref[...].astype(BF), preferred_element_type=F32) + b_ref[...]


def _ada_all(cc, ada_w, ada_b):
    rows = cc.shape[0]
    tn = 1536
    return pl.pallas_call(
        _ada_kernel,
        grid=(DEPTH, 6 * D // tn),
        in_specs=[pl.BlockSpec((rows, D), lambda l, j: (0, 0)),
                  pl.BlockSpec((None, D, tn), lambda l, j: (l, 0, j)),
                  pl.BlockSpec((None, 1, tn), lambda l, j: (l, 0, j))],
        out_specs=pl.BlockSpec((None, rows, tn), lambda l, j: (l, 0, j)),
        out_shape=jax.ShapeDtypeStruct((DEPTH, rows, 6 * D), F32),
        compiler_params=_cparams(("arbitrary", "arbitrary")),
        name="ada_mod",
    )(cc, ada_w, ada_b.reshape(DEPTH, 1, 6 * D))


def _residual_and_route(delta, h_ref, mod, n2g_ref, wr_ref, br_ref, outs):
    ho_ref, xo_ref, id_ref, wt_ref, cnt_ref = outs
    hn = h_ref[...] + mod[2:3] * delta
    ho_ref[...] = hn
    x2 = _norm_mod(hn, n2g_ref[...], mod[4:5], mod[3:4])
    xo_ref[...] = x2
    lg = jnp.dot(x2, wr_ref[...], preferred_element_type=F32, precision=HIGHEST) + br_ref[...]
    lane = lax.broadcasted_iota(jnp.int32, lg.shape, 1)
    neg = jnp.float32(-jnp.inf)
    big = jnp.int32(1 << 20)
    is_g = jnp.logical_and(lane >= MOE_EXPERTS, lane < MOE_EXPERTS + MOE_GROUPS)
    gl = jnp.where(is_g, lg, neg)
    gmax = jnp.max(gl, axis=-1, keepdims=True)
    gidx = jnp.min(jnp.where(gl == gmax, lane, big), axis=-1, keepdims=True) - MOE_EXPERTS
    pg = 1.0 / jnp.sum(jnp.where(is_g, jnp.exp(gl - gmax), 0.0), axis=-1, keepdims=True)
    lo = gidx * MOE_PER_GROUP
    in_grp = jnp.logical_and(lane >= lo, lane < lo + MOE_PER_GROUP)
    el = jnp.where(in_grp, lg, neg)
    e1 = jnp.max(el, axis=-1, keepdims=True)
    i1 = jnp.min(jnp.where(el == e1, lane, big), axis=-1, keepdims=True)
    el2 = jnp.where(lane == i1, neg, el)
    e2 = jnp.max(el2, axis=-1, keepdims=True)
    i2 = jnp.min(jnp.where(el2 == e2, lane, big), axis=-1, keepdims=True)
    t = jnp.exp(e2 - e1)
    w1 = pg / (1.0 + t)
    w2 = pg * t / (1.0 + t)
    id_ref[...] = jnp.where(lane == 0, i1, jnp.where(lane == 1, i2, 0))
    wt_ref[...] = jnp.where(lane == 0, w1, jnp.where(lane == 1, w2, 0.0))
    chosen = jnp.where(jnp.logical_or(lane == i1, lane == i2), 1.0, 0.0)
    cnt_ref[...] = jnp.sum(chosen, axis=0, keepdims=True)


def _tail_in_specs(n2g, wr, br):
    return [_full_spec(n2g), _full_spec(wr), _full_spec(br)]


def _tail_out(batch, t_all):
    specs = [_tok_spec(D), _tok_spec(D), _tok_spec(LANES), _tok_spec(LANES),
             pl.BlockSpec((None, None, 1, LANES), lambda b, t: (b, t, 0, 0))]
    shapes = [jax.ShapeDtypeStruct((batch, t_all, D), F32),
              jax.ShapeDtypeStruct((batch, t_all, D), F32),
              jax.ShapeDtypeStruct((batch, t_all, LANES), jnp.int32),
              jax.ShapeDtypeStruct((batch, t_all, LANES), F32),
              jax.ShapeDtypeStruct((batch, t_all // TM, 1, LANES), F32)]
    return specs, shapes


def _outproj_kernel(a_ref, w_ref, h_ref, mod_ref, n2g_ref, wr_ref, br_ref, *outs):
    delta = jnp.dot(a_ref[...], w_ref[...], preferred_element_type=F32)
    _residual_and_route(delta, h_ref, mod_ref[...], n2g_ref, wr_ref, br_ref, outs)


def _outproj(a, w, h, mod, n2g, wr, br):
    batch, t_all, k = a.shape
    out_specs, out_shapes = _tail_out(batch, t_all)
    return pl.pallas_call(
        _outproj_kernel,
        grid=(batch, t_all // TM),
        in_specs=[_tok_spec(k), _full_spec(w), _tok_spec(D), _mod_spec()] + _tail_in_specs(n2g, wr, br),
        out_specs=out_specs,
        out_shape=out_shapes,
        compiler_params=_cparams(("parallel", "parallel")),
        name="outproj_route",
    )(a, w, h, mod, n2g, wr, br)


def _rope_tables(t_lat, rot_dim):
    rows = t_lat // GRID_W
    row = np.repeat(np.arange(rows, dtype=np.float32), GRID_W)
    col = np.tile(np.arange(GRID_W, dtype=np.float32), rows)
    n_freq = rot_dim // 4
    inv_freq = jnp.asarray(ROPE_BASE, F32) ** (-jnp.arange(n_freq, dtype=F32) / n_freq)
    ang = jnp.concatenate([jnp.asarray(row)[:, None] * inv_freq, jnp.asarray(col)[:, None] * inv_freq], axis=-1)
    return jnp.cos(ang), jnp.sin(ang)


def _with_ctx_identity(cos_l, sin_l):
    cos = jnp.concatenate([jnp.ones((CTX, LANES), F32), cos_l], axis=0)
    sin = jnp.concatenate([jnp.zeros((CTX, LANES), F32), sin_l], axis=0)
    return cos, sin


def _da_perm():
    perm = np.zeros(2 * DA_HEADS * DA_HEAD_DIM, np.int32)
    for h in range(DA_HEADS):
        for m in range(2):
            for i in range(DA_HEAD_DIM):
                new = h * LANES + (i // 32) * 64 + m * 32 + (i % 32)
                perm[new] = (2 * h + m) * DA_HEAD_DIM + i
    return perm


def _da_lane_of_dim():
    lane = np.arange(LANES)
    return (lane // 64) * 32 + lane % 32


def _da_proj_kernel(h_ref, mod_ref, n1g_ref, w_ref, gq_ref, gk_ref, cos_ref, sin_ref, gavg_ref,
                    q_ref, k_ref, v_ref):
    mod = mod_ref[...]
    xn = _norm_mod(h_ref[...], n1g_ref[...], mod[1:2], mod[0:1]).astype(BF)
    cos = cos_ref[...]
    sin = sin_ref[...]
    gavg = gavg_ref[...]
    nq = DA_HEADS * LANES
    scale = DA_HEAD_DIM ** -0.5 * LOG2E
    for dst, off, g_ref, mult in ((q_ref, 0, gq_ref, scale), (k_ref, nq, gk_ref, 1.0)):
        y_all = jnp.dot(xn, w_ref[:, off:off + nq], preferred_element_type=F32)
        for j in range(DA_HEADS):
            y = y_all[:, j * LANES:(j + 1) * LANES]
            ms = jnp.dot(y * y, gavg, preferred_element_type=F32, precision=HIGHEST)
            y = y * lax.rsqrt(ms + EPS) * g_ref[...]
            y = y * cos + pltpu.roll(y, 64, 1) * sin
            dst[:, j * LANES:(j + 1) * LANES] = (y * mult).astype(BF)
    v_ref[...] = jnp.dot(xn, w_ref[:, 2 * nq:3 * nq], preferred_element_type=F32).astype(BF)


LOG2E = math.log2(math.e)


def _softmax_pv(q, k, v):
    s = lax.dot_general(q, k, (((1,), (1,)), ((), ())), preferred_element_type=F32)
    e = jnp.exp2(s - jnp.max(s, axis=-1, keepdims=True))
    l = jnp.sum(e, axis=-1, keepdims=True)
    return jnp.dot(e.astype(BF), v, preferred_element_type=F32) / l


def _da_attn_kernel(lam_ref, q_ref, k_ref, v_ref, subg_ref, o_ref, *, lam_init):
    q = q_ref[...]
    lane = lax.broadcasted_iota(jnp.int32, q.shape, 1)
    comp0 = ((lane // 32) % 2) == 0
    zero = jnp.zeros_like(q)
    q0 = jnp.where(comp0, q, zero)
    q1 = jnp.where(comp0, zero, q)

    def run(k, v):
        o = _softmax_pv(q0, k, v) - lam_ref[0] * _softmax_pv(q1, k, v)
        ms = jnp.mean(o * o, axis=-1, keepdims=True)
        o = o * lax.rsqrt(ms + EPS) * subg_ref[...] * (1.0 - lam_init)
        o_ref[...] = o.astype(BF)

    is_ctx = pl.program_id(2) == 0

    @pl.when(is_ctx)
    def _():
        run(k_ref[0:CTX, :], v_ref[0:CTX, :])

    @pl.when(jnp.logical_not(is_ctx))
    def _():
        run(k_ref[...], v_ref[...])


def _diff_attention(h, mod, n1g, w_in, q_g, k_g, lam_q1, lam_k1, lam_q2, lam_k2, sub_g, layer_idx):
    batch, t_all, _ = h.shape
    nq = DA_HEADS * LANES
    perm = _da_perm()
    w = jnp.concatenate([w_in[:, :nq][:, perm], w_in[:, nq:2 * nq][:, perm], w_in[:, 2 * nq:]], axis=1).astype(BF)
    ldim = _da_lane_of_dim()
    gq = q_g[ldim].reshape(1, LANES)
    gk = k_g[ldim].reshape(1, LANES)
    cos, sin = _rope_tables(t_all - CTX, DA_HEAD_DIM)
    cos, sin = _with_ctx_identity(jnp.tile(cos, (1, 4)), jnp.concatenate([-sin, -sin, sin, sin], axis=1))
    comp = (np.arange(LANES) // 32) % 2
    gavg = jnp.asarray((comp[:, None] == comp[None, :]).astype(np.float32) / DA_HEAD_DIM)
    tab_spec = pl.BlockSpec((TM, LANES), lambda b, t: (t, 0))
    qkv_shape = jax.ShapeDtypeStruct((batch, t_all, nq), BF)
    q, k, v = pl.pallas_call(
        _da_proj_kernel,
        grid=(batch, t_all // TM),
        in_specs=[_tok_spec(D), _mod_spec(), _full_spec(n1g), _full_spec(w), _full_spec(gq), _full_spec(gk),
                  tab_spec, tab_spec, _full_spec(gavg)],
        out_specs=[_tok_spec(nq)] * 3,
        out_shape=[qkv_shape] * 3,
        compiler_params=_cparams(("parallel", "parallel")),
        name="da_proj",
    )(h, mod, n1g, w, gq, gk, cos, sin, gavg)

    lam_init = 0.8 - 0.6 * math.exp(-0.3 * layer_idx)
    lam = (jnp.exp(jnp.sum(lam_q1 * lam_k1)) - jnp.exp(jnp.sum(lam_q2 * lam_k2)) + lam_init).reshape(1)
    kv_spec = pl.BlockSpec((None, t_all, LANES), lambda b, hh, t: (b, 0, hh))
    qo_spec = pl.BlockSpec((None, TM, LANES), lambda b, hh, t: (b, t, hh))
    subg = sub_g.reshape(1, LANES)
    return pl.pallas_call(
        functools.partial(_da_attn_kernel, lam_init=lam_init),
        grid=(batch, DA_HEADS, t_all // TM),
        in_specs=[pl.BlockSpec(memory_space=pltpu.SMEM), qo_spec, kv_spec, kv_spec,
                  pl.BlockSpec((1, LANES), lambda b, hh, t: (0, 0))],
        out_specs=qo_spec,
        out_shape=jax.ShapeDtypeStruct((batch, t_all, nq), BF),
        compiler_params=_cparams(("parallel", "parallel", "arbitrary")),
        name="da_attn",
    )(lam, q, k, v, subg)


def _gmlp_kernel(h_ref, mod_ref, n1g_ref, win_ref, vg_ref, ws_ref, bs_ref, wout_ref,
                 n2g_ref, wr_ref, br_ref, *outs):
    mod = mod_ref[...]
    xn = _norm_mod(h_ref[...], n1g_ref[...], mod[1:2], mod[0:1]).astype(BF)
    uv =jnp.dot(xn, win_ref[...], preferred_element_type=F32)
    uv = 0.5 * uv * (1.0 + lax.erf(uv * (2.0 ** -0.5)))
    u = uv[:, :D]
    v = uv[:, D:]
    v = (v * lax.rsqrt(jnp.mean(v * v, axis=-1, keepdims=True) + EPS) * vg_ref[...]).astype(BF)
    rows = []
    for c in range(TM // GM_CHUNK):
        cols = []
        for g in range(GM_GROUPS):
            vb = v[c * GM_CHUNK:(c + 1) * GM_CHUNK, g * LANES:(g + 1) * LANES]
            s = jnp.dot(ws_ref[g], vb, preferred_element_type=F32) + bs_ref[g]
            cols.append(s)
        rows.append(jnp.concatenate(cols, axis=1))
    s_all = jnp.concatenate(rows, axis=0)
    delta = jnp.dot((u * s_all).astype(BF), wout_ref[...], preferred_element_type=F32)
    _residual_and_route(delta, h_ref, mod, n2g_ref, wr_ref, br_ref, outs)


def _gmlp_layer(h, mod, n1g, w_in, v_g, w_s, b_s, w_out, n2g, wr, br):
    batch, t_all, _ = h.shape
    win = w_in.astype(BF)
    wout = w_out.astype(BF)
    ws = w_s.astype(BF)
    bs = jnp.broadcast_to(b_s[:, :, None], (GM_GROUPS, GM_CHUNK, LANES)).astype(F32)
    vg = v_g.reshape(1, D)
    out_specs, out_shapes = _tail_out(batch, t_all)
    return pl.pallas_call(
        _gmlp_kernel,
        grid=(batch, t_all // TM),
        in_specs=[_tok_spec(D), _mod_spec(), _full_spec(n1g), _full_spec(win), _full_spec(vg), _full_spec(ws),
                  _full_spec(bs), _full_spec(wout)] + _tail_in_specs(n2g, wr, br),
        out_specs=out_specs,
        out_shape=out_shapes,
        compiler_params=_cparams(("parallel", "parallel")),
        name="gmlp_layer",
    )(h, mod, n1g, win, vg, ws, bs, wout, n2g, wr, br)


SSM_DT_PAD = LANES


def _ssm_proj_kernel(h_ref, mod_ref, n1g_ref, w_ref, dtb_ref, z_ref, xbc_ref, dt_ref):
    mod = mod_ref[...]
    xn = _norm_mod(h_ref[...], n1g_ref[...], mod[1:2], mod[0:1]).astype(BF)
    z_ref[...] = jnp.dot(xn, w_ref[:, :SSM_INNER], preferred_element_type=F32).astype(BF)
    c0 = SSM_INNER
    for j in range(SSM_CONV_DIM // 1024):
        xbc_ref[:, j * 1024:(j + 1) * 1024] = jnp.dot(
            xn, w_ref[:, c0 + j * 1024:c0 + (j + 1) * 1024], preferred_element_type=F32).astype(BF)
    c1 = SSM_INNER + SSM_CONV_DIM
    raw = jnp.dot(xn, w_ref[:, c1:c1 + SSM_DT_PAD], preferred_element_type=F32) + dtb_ref[...]
    dt_ref[...] = jnp.maximum(raw, 0.0) + jnp.log1p(jnp.exp(-jnp.abs(raw)))


CONV_PAD = 8
CONV_W = 512


def _conv_kernel(x_ref, w_ref, b_ref, o_ref, pad_ref):
    t_all = x_ref.shape[0]
    t_lat = t_all - CTX
    lat0 = CTX + 2 * CONV_PAD
    zeros = jnp.zeros((CONV_PAD, CONV_W), F32)
    pad_ref[0:CONV_PAD, :] = zeros
    pad_ref[CONV_PAD + CTX:lat0, :] = jnp.zeros((CONV_PAD, CONV_W), F32)
    pad_ref[lat0 + t_lat:lat0 + t_lat + CONV_PAD, :] = zeros
    pad_ref[CONV_PAD:CONV_PAD + CTX, :] = x_ref[0:CTX, :].astype(F32)
    pad_ref[lat0:lat0 + t_lat, :] = x_ref[CTX:t_all, :].astype(F32)
    half = (SSM_CONV - 1) // 2
    for start, n, dst in ((CONV_PAD, CTX, 0), (lat0, t_lat, CTX)):
        acc = jnp.zeros((n, CONV_W), F32) + b_ref[...]
        for kk in range(SSM_CONV):
            acc = acc + pad_ref[start + kk - half:start + kk - half + n, :] * w_ref[kk:kk + 1, :]
        o_ref[dst:dst + n, :] = _silu(acc).astype(BF)


def _ssd_kernel(dt_ref, a_ref, x_ref, b_ref, c_ref, y_ref, s_ref):
    direction = pl.program_id(1)
    step = pl.program_id(3)

    @pl.when(step == 0)
    def _():
        s_ref[...] = jnp.zeros_like(s_ref)

    q = SSM_CHUNK
    row = lax.broadcasted_iota(jnp.int32, (q, q), 0)
    col = lax.broadcasted_iota(jnp.int32, (q, q), 1)
    sign = 1 - 2 * direction
    tri_b = (col - row) * sign <= 0
    tri = tri_b.astype(F32)
    tri_t = ((row - col) * sign <= 0).astype(F32)
    eye = (row == col).astype(F32)
    lane_lo = col < 64

    dt_t = dt_ref[...]
    a_t = dt_t * a_ref[...]
    cs_rows = jnp.dot(a_t, tri_t, preferred_element_type=F32, precision=HIGHEST)
    tot = jnp.sum(a_t, axis=1, keepdims=True)
    bm = b_ref[...]
    cm = c_ref[...]
    cb = lax.dot_general(cm, bm, (((1,), (1,)), ((), ())), preferred_element_type=F32)
    bm_t = bm.astype(F32).T.astype(BF)

    for p in range(4):
        xp = x_ref[:, p * LANES:(p + 1) * LANES].astype(F32)
        cs_col, dt_col, mats = [], [], []
        for hh in range(2):
            e = 2 * p + hh
            cc = jnp.sum(tri * a_t[e:e + 1, :], axis=1, keepdims=True)
            cs_col.append(cc)
            dt_col.append(jnp.sum(eye * dt_t[e:e + 1, :], axis=1, keepdims=True))
            decay = jnp.exp(jnp.where(tri_b, cc - cs_rows[e:e + 1, :], -jnp.inf))
            mats.append((cb * decay).astype(BF))
        xdt = xp * jnp.where(lane_lo, dt_col[0], dt_col[1])
        xb = xdt.astype(BF)
        y_diag = jnp.where(lane_lo,
                           jnp.dot(mats[0], xb, preferred_element_type=F32),
                           jnp.dot(mats[1], xb, preferred_element_type=F32))
        s_prev = s_ref[p]
        y_off = jnp.dot(cm, s_prev.astype(BF), preferred_element_type=F32)
        y_off = y_off * jnp.where(lane_lo, jnp.exp(cs_col[0]), jnp.exp(cs_col[1]))
        y_ref[:, p * LANES:(p + 1) * LANES] = y_diag + y_off
        t0 = tot[2 * p:2 * p + 1, :]
        t1 = tot[2 * p + 1:2 * p + 2, :]
        to_end = jnp.where(lane_lo, jnp.exp(t0 - cs_col[0]), jnp.exp(t1 - cs_col[1]))
        upd = jnp.dot(bm_t, (xdt * to_end).astype(BF), preferred_element_type=F32)
        s_ref[p] = s_prev * jnp.where(lane_lo[0:1, :], jnp.exp(t0), jnp.exp(t1)) + upd


def _ssm_finish_kernel(yf_ref, yb_ref, xs_ref, z_ref, d_ref, og_ref, w_ref, h_ref, mod_ref,
                       n2g_ref, wr_ref, br_ref, *outs):
    y = yf_ref[...] + yb_ref[...] + d_ref[...] * xs_ref[...].astype(F32)
    y = y * _silu(z_ref[...].astype(F32))
    y = y * lax.rsqrt(jnp.mean(y * y, axis=-1, keepdims=True) + EPS) * og_ref[...]
    delta = jnp.dot(y.astype(BF), w_ref[...], preferred_element_type=F32)
    _residual_and_route(delta, h_ref, mod_ref[...], n2g_ref, wr_ref, br_ref, outs)


def _mamba_layer(h, mod, n1g, w_in, conv_w, conv_b, dt_bias, a_log, d_skip, out_g, w_out, n2g, wr, br):
    batch, t_all, _ = h.shape
    n_dt = 2 * SSM_HEADS
    c1 = SSM_INNER + SSM_CONV_DIM
    w = jnp.concatenate([w_in, jnp.zeros((D, SSM_DT_PAD - n_dt), F32)], axis=1).astype(BF)
    dtb = jnp.concatenate([dt_bias.reshape(n_dt), jnp.zeros((SSM_DT_PAD - n_dt,), F32)]).reshape(1, SSM_DT_PAD)
    z, xbc, dt = pl.pallas_call(
        _ssm_proj_kernel,
        grid=(batch, t_all // TM),
        in_specs=[_tok_spec(D), _mod_spec(), _full_spec(n1g), _full_spec(w), _full_spec(dtb)],
        out_specs=[_tok_spec(SSM_INNER), _tok_spec(SSM_CONV_DIM), _tok_spec(SSM_DT_PAD)],
        out_shape=[jax.ShapeDtypeStruct((batch, t_all, SSM_INNER), BF),
                   jax.ShapeDtypeStruct((batch, t_all, SSM_CONV_DIM), BF),
                   jax.ShapeDtypeStruct((batch, t_all, SSM_DT_PAD), F32)],
        compiler_params=_cparams(("parallel", "parallel")),
        name="ssm_proj",
    )(h, mod, n1g, w, dtb)
    assert c1 + n_dt == w_in.shape[1]

    cw = jnp.concatenate([conv_w, jnp.zeros((8 - SSM_CONV, SSM_CONV_DIM), F32)], axis=0)
    cbias = conv_b.reshape(1, SSM_CONV_DIM)
    xbc = pl.pallas_call(
        _conv_kernel,
        grid=(batch, SSM_CONV_DIM // CONV_W),
        in_specs=[pl.BlockSpec((None, t_all, CONV_W), lambda b, j: (b, 0, j)),
                  pl.BlockSpec((8, CONV_W), lambda b, j: (0, j)),
                  pl.BlockSpec((1, CONV_W), lambda b, j: (0, j))],
        out_specs=pl.BlockSpec((None, t_all, CONV_W), lambda b, j: (b, 0, j)),
        out_shape=jax.ShapeDtypeStruct((batch, t_all, SSM_CONV_DIM), BF),
        scratch_shapes=[pltpu.VMEM((t_all + 3 * CONV_PAD, CONV_W), F32)],
        compiler_params=_cparams(("parallel", "parallel")),
        name="ssm_conv",
    )(xbc, cw, cbias)

    n_chunks = t_all // SSM_CHUNK
    ctx_chunks = CTX // SSM_CHUNK
    e_per_g = SSM_HEADS // SSM_GROUPS
    dt_t = jnp.swapaxes(dt[:, :, :n_dt], 1, 2)
    a_neg = -jnp.exp(a_log.astype(F32)).reshape(2, SSM_GROUPS, e_per_g, 1)
    a_neg = jnp.broadcast_to(a_neg, (2, SSM_GROUPS, e_per_g, SSM_CHUNK))

    def chunk_of(dr, s):
        rev = jnp.where(s < ctx_chunks, ctx_chunks - 1 - s, n_chunks - 1 + ctx_chunks - s)
        return jnp.where(dr == 0, s, rev)

    gw = e_per_g * 64
    y = pl.pallas_call(
        _ssd_kernel,
        grid=(batch, 2, SSM_GROUPS, n_chunks),
        in_specs=[pl.BlockSpec((None, e_per_g, SSM_CHUNK), lambda b, dr, g, s: (b, dr * SSM_GROUPS + g, chunk_of(dr, s))),
                  pl.BlockSpec((None, None, e_per_g, SSM_CHUNK), lambda b, dr, g, s: (dr, g, 0, 0)),
                  pl.BlockSpec((None, SSM_CHUNK, gw), lambda b, dr, g, s: (b, chunk_of(dr, s), g)),
                  pl.BlockSpec((None, SSM_CHUNK, SSM_STATE),
                               lambda b, dr, g, s: (b, chunk_of(dr, s), SSM_INNER // SSM_STATE + g)),
                  pl.BlockSpec((None, SSM_CHUNK, SSM_STATE),
                               lambda b, dr, g, s: (b, chunk_of(dr, s), SSM_INNER // SSM_STATE + SSM_GROUPS + g))],
        out_specs=pl.BlockSpec((None, None, SSM_CHUNK, gw), lambda b, dr, g, s: (dr, b, chunk_of(dr, s), g)),
        out_shape=jax.ShapeDtypeStruct((2, batch, t_all, SSM_INNER), F32),
        scratch_shapes=[pltpu.VMEM((4, SSM_STATE, LANES), F32)],
        compiler_params=_cparams(("parallel", "parallel", "parallel", "arbitrary")),
        name="ssd_scan",
    )(dt_t, a_neg, xbc, xbc, xbc)

    dvec = jnp.repeat(d_skip, SSM_INNER // SSM_HEADS).reshape(1, SSM_INNER)
    og = out_g.reshape(1, SSM_INNER)
    wo = w_out.astype(BF)
    out_specs, out_shapes = _tail_out(batch, t_all)
    y_spec = lambda dr: pl.BlockSpec((None, None, TM, SSM_INNER), lambda b, t: (dr, b, t, 0))
    return pl.pallas_call(
        _ssm_finish_kernel,
        grid=(batch, t_all // TM),
        in_specs=[y_spec(0), y_spec(1), _tok_spec(SSM_INNER), _tok_spec(SSM_INNER), _full_spec(dvec),
                  _full_spec(og), _full_spec(wo), _tok_spec(D), _mod_spec()] + _tail_in_specs(n2g, wr, br),
        out_specs=out_specs,
        out_shape=out_shapes,
        compiler_params=_cparams(("parallel", "parallel")),
        name="ssm_finish",
    )(y, y, xbc, z, dvec, og, wo, h, mod, n2g, wr, br)


def _mla_lane_src():
    src = np.full(LANES, -1, np.int32)
    src[0:16] = MLA_NOPE + np.arange(16)
    src[16:64] = np.arange(48)
    src[64:80] = MLA_NOPE + 16 + np.arange(16)
    src[80:96] = 48 + np.arange(16)
    return src


def _mla_proj_kernel(h_ref, mod_ref, n1g_ref, win_ref, qng_ref, kvng_ref, wuq_ref, wuk_ref, wuv_ref,
                     gq_ref, gk_ref, cos_ref, sin_ref, avg_ref, q_ref, k_ref, v_ref):
    mod = mod_ref[...]
    xn = _norm_mod(h_ref[...], n1g_ref[...], mod[1:2], mod[0:1]).astype(BF)
    lat = jnp.dot(xn, win_ref[...], preferred_element_type=F32)
    cq = lat[:, :MLA_Q_RANK]
    ckv = lat[:, MLA_Q_RANK:MLA_Q_RANK + MLA_KV_RANK]
    kpe = lat[:, MLA_Q_RANK + MLA_KV_RANK:]
    cq = (cq * lax.rsqrt(jnp.mean(cq * cq, axis=-1, keepdims=True) + EPS) * qng_ref[...]).astype(BF)
    ckv = (ckv * lax.rsqrt(jnp.mean(ckv * ckv, axis=-1, keepdims=True) + EPS) * kvng_ref[...]).astype(BF)
    cos = cos_ref[...]
    sin = sin_ref[...]
    avg = avg_ref[...]
    scale = MLA_QK ** -0.5 * LOG2E
    qa = jnp.dot(cq, wuq_ref[...], preferred_element_type=F32)
    ka = jnp.dot(ckv, wuk_ref[...], preferred_element_type=F32)
    for j in range(MLA_HEADS):
        for dst, src, add, g_ref, mult in ((q_ref, qa, None, gq_ref, scale), (k_ref, ka, kpe, gk_ref, 1.0)):
            y = src[:, j * LANES:(j + 1) * LANES]
            if add is not None:
                y = y + add
            ms = jnp.dot(y * y, avg, preferred_element_type=F32, precision=HIGHEST)
            y = y * lax.rsqrt(ms + EPS) * g_ref[...]
            y = y * cos + pltpu.roll(y, 64, 1) * sin
            dst[:, j * LANES:(j + 1) * LANES] = (y * mult).astype(BF)
    v_ref[...] = jnp.dot(ckv, wuv_ref[...], preferred_element_type=F32).astype(BF)


def _mla_attn_kernel(q_ref, k_ref, v_ref, o_ref):
    lane = lax.broadcasted_iota(jnp.int32, (TM, LANES), 1)

    def run(n_keys):
        v = v_ref[0:n_keys, :]
        outs = [_softmax_pv(q_ref[:, hh * LANES:(hh + 1) * LANES], k_ref[0:n_keys, hh * LANES:(hh + 1) * LANES], v)
                for hh in range(2)]
        o_ref[...] = jnp.where(lane < MLA_V, outs[0], outs[1]).astype(BF)

    is_ctx = pl.program_id(2) == 0

    @pl.when(is_ctx)
    def _():
        run(CTX)

    @pl.when(jnp.logical_not(is_ctx))
    def _():
        run(v_ref.shape[0])


def _mla_attention(h, mod, n1g, w_in, q_norm_g, kv_norm_g, w_uq, w_ukv, q_g, k_g):
    batch, t_all, _ = h.shape
    src = _mla_lane_src()
    used = src >= 0
    srcc = np.where(used, src, 0)
    nh = MLA_HEADS * LANES
    kpe_cols = np.where(srcc >= MLA_NOPE, MLA_Q_RANK + MLA_KV_RANK + srcc - MLA_NOPE, 0)
    kpe_used = np.logical_and(used, src >= MLA_NOPE)
    w_kpe = jnp.where(jnp.asarray(kpe_used)[None, :], w_in[:, kpe_cols], 0.0)
    win = jnp.concatenate([w_in[:, :MLA_Q_RANK + MLA_KV_RANK], w_kpe], axis=1).astype(BF)
    q_cols = (np.arange(MLA_HEADS)[:, None] * MLA_QK + srcc[None, :]).reshape(-1)
    q_mask = jnp.asarray(np.tile(used, MLA_HEADS))[None, :]
    wuq = jnp.where(q_mask, w_uq[:, q_cols], 0.0).astype(BF)
    nope_used = np.logical_and(used, src < MLA_NOPE)
    k_cols = (np.arange(MLA_HEADS)[:, None] * (MLA_NOPE + MLA_V) + np.where(nope_used, srcc, 0)[None, :]).reshape(-1)
    k_mask = jnp.asarray(np.tile(nope_used, MLA_HEADS))[None, :]
    wuk = jnp.where(k_mask, w_ukv[:, k_cols], 0.0).astype(BF)
    v_cols = (np.arange(MLA_HEADS)[:, None] * (MLA_NOPE + MLA_V) + MLA_NOPE + np.arange(MLA_V)[None, :]).reshape(-1)
    wuv = w_ukv[:, v_cols].astype(BF)
    gq = jnp.where(jnp.asarray(used), q_g[srcc], 0.0).reshape(1, LANES)
    gk = jnp.where(jnp.asarray(used), k_g[srcc], 0.0).reshape(1, LANES)
    cos16, sin16 = _rope_tables(t_all - CTX, MLA_ROPE)
    t_lat = t_all - CTX
    cos_l = jnp.ones((t_lat, LANES), F32).at[:, 0:16].set(cos16).at[:, 64:80].set(cos16)
    sin_l = jnp.zeros((t_lat, LANES), F32).at[:, 0:16].set(-sin16).at[:, 64:80].set(sin16)
    cos, sin = _with_ctx_identity(cos_l, sin_l)
    avg = jnp.full((LANES, LANES), 1.0 / MLA_QK, F32)
    qng = q_norm_g.reshape(1, MLA_Q_RANK)
    kvng = kv_norm_g.reshape(1, MLA_KV_RANK)
    tab_spec = pl.BlockSpec((TM, LANES), lambda b, t: (t, 0))
    q, k, v = pl.pallas_call(
        _mla_proj_kernel,
        grid=(batch, t_all // TM),
        in_specs=[_tok_spec(D), _mod_spec(), _full_spec(n1g), _full_spec(win), _full_spec(qng), _full_spec(kvng),
                  _full_spec(wuq), _full_spec(wuk), _full_spec(wuv), _full_spec(gq), _full_spec(gk),
                  tab_spec, tab_spec, _full_spec(avg)],
        out_specs=[_tok_spec(nh), _tok_spec(nh), _tok_spec(MLA_HEADS * MLA_V)],
        out_shape=[jax.ShapeDtypeStruct((batch, t_all, nh), BF), jax.ShapeDtypeStruct((batch, t_all, nh), BF),
                   jax.ShapeDtypeStruct((batch, t_all, MLA_HEADS * MLA_V), BF)],
        compiler_params=_cparams(("parallel", "parallel")),
        name="mla_proj",
    )(h, mod, n1g, win, qng, kvng, wuq, wuk, wuv, gq, gk, cos, sin, avg)

    return pl.pallas_call(
        _mla_attn_kernel,
        grid=(batch, MLA_HEADS // 2, t_all // TM),
        in_specs=[pl.BlockSpec((None, TM, 2 * LANES), lambda b, p, t: (b, t, p)),
                  pl.BlockSpec((None, t_all, 2 * LANES), lambda b, p, t: (b, 0, p)),
                  pl.BlockSpec((None, t_all, LANES), lambda b, p, t: (b, 0, p))],
        out_specs=pl.BlockSpec((None, TM, LANES), lambda b, p, t: (b, t, p)),
        out_shape=jax.ShapeDtypeStruct((batch, t_all, MLA_HEADS * MLA_V), BF),
        compiler_params=_cparams(("parallel", "parallel", "arbitrary")),
        name="mla_attn",
    )(q, k, v)


def _plan_kernel(ids_ref, base_ref, dest_ref):
    ids = ids_ref[...]
    lane = lax.broadcasted_iota(jnp.int32, ids.shape, 1)
    sel1 = lane == ids[:, 0:1]
    sel2 = lane == ids[:, 1:2]
    chosen = jnp.where(jnp.logical_or(sel1, sel2), 1.0, 0.0).astype(BF)
    r = lax.broadcasted_iota(jnp.int32, (TM, TM), 0)
    c = lax.broadcasted_iota(jnp.int32, (TM, TM), 1)
    earlier = jnp.where(c < r, 1.0, 0.0).astype(BF)
    row = jnp.dot(earlier, chosen, preferred_element_type=F32) + base_ref[...]
    d1 = jnp.sum(jnp.where(sel1, row, 0.0), axis=-1, keepdims=True)
    d2 = jnp.sum(jnp.where(sel2, row, 0.0), axis=-1, keepdims=True)
    dest_ref[...] = jnp.where(lane == 0, d1, jnp.where(lane == 1, d2, 0.0)).astype(jnp.int32)


def _row_copy(src, src_row, dst, dst_row, sem):
    return pltpu.make_async_copy(src.at[pl.ds(src_row, 1), :], dst.at[pl.ds(dst_row, 1), :], sem)


def _dispatch_kernel(ztile_ref, dest_ref, x_ref, xs_hbm, zbuf, zsem, sem):
    def zero_copy(e):
        start = pl.multiple_of(ztile_ref[e], MOE_TM)
        return pltpu.make_async_copy(zbuf, xs_hbm.at[pl.ds(start, MOE_TM), :], zsem)

    @pl.when(pl.program_id(0) == 0)
    def _():
        zbuf[...] = jnp.zeros_like(zbuf)
        for e in range(2 * MOE_EXPERTS):
            @pl.when(ztile_ref[e] >= 0)
            def _():
                zero_copy(e).start()
        for e in range(2 * MOE_EXPERTS):
            @pl.when(ztile_ref[e] >= 0)
            def _():
                zero_copy(e).wait()

    def issue(r, carry):
        for k in range(2):
            _row_copy(x_ref, r, xs_hbm, dest_ref[0, 2 * r + k], sem).start()
        return carry

    lax.fori_loop(0, TM, issue, 0)
    for k in range(2):
        pltpu.make_async_copy(x_ref, xs_hbm.at[pl.ds(0, TM), :], sem).wait()


def _moe_kernel(te_ref, nt_ref, x_ref, w1_ref, w3_ref, w2_ref, o_ref, w1b, w3b, w2b):
    i = pl.program_id(0)
    prev = te_ref[jnp.maximum(i - 1, 0)]

    @pl.when(jnp.logical_or(i == 0, te_ref[i] != prev))
    def _():
        w1b[...] = w1_ref[...].astype(BF)
        w3b[...] = w3_ref[...].astype(BF)
        w2b[...] = w2_ref[...].astype(BF)

    @pl.when(i < nt_ref[0])
    def _():
        x = x_ref[...].astype(BF)
        h1 = jnp.dot(x, w1b[...], preferred_element_type=F32)
        h3 = jnp.dot(x, w3b[...], preferred_element_type=F32)
        hid = (_silu(h1) * h3).astype(BF)
        o_ref[...] = jnp.dot(hid, w2b[...], preferred_element_type=F32)

    @pl.when(i >= nt_ref[0])
    def _():
        o_ref[...] = jnp.zeros_like(o_ref)


def _combine_kernel(dest_ref, out_hbm, h_ref, mod_ref, wt_ref, ho_ref, buf, sem):
    def issue(r, carry):
        for k in range(2):
            _row_copy(out_hbm, dest_ref[0, 2 * r + k], buf, k * TM + r, sem).start()
        return carry

    lax.fori_loop(0, TM, issue, 0)
    pltpu.make_async_copy(out_hbm.at[pl.ds(0, 2 * TM), :], buf, sem).wait()
    w = wt_ref[...]
    y = buf[0:TM, :] * w[:, 0:1] + buf[TM:2 * TM, :] * w[:, 1:2]
    ho_ref[...] = h_ref[...] + mod_ref[...][5:6] * y


def _moe(h, mod, xn, ids, wts, cnt, w1, w3, w2):
    batch, t_all, _ = xn.shape
    n_tok = batch * t_all
    tiles_b = t_all // TM
    n_tt = n_tok // TM
    n_tiles = (2 * n_tok + MOE_EXPERTS * (MOE_TM - 1) + MOE_TM - 1) // MOE_TM

    cnt = cnt.reshape(n_tt, LANES)[:, :MOE_EXPERTS].astype(jnp.int32)
    tiles_e = (jnp.sum(cnt, axis=0) + MOE_TM - 1) // MOE_TM
    tile_end = jnp.cumsum(tiles_e)
    tile_start = tile_end - tiles_e
    before = jnp.cumsum(cnt, axis=0) - cnt
    base = (tile_start[None, :] * MOE_TM + before).astype(F32)
    base = jnp.concatenate([base, jnp.zeros((n_tt, LANES - MOE_EXPERTS), F32)], axis=1).reshape(n_tt, 1, LANES)
    n_valid = tile_end[-1]
    tidx = jnp.arange(n_tiles, dtype=jnp.int32)
    te = jnp.minimum(jnp.sum((tile_end[None, :] <= tidx[:, None]).astype(jnp.int32), axis=1), MOE_EXPERTS - 1)
    te = jnp.where(tidx < n_valid, te, jnp.max(jnp.where(tiles_e > 0, jnp.arange(MOE_EXPERTS), 0))).astype(jnp.int32)
    assert n_tiles * MOE_TM - 2 * n_tok <= MOE_EXPERTS * MOE_TM
    tail = n_valid + jnp.arange(MOE_EXPERTS, dtype=jnp.int32)
    ztile = jnp.concatenate([jnp.where(tiles_e > 0, (tile_end - 1) * MOE_TM, -1),
                             jnp.where(tail < n_tiles, tail * MOE_TM, -1)]).astype(jnp.int32)

    dest = pl.pallas_call(
        _plan_kernel,
        grid=(n_tt,),
        in_specs=[pl.BlockSpec((TM, LANES), lambda i: (i, 0)),
                  pl.BlockSpec((None, 1, LANES), lambda i: (i, 0, 0))],
        out_specs=pl.BlockSpec((TM, LANES), lambda i: (i, 0)),
        out_shape=jax.ShapeDtypeStruct((n_tok, LANES), jnp.int32),
        compiler_params=_cparams(("parallel",)),
        name="moe_plan",
    )(ids.reshape(n_tok, LANES), base)
    dest = dest[:, :2].reshape(n_tt, 1, 2 * TM)
    dest_spec = pl.BlockSpec((None, 1, 2 * TM), lambda i, *_: (i, 0, 0), memory_space=pltpu.SMEM)

    xs = pl.pallas_call(
        _dispatch_kernel,
        grid_spec=pltpu.PrefetchScalarGridSpec(
            num_scalar_prefetch=1,
            grid=(n_tt,),
            in_specs=[dest_spec, pl.BlockSpec((TM, D), lambda i, zt: (i, 0))],
            out_specs=pl.BlockSpec(memory_space=pl.ANY),
            scratch_shapes=[pltpu.VMEM((MOE_TM, D), F32), pltpu.SemaphoreType.DMA(()), pltpu.SemaphoreType.DMA(())],
        ),
        out_shape=jax.ShapeDtypeStruct((n_tiles * MOE_TM, D), F32),
        compiler_params=_cparams(("arbitrary",)),
        name="moe_dispatch",
    )(ztile, dest, xn.reshape(n_tok, D))

    out = pl.pallas_call(
        _moe_kernel,
        grid_spec=pltpu.PrefetchScalarGridSpec(
            num_scalar_prefetch=2,
            grid=(n_tiles,),
            in_specs=[pl.BlockSpec((MOE_TM, D), lambda i, te, nt: (i, 0)),
                      pl.BlockSpec((None, D, MOE_FF), lambda i, te, nt: (te[i], 0, 0)),
                      pl.BlockSpec((None, D, MOE_FF), lambda i, te, nt: (te[i], 0, 0)),
                      pl.BlockSpec((None, MOE_FF, D), lambda i, te, nt: (te[i], 0, 0))],
            out_specs=pl.BlockSpec((MOE_TM, D), lambda i, te, nt: (i, 0)),
            scratch_shapes=[pltpu.VMEM((D, MOE_FF), BF), pltpu.VMEM((D, MOE_FF), BF), pltpu.VMEM((MOE_FF, D), BF)],
        ),
        out_shape=jax.ShapeDtypeStruct((n_tiles * MOE_TM, D), F32),
        compiler_params=_cparams(("arbitrary",)),
        name="moe_experts",
    )(te, n_valid.reshape(1).astype(jnp.int32), xs, w1, w3, w2)

    h_new = pl.pallas_call(
        _combine_kernel,
        grid=(n_tt,),
        in_specs=[dest_spec, pl.BlockSpec(memory_space=pl.ANY),
                  pl.BlockSpec((TM, D), lambda i: (i, 0)),
                  pl.BlockSpec((None, None, 8, D), lambda i: (i // tiles_b, jnp.minimum(i % tiles_b, 1), 0, 0)),
                  pl.BlockSpec((TM, LANES), lambda i: (i, 0))],
        out_specs=pl.BlockSpec((TM, D), lambda i: (i, 0)),
        out_shape=jax.ShapeDtypeStruct((n_tok, D), F32),
        scratch_shapes=[pltpu.VMEM((2 * TM, D), F32), pltpu.SemaphoreType.DMA(())],
        compiler_params=_cparams(("arbitrary",)),
        name="moe_combine",
    )(dest, out, h.reshape(n_tok, D), mod, wts.reshape(n_tok, LANES))
    return h_new.reshape(batch, t_all, D)


def kernel(x, c, ctx, c_ctx, ada_w, ada_b, norm1_g, norm2_g, da_w_in, da_w_out, da_q_g, da_k_g, da_lam_q1, da_lam_k1, da_lam_q2, da_lam_k2, da_sub_g, gm_w_in, gm_v_g, gm_w_s, gm_b_s, gm_w_out, ssm_w_in, ssm_conv_w, ssm_conv_b, ssm_dt_bias, ssm_a_log, ssm_d, ssm_out_g, ssm_w_out, mla_w_in, mla_q_norm_g, mla_kv_norm_g, mla_w_uq, mla_w_ukv, mla_q_g, mla_k_g, mla_w_out, moe_w_group, moe_b_group, moe_w_router, moe_b_router, moe_w1, moe_w3, moe_w2):
    batch, seq, _ = x.shape
    assert ctx.shape[1] == CTX and seq % TM == 0 and seq % GRID_W == 0
    h = jnp.concatenate([ctx, x], axis=1)

    rows = ((batch + 1 + 7) // 8) * 8
    cc = jnp.concatenate([c, c_ctx[None, :], jnp.zeros((rows - batch - 1, D), F32)], axis=0)
    ada = _ada_all(cc, ada_w, ada_b)
    mod_l = ada[:, :batch].reshape(DEPTH, batch, 6, D)
    mod_c = jnp.broadcast_to(ada[:, batch].reshape(DEPTH, 1, 6, D), (DEPTH, batch, 6, D))
    mod_all = jnp.stack([mod_c, mod_l], axis=2)
    mod_all = jnp.concatenate([mod_all, jnp.zeros((DEPTH, batch, 2, 2, D), F32)], axis=3)

    pad = LANES - MOE_EXPERTS - MOE_GROUPS
    for i in range(DEPTH):
        kind = i % 4
        mod = mod_all[i]
        n1g = norm1_g[i].reshape(1, D)
        n2g = norm2_g[i].reshape(1, D)
        wr = jnp.concatenate([moe_w_router[i], moe_w_group[i], jnp.zeros((D, pad), F32)], axis=1)
        br = jnp.concatenate([moe_b_router[i], moe_b_group[i], jnp.zeros((pad,), F32)]).reshape(1, LANES)
        if kind == 0:
            a = _diff_attention(h, mod, n1g, da_w_in[0], da_q_g[0], da_k_g[0], da_lam_q1[0], da_lam_k1[0],
                                da_lam_q2[0], da_lam_k2[0], da_sub_g[0], i)
            h, xn, ids, wts, cnt =_outproj(a, da_w_out[0].astype(BF), h, mod, n2g, wr, br)
        elif kind == 1:
            h, xn, ids, wts, cnt =_gmlp_layer(h, mod, n1g, gm_w_in[0], gm_v_g[0], gm_w_s[0], gm_b_s[0], gm_w_out[0],
                                          n2g, wr, br)
        elif kind == 2:
            h, xn, ids, wts, cnt =_mamba_layer(h, mod, n1g, ssm_w_in[0], ssm_conv_w[0], ssm_conv_b[0], ssm_dt_bias[0],
                                           ssm_a_log[0], ssm_d[0], ssm_out_g[0], ssm_w_out[0], n2g, wr, br)
        else:
            a = _mla_attention(h, mod, n1g, mla_w_in[0], mla_q_norm_g[0], mla_kv_norm_g[0], mla_w_uq[0],
                               mla_w_ukv[0], mla_q_g[0], mla_k_g[0])
            h, xn, ids, wts, cnt =_outproj(a, mla_w_out[0].astype(BF), h, mod, n2g, wr, br)
        h = _moe(h, mod, xn, ids, wts, cnt, moe_w1[i], moe_w3[i], moe_w2[i])
    return h[:, CTX:, :]
```

```python
import functools
import math

import numpy as np
import jax
import jax.numpy as jnp
from jax import lax
from jax.experimental import pallas as pl
from jax.experimental.pallas import tpu as pltpu

F32 = jnp.float32
BF = jnp.bfloat16
HIGHEST = lax.Precision.HIGHEST

D = 1024
CTX = 256
GRID_W = 64
EPS = 1e-6
ROPE_BASE = 10000.0
DEPTH = 4
LANES = 128
TM = 256

DA_HEADS = 8
DA_HEAD_DIM = 64

GM_CHUNK = 128
GM_GROUPS = 8

SSM_INNER = 2048
SSM_HEADS = 32
SSM_GROUPS = 4
SSM_STATE = 128
SSM_CONV = 5
SSM_CHUNK = 128
SSM_CONV_DIM = SSM_INNER + 2 * SSM_GROUPS * SSM_STATE

MLA_HEADS = 16
MLA_Q_RANK = 384
MLA_KV_RANK = 256
MLA_NOPE = 64
MLA_ROPE = 32
MLA_V = 64
MLA_QK = MLA_NOPE + MLA_ROPE

MOE_GROUPS = 4
MOE_PER_GROUP = 8
MOE_EXPERTS = 32
MOE_FF = 512
MOE_TM = 256

VMEM_LIMIT = 56 * 1024 * 1024


def _cparams(sem):
    return pltpu.CompilerParams(dimension_semantics=sem, vmem_limit_bytes=VMEM_LIMIT)


def _full_spec(arr):
    nd = arr.ndim
    return pl.BlockSpec(arr.shape, lambda *_: (0,) * nd)


def _tok_spec(width, col=0):
    return pl.BlockSpec((None, TM, width), lambda b, t: (b, t, col))


def _mod_spec():
    return pl.BlockSpec((None, None, 8, D), lambda b, t: (b, jnp.minimum(t, 1), 0, 0))


def _silu(x):
    return x * (1.0 / (1.0 + jnp.exp(-x)))


def _norm_mod(h, g, scale, shift):
    ms = jnp.mean(h * h, axis=-1, keepdims=True)
    y = h * lax.rsqrt(ms + EPS) * g
    return y * (1.0 + scale) + shift


def _ada_kernel(c_ref, w_ref, b_ref, o_ref):
    a = _silu(c_ref[...]).astype(BF)
    o_ref[...] = jnp.dot(a, w_ref[...].astype(BF), preferred_element_type=F32) + b_ref[...]


def _ada_all(cc, ada_w, ada_b):
    rows = cc.shape[0]
    tn = 1536
    return pl.pallas_call(
        _ada_kernel,
        grid=(DEPTH, 6 * D // tn),
        in_specs=[pl.BlockSpec((rows, D), lambda l, j: (0, 0)),
                  pl.BlockSpec((None, D, tn), lambda l, j: (l, 0, j)),
                  pl.BlockSpec((None, 1, tn), lambda l, j: (l, 0, j))],
        out_specs=pl.BlockSpec((None, rows, tn), lambda l, j: (l, 0, j)),
        out_shape=jax.ShapeDtypeStruct((DEPTH, rows, 6 * D), F32),
        compiler_params=_cparams(("arbitrary", "arbitrary")),
        name="ada_mod",
    )(cc, ada_w, ada_b.reshape(DEPTH, 1, 6 * D))


def _residual_and_route(delta, h_ref, mod, n2g_ref, wr_ref, br_ref, outs):
    ho_ref, xo_ref, id_ref, wt_ref, cnt_ref = outs
    hn = h_ref[...] + mod[2:3] * delta
    ho_ref[...] = hn
    x2 = _norm_mod(hn, n2g_ref[...], mod[4:5], mod[3:4])
    xo_ref[...] = x2
    lg = jnp.dot(x2.astype(BF), wr_ref[...], preferred_element_type=F32) + br_ref[...]
    lane = lax.broadcasted_iota(jnp.int32, lg.shape, 1)
    neg = jnp.float32(-jnp.inf)
    big = jnp.int32(1 << 20)
    is_g = jnp.logical_and(lane >= MOE_EXPERTS, lane < MOE_EXPERTS + MOE_GROUPS)
    gl = jnp.where(is_g, lg, neg)
    gmax = jnp.max(gl, axis=-1, keepdims=True)
    gidx = jnp.min(jnp.where(gl == gmax, lane, big), axis=-1, keepdims=True) - MOE_EXPERTS
    pg = 1.0 / jnp.sum(jnp.where(is_g, jnp.exp(gl - gmax), 0.0), axis=-1, keepdims=True)
    lo = gidx * MOE_PER_GROUP
    in_grp = jnp.logical_and(lane >= lo, lane < lo + MOE_PER_GROUP)
    el = jnp.where(in_grp, lg, neg)
    e1 = jnp.max(el, axis=-1, keepdims=True)
    i1 = jnp.min(jnp.where(el == e1, lane, big), axis=-1, keepdims=True)
    el2 = jnp.where(lane == i1, neg, el)
    e2 = jnp.max(el2, axis=-1, keepdims=True)
    i2 = jnp.min(jnp.where(el2 == e2, lane, big), axis=-1, keepdims=True)
    t = jnp.exp(e2 - e1)
    w1 = pg / (1.0 + t)
    w2 = pg * t / (1.0 + t)
    id_ref[...] = jnp.where(lane == 0, i1, jnp.where(lane == 1, i2, 0))
    wt_ref[...] = jnp.where(lane == 0, w1, jnp.where(lane == 1, w2, 0.0))
    chosen = jnp.where(jnp.logical_or(lane == i1, lane == i2), 1.0, 0.0)
    cnt_ref[...] = jnp.sum(chosen, axis=0, keepdims=True)


def _tail_in_specs(n2g, wr, br):
    return [_full_spec(n2g), _full_spec(wr), _full_spec(br)]


def _tail_out(batch, t_all):
    specs = [_tok_spec(D), _tok_spec(D), _tok_spec(LANES), _tok_spec(LANES),
             pl.BlockSpec((None, None, 1, LANES), lambda b, t: (b, t, 0, 0))]
    shapes = [jax.ShapeDtypeStruct((batch, t_all, D), F32),
              jax.ShapeDtypeStruct((batch, t_all, D), F32),
              jax.ShapeDtypeStruct((batch, t_all, LANES), jnp.int32),
              jax.ShapeDtypeStruct((batch, t_all, LANES), F32),
              jax.ShapeDtypeStruct((batch, t_all // TM, 1, LANES), F32)]
    return specs, shapes


def _outproj_kernel(a_ref, w_ref, h_ref, mod_ref, n2g_ref, wr_ref, br_ref, *outs):
    delta = jnp.dot(a_ref[...], w_ref[...], preferred_element_type=F32)
    _residual_and_route(delta, h_ref, mod_ref[...], n2g_ref, wr_ref, br_ref, outs)


def _outproj(a, w, h, mod, n2g, wr, br):
    batch, t_all, k = a.shape
    out_specs, out_shapes = _tail_out(batch, t_all)
    return pl.pallas_call(
        _outproj_kernel,
        grid=(batch, t_all // TM),
        in_specs=[_tok_spec(k), _full_spec(w), _tok_spec(D), _mod_spec()] + _tail_in_specs(n2g, wr, br),
        out_specs=out_specs,
        out_shape=out_shapes,
        compiler_params=_cparams(("parallel", "parallel")),
        name="outproj_route",
    )(a, w, h, mod, n2g, wr, br)


def _rope_tables(t_lat, rot_dim):
    rows = t_lat // GRID_W
    row = np.repeat(np.arange(rows, dtype=np.float32), GRID_W)
    col = np.tile(np.arange(GRID_W, dtype=np.float32), rows)
    n_freq = rot_dim // 4
    inv_freq = jnp.asarray(ROPE_BASE, F32) ** (-jnp.arange(n_freq, dtype=F32) / n_freq)
    ang = jnp.concatenate([jnp.asarray(row)[:, None] * inv_freq, jnp.asarray(col)[:, None] * inv_freq], axis=-1)
    return jnp.cos(ang), jnp.sin(ang)


def _with_ctx_identity(cos_l, sin_l):
    cos = jnp.concatenate([jnp.ones((CTX, LANES), F32), cos_l], axis=0)
    sin = jnp.concatenate([jnp.zeros((CTX, LANES), F32), sin_l], axis=0)
    return cos, sin


def _da_perm():
    perm = np.zeros(2 * DA_HEADS * DA_HEAD_DIM, np.int32)
    for h in range(DA_HEADS):
        for m in range(2):
            for i in range(DA_HEAD_DIM):
                new = h * LANES + (i // 32) * 64 + m * 32 + (i % 32)
                perm[new] = (2 * h + m) * DA_HEAD_DIM + i
    return perm


def _da_lane_of_dim():
    lane = np.arange(LANES)
    return (lane // 64) * 32 + lane % 32


def _da_proj_kernel(h_ref, mod_ref, n1g_ref, w_ref, gq_ref, gk_ref, cos_ref, sin_ref, gavg_ref,
                    q_ref, k_ref, v_ref):
    mod = mod_ref[...]
    xn = _norm_mod(h_ref[...], n1g_ref[...], mod[1:2], mod[0:1]).astype(BF)
    cos = cos_ref[...]
    sin = sin_ref[...]
    gavg = gavg_ref[...]
    nq = DA_HEADS * LANES
    scale = DA_HEAD_DIM ** -0.5 * LOG2E
    for dst, off, g_ref, mult in ((q_ref, 0, gq_ref, scale), (k_ref, nq, gk_ref, 1.0)):
        y_all = jnp.dot(xn, w_ref[:, off:off + nq], preferred_element_type=F32)
        for jj in range(DA_HEADS // 2):
            yy = y_all[:, jj * 2 * LANES:(jj + 1) * 2 * LANES]
            ms2 = jnp.dot((yy * yy).astype(BF), gavg, preferred_element_type=F32)
            for j in (2 * jj, 2 * jj + 1):
                y = y_all[:, j * LANES:(j + 1) * LANES]
                ms = ms2[:, (j % 2) * LANES:(j % 2 + 1) * LANES]
                y = y * lax.rsqrt(ms + EPS) * g_ref[...]
                y = y * cos + pltpu.roll(y, 64, 1) * sin
                dst[:, j * LANES:(j + 1) * LANES] = (y * mult).astype(BF)
    v_ref[...] = jnp.dot(xn, w_ref[:, 2 * nq:3 * nq], preferred_element_type=F32).astype(BF)


LOG2E = math.log2(math.e)


def _softmax_pv(q, k, v):
    s = lax.dot_general(q, k, (((1,), (1,)), ((), ())), preferred_element_type=F32)
    e = jnp.exp2(s - jnp.max(s, axis=-1, keepdims=True))
    l = jnp.sum(e, axis=-1, keepdims=True)
    return jnp.dot(e.astype(BF), v, preferred_element_type=F32) / l


def _da_attn_kernel(lam_ref, q_ref, k_ref, v_ref, subg_ref, o_ref, *, lam_init):
    q = q_ref[...]
    lane = lax.broadcasted_iota(jnp.int32, q.shape, 1)
    comp0 = ((lane // 32) % 2) == 0
    zero = jnp.zeros_like(q)
    q0 = jnp.where(comp0, q, zero)
    q1 = jnp.where(comp0, zero, q)

    def run(k, v):
        o = _softmax_pv(q0, k, v) - lam_ref[0] * _softmax_pv(q1, k, v)
        ms = jnp.mean(o * o, axis=-1, keepdims=True)
        o = o * lax.rsqrt(ms + EPS) * subg_ref[...] * (1.0 - lam_init)
        o_ref[...] = o.astype(BF)

    is_ctx = pl.program_id(2) == 0

    @pl.when(is_ctx)
    def _():
        run(k_ref[0:CTX, :], v_ref[0:CTX, :])

    @pl.when(jnp.logical_not(is_ctx))
    def _():
        run(k_ref[...], v_ref[...])


def _diff_attention(h, mod, n1g, w_in, q_g, k_g, lam_q1, lam_k1, lam_q2, lam_k2, sub_g, layer_idx):
    batch, t_all, _ = h.shape
    nq = DA_HEADS * LANES
    perm = _da_perm()
    w = jnp.concatenate([w_in[:, :nq][:, perm], w_in[:, nq:2 * nq][:, perm], w_in[:, 2 * nq:]], axis=1).astype(BF)
    ldim = _da_lane_of_dim()
    gq = q_g[ldim].reshape(1, LANES)
    gk = k_g[ldim].reshape(1, LANES)
    cos, sin = _rope_tables(t_all - CTX, DA_HEAD_DIM)
    cos, sin = _with_ctx_identity(jnp.tile(cos, (1, 4)), jnp.concatenate([-sin, -sin, sin, sin], axis=1))
    unit = np.arange(2 * LANES) // LANES * 2 + (np.arange(2 * LANES) // 32) % 2
    gavg = jnp.asarray((unit[:, None] == unit[None, :]).astype(np.float32) / DA_HEAD_DIM).astype(BF)
    tab_spec = pl.BlockSpec((TM, LANES), lambda b, t: (t, 0))
    qkv_shape = jax.ShapeDtypeStruct((batch, t_all, nq), BF)
    q, k, v = pl.pallas_call(
        _da_proj_kernel,
        grid=(batch, t_all // TM),
        in_specs=[_tok_spec(D), _mod_spec(), _full_spec(n1g), _full_spec(w), _full_spec(gq), _full_spec(gk),
                  tab_spec, tab_spec, _full_spec(gavg)],
        out_specs=[_tok_spec(nq)] * 3,
        out_shape=[qkv_shape] * 3,
        compiler_params=_cparams(("parallel", "parallel")),
        name="da_proj",
    )(h, mod, n1g, w, gq, gk, cos, sin, gavg)

    lam_init = 0.8 - 0.6 * math.exp(-0.3 * layer_idx)
    lam = (jnp.exp(jnp.sum(lam_q1 * lam_k1)) - jnp.exp(jnp.sum(lam_q2 * lam_k2)) + lam_init).reshape(1)
    kv_spec = pl.BlockSpec((None, t_all, LANES), lambda b, hh, t: (b, 0, hh))
    qo_spec = pl.BlockSpec((None, TM, LANES), lambda b, hh, t: (b, t, hh))
    subg = sub_g.reshape(1, LANES)
    return pl.pallas_call(
        functools.partial(_da_attn_kernel, lam_init=lam_init),
        grid=(batch, DA_HEADS, t_all // TM),
        in_specs=[pl.BlockSpec(memory_space=pltpu.SMEM), qo_spec, kv_spec, kv_spec,
                  pl.BlockSpec((1, LANES), lambda b, hh, t: (0, 0))],
        out_specs=qo_spec,
        out_shape=jax.ShapeDtypeStruct((batch, t_all, nq), BF),
        compiler_params=_cparams(("parallel", "parallel", "arbitrary")),
        name="da_attn",
    )(lam, q, k, v, subg)


def _gmlp_kernel(h_ref, mod_ref, n1g_ref, win_ref, vg_ref, ws_ref, bs_ref, wout_ref,
                 n2g_ref, wr_ref, br_ref, *outs):
    mod = mod_ref[...]
    xn = _norm_mod(h_ref[...], n1g_ref[...], mod[1:2], mod[0:1]).astype(BF)
    uv =jnp.dot(xn, win_ref[...], preferred_element_type=F32)
    uv = 0.5 * uv * (1.0 + lax.erf(uv * (2.0 ** -0.5)))
    u = uv[:, :D]
    v = uv[:, D:]
    v = (v * lax.rsqrt(jnp.mean(v * v, axis=-1, keepdims=True) + EPS) * vg_ref[...]).astype(BF)
    rows = []
    for c in range(TM // GM_CHUNK):
        cols = []
        for g in range(GM_GROUPS):
            vb = v[c * GM_CHUNK:(c + 1) * GM_CHUNK, g * LANES:(g + 1) * LANES]
            s = jnp.dot(ws_ref[g], vb, preferred_element_type=F32) + bs_ref[g]
            cols.append(s)
        rows.append(jnp.concatenate(cols, axis=1))
    s_all = jnp.concatenate(rows, axis=0)
    delta = jnp.dot((u * s_all).astype(BF), wout_ref[...], preferred_element_type=F32)
    _residual_and_route(delta, h_ref, mod, n2g_ref, wr_ref, br_ref, outs)


def _gmlp_layer(h, mod, n1g, w_in, v_g, w_s, b_s, w_out, n2g, wr, br):
    batch, t_all, _ = h.shape
    win = w_in.astype(BF)
    wout = w_out.astype(BF)
    ws = w_s.astype(BF)
    bs = jnp.broadcast_to(b_s[:, :, None], (GM_GROUPS, GM_CHUNK, LANES)).astype(F32)
    vg = v_g.reshape(1, D)
    out_specs, out_shapes = _tail_out(batch, t_all)
    return pl.pallas_call(
        _gmlp_kernel,
        grid=(batch, t_all // TM),
        in_specs=[_tok_spec(D), _mod_spec(), _full_spec(n1g), _full_spec(win), _full_spec(vg), _full_spec(ws),
                  _full_spec(bs), _full_spec(wout)] + _tail_in_specs(n2g, wr, br),
        out_specs=out_specs,
        out_shape=out_shapes,
        compiler_params=_cparams(("parallel", "parallel")),
        name="gmlp_layer",
    )(h, mod, n1g, win, vg, ws, bs, wout, n2g, wr, br)


SSM_DT_PAD = LANES


def _ssm_proj_kernel(h_ref, mod_ref, n1g_ref, w_ref, dtb_ref, z_ref, xbc_ref, dt_ref):
    mod = mod_ref[...]
    xn = _norm_mod(h_ref[...], n1g_ref[...], mod[1:2], mod[0:1]).astype(BF)
    z_ref[...] = jnp.dot(xn, w_ref[:, :SSM_INNER], preferred_element_type=F32).astype(BF)
    c0 = SSM_INNER
    for j in range(SSM_CONV_DIM // 1024):
        xbc_ref[:, j * 1024:(j + 1) * 1024] = jnp.dot(
            xn, w_ref[:, c0 + j * 1024:c0 + (j + 1) * 1024], preferred_element_type=F32).astype(BF)
    c1 = SSM_INNER + SSM_CONV_DIM
    raw = jnp.dot(xn, w_ref[:, c1:c1 + SSM_DT_PAD], preferred_element_type=F32) + dtb_ref[...]
    dt_ref[...] = jnp.maximum(raw, 0.0) + jnp.log1p(jnp.exp(-jnp.abs(raw)))


CONV_PAD = 8
CONV_W = 512


def _conv_kernel(x_ref, w_ref, b_ref, o_ref, pad_ref):
    t_all = x_ref.shape[0]
    t_lat = t_all - CTX
    lat0 = CTX + 2 * CONV_PAD
    zeros = jnp.zeros((CONV_PAD, CONV_W), F32)
    pad_ref[0:CONV_PAD, :] = zeros
    pad_ref[CONV_PAD + CTX:lat0, :] = jnp.zeros((CONV_PAD, CONV_W), F32)
    pad_ref[lat0 + t_lat:lat0 + t_lat + CONV_PAD, :] = zeros
    pad_ref[CONV_PAD:CONV_PAD + CTX, :] = x_ref[0:CTX, :].astype(F32)
    pad_ref[lat0:lat0 + t_lat, :] = x_ref[CTX:t_all, :].astype(F32)
    half = (SSM_CONV - 1) // 2
    for start, n, dst in ((CONV_PAD, CTX, 0), (lat0, t_lat, CTX)):
        acc = jnp.zeros((n, CONV_W), F32) + b_ref[...]
        for kk in range(SSM_CONV):
            acc = acc + pad_ref[start + kk - half:start + kk - half + n, :] * w_ref[kk:kk + 1, :]
        o_ref[dst:dst + n, :] = _silu(acc).astype(BF)


def _ssd_kernel(dt_ref, a_ref, x_ref, b_ref, c_ref, y_ref, s_ref):
    direction = pl.program_id(1)
    step = pl.program_id(2)

    @pl.when(step == 0)
    def _():
        s_ref[...] = jnp.zeros_like(s_ref)

    q = SSM_CHUNK
    row = lax.broadcasted_iota(jnp.int32, (q, q), 0)
    col = lax.broadcasted_iota(jnp.int32, (q, q), 1)
    sign = 1 - 2 * direction
    tri_b = (col - row) * sign <= 0
    tri = tri_b.astype(F32)
    tri_t = ((row - col) * sign <= 0).astype(F32)
    eye = (row == col).astype(F32)
    lane_lo = col < 64

    dt_t = dt_ref[...]
    a_t = dt_t * a_ref[...]
    cs_rows = jnp.dot(a_t, tri_t, preferred_element_type=F32, precision=HIGHEST)
    tot = jnp.sum(a_t, axis=1, keepdims=True)
    heads_per_group = SSM_HEADS // SSM_GROUPS

    for g in range(SSM_GROUPS):
        bm = b_ref[:, g * SSM_STATE:(g + 1) * SSM_STATE]
        cm = c_ref[:, g * SSM_STATE:(g + 1) * SSM_STATE]
        cb = lax.dot_general(cm, bm, (((1,), (1,)), ((), ())), preferred_element_type=F32)
        bm_t = bm.astype(F32).T.astype(BF)
        for pp in range(heads_per_group // 2):
            p = g * (heads_per_group // 2) + pp
            xp = x_ref[:, p * LANES:(p + 1) * LANES].astype(F32)
            cs_col, dt_col, mats = [], [], []
            for hh in range(2):
                e = 2 * p + hh
                cc = jnp.sum(tri * a_t[e:e + 1, :], axis=1, keepdims=True)
                cs_col.append(cc)
                dt_col.append(jnp.sum(eye * dt_t[e:e + 1, :], axis=1, keepdims=True))
                decay = jnp.exp(jnp.where(tri_b, cc - cs_rows[e:e + 1, :], -jnp.inf))
                mats.append((cb * decay).astype(BF))
            xdt = xp * jnp.where(lane_lo, dt_col[0], dt_col[1])
            xb = xdt.astype(BF)
            y_diag = jnp.where(lane_lo,
                               jnp.dot(mats[0], xb, preferred_element_type=F32),
                               jnp.dot(mats[1], xb, preferred_element_type=F32))
            s_prev = s_ref[p]
            y_off = jnp.dot(cm, s_prev.astype(BF), preferred_element_type=F32)
            y_off = y_off * jnp.exp(jnp.where(lane_lo, cs_col[0], cs_col[1]))
            y_ref[:, p * LANES:(p + 1) * LANES] = y_diag + y_off
            t0 = tot[2 * p:2 * p + 1, :]
            t1 = tot[2 * p + 1:2 * p + 2, :]
            to_end = jnp.exp(jnp.where(lane_lo, t0 - cs_col[0], t1 - cs_col[1]))
            upd = jnp.dot(bm_t, (xdt * to_end).astype(BF), preferred_element_type=F32)
            s_ref[p] = s_prev * jnp.exp(jnp.where(lane_lo[0:1, :], t0, t1)) + upd


def _ssm_finish_kernel(yf_ref, yb_ref, xs_ref, z_ref, d_ref, og_ref, w_ref, h_ref, mod_ref,
                       n2g_ref, wr_ref, br_ref, *outs):
    y = yf_ref[...] + yb_ref[...] + d_ref[...] * xs_ref[...].astype(F32)
    y = y * _silu(z_ref[...].astype(F32))
    y = y * lax.rsqrt(jnp.mean(y * y, axis=-1, keepdims=True) + EPS) * og_ref[...]
    delta = jnp.dot(y.astype(BF), w_ref[...], preferred_element_type=F32)
    _residual_and_route(delta, h_ref, mod_ref[...], n2g_ref, wr_ref, br_ref, outs)


def _mamba_layer(h, mod, n1g, w_in, conv_w, conv_b, dt_bias, a_log, d_skip, out_g, w_out, n2g, wr, br):
    batch, t_all, _ = h.shape
    n_dt = 2 * SSM_HEADS
    c1 = SSM_INNER + SSM_CONV_DIM
    w = jnp.concatenate([w_in, jnp.zeros((D, SSM_DT_PAD - n_dt), F32)], axis=1).astype(BF)
    dtb = jnp.concatenate([dt_bias.reshape(n_dt), jnp.zeros((SSM_DT_PAD - n_dt,), F32)]).reshape(1, SSM_DT_PAD)
    z, xbc, dt = pl.pallas_call(
        _ssm_proj_kernel,
        grid=(batch, t_all // TM),
        in_specs=[_tok_spec(D), _mod_spec(), _full_spec(n1g), _full_spec(w), _full_spec(dtb)],
        out_specs=[_tok_spec(SSM_INNER), _tok_spec(SSM_CONV_DIM), _tok_spec(SSM_DT_PAD)],
        out_shape=[jax.ShapeDtypeStruct((batch, t_all, SSM_INNER), BF),
                   jax.ShapeDtypeStruct((batch, t_all, SSM_CONV_DIM), BF),
                   jax.ShapeDtypeStruct((batch, t_all, SSM_DT_PAD), F32)],
        compiler_params=_cparams(("parallel", "parallel")),
        name="ssm_proj",
    )(h, mod, n1g, w, dtb)
    assert c1 + n_dt == w_in.shape[1]

    cw = jnp.concatenate([conv_w, jnp.zeros((8 - SSM_CONV, SSM_CONV_DIM), F32)], axis=0)
    cbias = conv_b.reshape(1, SSM_CONV_DIM)
    xbc = pl.pallas_call(
        _conv_kernel,
        grid=(batch, SSM_CONV_DIM // CONV_W),
        in_specs=[pl.BlockSpec((None, t_all, CONV_W), lambda b, j: (b, 0, j)),
                  pl.BlockSpec((8, CONV_W), lambda b, j: (0, j)),
                  pl.BlockSpec((1, CONV_W), lambda b, j: (0, j))],
        out_specs=pl.BlockSpec((None, t_all, CONV_W), lambda b, j: (b, 0, j)),
        out_shape=jax.ShapeDtypeStruct((batch, t_all, SSM_CONV_DIM), BF),
        scratch_shapes=[pltpu.VMEM((t_all + 3 * CONV_PAD, CONV_W), F32)],
        compiler_params=_cparams(("parallel", "parallel")),
        name="ssm_conv",
    )(xbc, cw, cbias)

    n_chunks = t_all // SSM_CHUNK
    ctx_chunks = CTX // SSM_CHUNK
    dt_t = jnp.swapaxes(dt[:, :, :n_dt], 1, 2)
    a_neg = jnp.broadcast_to(-jnp.exp(a_log.astype(F32)).reshape(2, SSM_HEADS, 1), (2, SSM_HEADS, SSM_CHUNK))

    def chunk_of(dr, s):
        rev = jnp.where(s < ctx_chunks, ctx_chunks - 1 - s, n_chunks - 1 + ctx_chunks - s)
        return jnp.where(dr == 0, s, rev)

    gs = SSM_GROUPS * SSM_STATE
    y = pl.pallas_call(
        _ssd_kernel,
        grid=(batch, 2, n_chunks),
        in_specs=[pl.BlockSpec((None, SSM_HEADS, SSM_CHUNK), lambda b, dr, s: (b, dr, chunk_of(dr, s))),
                  pl.BlockSpec((None, SSM_HEADS, SSM_CHUNK), lambda b, dr, s: (dr, 0, 0)),
                  pl.BlockSpec((None, SSM_CHUNK, SSM_INNER), lambda b, dr, s: (b, chunk_of(dr, s), 0)),
                  pl.BlockSpec((None, SSM_CHUNK, gs), lambda b, dr, s: (b, chunk_of(dr, s), SSM_INNER // gs)),
                  pl.BlockSpec((None, SSM_CHUNK, gs), lambda b, dr, s: (b, chunk_of(dr, s), SSM_INNER // gs + 1))],
        out_specs=pl.BlockSpec((None, None, SSM_CHUNK, SSM_INNER), lambda b, dr, s: (dr, b, chunk_of(dr, s), 0)),
        out_shape=jax.ShapeDtypeStruct((2, batch, t_all, SSM_INNER), F32),
        scratch_shapes=[pltpu.VMEM((SSM_HEADS // 2, SSM_STATE, LANES), F32)],
        compiler_params=_cparams(("parallel", "parallel", "arbitrary")),
        name="ssd_scan",
    )(dt_t, a_neg, xbc, xbc, xbc)

    dvec = jnp.repeat(d_skip, SSM_INNER // SSM_HEADS).reshape(1, SSM_INNER)
    og = out_g.reshape(1, SSM_INNER)
    wo = w_out.astype(BF)
    out_specs, out_shapes = _tail_out(batch, t_all)
    y_spec = lambda dr: pl.BlockSpec((None, None, TM, SSM_INNER), lambda b, t: (dr, b, t, 0))
    return pl.pallas_call(
        _ssm_finish_kernel,
        grid=(batch, t_all // TM),
        in_specs=[y_spec(0), y_spec(1), _tok_spec(SSM_INNER), _tok_spec(SSM_INNER), _full_spec(dvec),
                  _full_spec(og), _full_spec(wo), _tok_spec(D), _mod_spec()] + _tail_in_specs(n2g, wr, br),
        out_specs=out_specs,
        out_shape=out_shapes,
        compiler_params=_cparams(("parallel", "parallel")),
        name="ssm_finish",
    )(y, y, xbc, z, dvec, og, wo, h, mod, n2g, wr, br)


def _mla_lane_src():
    src = np.full(LANES, -1, np.int32)
    src[0:16] = MLA_NOPE + np.arange(16)
    src[16:64] = np.arange(48)
    src[64:80] = MLA_NOPE + 16 + np.arange(16)
    src[80:96] = 48 + np.arange(16)
    return src


def _mla_proj_kernel(h_ref, mod_ref, n1g_ref, win_ref, qng_ref, kvng_ref, wuq_ref, wuk_ref, wuv_ref,
                     gq_ref, gk_ref, cos_ref, sin_ref, avg_ref, q_ref, k_ref, v_ref):
    mod = mod_ref[...]
    xn = _norm_mod(h_ref[...], n1g_ref[...], mod[1:2], mod[0:1]).astype(BF)
    lat = jnp.dot(xn, win_ref[...], preferred_element_type=F32)
    cq = lat[:, :MLA_Q_RANK]
    ckv = lat[:, MLA_Q_RANK:MLA_Q_RANK + MLA_KV_RANK]
    kpe = lat[:, MLA_Q_RANK + MLA_KV_RANK:]
    cq = (cq * lax.rsqrt(jnp.mean(cq * cq, axis=-1, keepdims=True) + EPS) * qng_ref[...]).astype(BF)
    ckv = (ckv * lax.rsqrt(jnp.mean(ckv * ckv, axis=-1, keepdims=True) + EPS) * kvng_ref[...]).astype(BF)
    cos = cos_ref[...]
    sin = sin_ref[...]
    avg = avg_ref[...]
    scale = MLA_QK ** -0.5 * LOG2E
    qa = jnp.dot(cq, wuq_ref[...], preferred_element_type=F32)
    ka = jnp.dot(ckv, wuk_ref[...], preferred_element_type=F32)
    kpe2 = jnp.concatenate([kpe, kpe], axis=1)
    for jj in range(MLA_HEADS // 2):
        for dst, src, add, g_ref, mult in ((q_ref, qa, None, gq_ref, scale), (k_ref, ka, kpe2, gk_ref, 1.0)):
            yy = src[:, jj * 2 * LANES:(jj + 1) * 2 * LANES]
            if add is not None:
                yy = yy + add
            ms2 = jnp.dot((yy * yy).astype(BF), avg, preferred_element_type=F32) * (1.0 / MLA_QK)
            for hh in range(2):
                j = 2 * jj + hh
                y = yy[:, hh * LANES:(hh + 1) * LANES]
                y = y * lax.rsqrt(ms2[:, hh * LANES:(hh + 1) * LANES] + EPS) * g_ref[...]
                y = y * cos + pltpu.roll(y, 64, 1) * sin
                dst[:, j * LANES:(j + 1) * LANES] = (y * mult).astype(BF)
    v_ref[...] = jnp.dot(ckv, wuv_ref[...], preferred_element_type=F32).astype(BF)


def _mla_attn_kernel(q_ref, k_ref, v_ref, o_ref):
    lane = lax.broadcasted_iota(jnp.int32, (TM, LANES), 1)

    def run(n_keys):
        v = v_ref[0:n_keys, :]
        outs = [_softmax_pv(q_ref[:, hh * LANES:(hh + 1) * LANES], k_ref[0:n_keys, hh * LANES:(hh + 1) * LANES], v)
                for hh in range(2)]
        o_ref[...] = jnp.where(lane < MLA_V, outs[0], outs[1]).astype(BF)

    is_ctx = pl.program_id(2) == 0

    @pl.when(is_ctx)
    def _():
        run(CTX)

    @pl.when(jnp.logical_not(is_ctx))
    def _():
        run(v_ref.shape[0])


def _mla_attention(h, mod, n1g, w_in, q_norm_g, kv_norm_g, w_uq, w_ukv, q_g, k_g):
    batch, t_all, _ = h.shape
    src = _mla_lane_src()
    used = src >= 0
    srcc = np.where(used, src, 0)
    nh = MLA_HEADS * LANES
    kpe_cols = np.where(srcc >= MLA_NOPE, MLA_Q_RANK + MLA_KV_RANK + srcc - MLA_NOPE, 0)
    kpe_used = np.logical_and(used, src >= MLA_NOPE)
    w_kpe = jnp.where(jnp.asarray(kpe_used)[None, :], w_in[:, kpe_cols], 0.0)
    win = jnp.concatenate([w_in[:, :MLA_Q_RANK + MLA_KV_RANK], w_kpe], axis=1).astype(BF)
    q_cols = (np.arange(MLA_HEADS)[:, None] * MLA_QK + srcc[None, :]).reshape(-1)
    q_mask = jnp.asarray(np.tile(used, MLA_HEADS))[None, :]
    wuq = jnp.where(q_mask, w_uq[:, q_cols], 0.0).astype(BF)
    nope_used = np.logical_and(used, src < MLA_NOPE)
    k_cols = (np.arange(MLA_HEADS)[:, None] * (MLA_NOPE + MLA_V) + np.where(nope_used, srcc, 0)[None, :]).reshape(-1)
    k_mask = jnp.asarray(np.tile(nope_used, MLA_HEADS))[None, :]
    wuk = jnp.where(k_mask, w_ukv[:, k_cols], 0.0).astype(BF)
    v_cols = (np.arange(MLA_HEADS)[:, None] * (MLA_NOPE + MLA_V) + MLA_NOPE + np.arange(MLA_V)[None, :]).reshape(-1)
    wuv = w_ukv[:, v_cols].astype(BF)
    gq = jnp.where(jnp.asarray(used), q_g[srcc], 0.0).reshape(1, LANES)
    gk = jnp.where(jnp.asarray(used), k_g[srcc], 0.0).reshape(1, LANES)
    cos16, sin16 = _rope_tables(t_all - CTX, MLA_ROPE)
    t_lat = t_all - CTX
    cos_l = jnp.ones((t_lat, LANES), F32).at[:, 0:16].set(cos16).at[:, 64:80].set(cos16)
    sin_l = jnp.zeros((t_lat, LANES), F32).at[:, 0:16].set(-sin16).at[:, 64:80].set(sin16)
    cos, sin = _with_ctx_identity(cos_l, sin_l)
    head_of_lane = np.arange(2 * LANES) // LANES
    avg = jnp.asarray((head_of_lane[:, None] == head_of_lane[None, :]).astype(np.float32)).astype(BF)
    qng = q_norm_g.reshape(1, MLA_Q_RANK)
    kvng = kv_norm_g.reshape(1, MLA_KV_RANK)
    tab_spec = pl.BlockSpec((TM, LANES), lambda b, t: (t, 0))
    q, k, v = pl.pallas_call(
        _mla_proj_kernel,
        grid=(batch, t_all // TM),
        in_specs=[_tok_spec(D), _mod_spec(), _full_spec(n1g), _full_spec(win), _full_spec(qng), _full_spec(kvng),
                  _full_spec(wuq), _full_spec(wuk), _full_spec(wuv), _full_spec(gq), _full_spec(gk),
                  tab_spec, tab_spec, _full_spec(avg)],
        out_specs=[_tok_spec(nh), _tok_spec(nh), _tok_spec(MLA_HEADS * MLA_V)],
        out_shape=[jax.ShapeDtypeStruct((batch, t_all, nh), BF), jax.ShapeDtypeStruct((batch, t_all, nh), BF),
                   jax.ShapeDtypeStruct((batch, t_all, MLA_HEADS * MLA_V), BF)],
        compiler_params=_cparams(("parallel", "parallel")),
        name="mla_proj",
    )(h, mod, n1g, win, qng, kvng, wuq, wuk, wuv, gq, gk, cos, sin, avg)

    return pl.pallas_call(
        _mla_attn_kernel,
        grid=(batch, MLA_HEADS // 2, t_all // TM),
        in_specs=[pl.BlockSpec((None, TM, 2 * LANES), lambda b, p, t: (b, t, p)),
                  pl.BlockSpec((None, t_all, 2 * LANES), lambda b, p, t: (b, 0, p)),
                  pl.BlockSpec((None, t_all, LANES), lambda b, p, t: (b, 0, p))],
        out_specs=pl.BlockSpec((None, TM, LANES), lambda b, p, t: (b, t, p)),
        out_shape=jax.ShapeDtypeStruct((batch, t_all, MLA_HEADS * MLA_V), BF),
        compiler_params=_cparams(("parallel", "parallel", "arbitrary")),
        name="mla_attn",
    )(q, k, v)


def _plan_kernel(ids_ref, base_ref, dest_ref):
    ids = ids_ref[...]
    lane = lax.broadcasted_iota(jnp.int32, ids.shape, 1)
    sel1 = lane == ids[:, 0:1]
    sel2 = lane == ids[:, 1:2]
    chosen = jnp.where(jnp.logical_or(sel1, sel2), 1.0, 0.0).astype(BF)
    r = lax.broadcasted_iota(jnp.int32, (TM, TM), 0)
    c = lax.broadcasted_iota(jnp.int32, (TM, TM), 1)
    earlier = jnp.where(c < r, 1.0, 0.0).astype(BF)
    row = jnp.dot(earlier, chosen, preferred_element_type=F32) + base_ref[...]
    d1 = jnp.sum(jnp.where(sel1, row, 0.0), axis=-1, keepdims=True)
    d2 = jnp.sum(jnp.where(sel2, row, 0.0), axis=-1, keepdims=True)
    dest_ref[...] = jnp.where(lane == 0, d1, jnp.where(lane == 1, d2, 0.0)).astype(jnp.int32)


ROWS_PER_ISSUE = 8


def _row_copy(src, src_row, dst, dst_row, sem):
    return pltpu.make_async_copy(src.at[pl.ds(src_row, 1), :], dst.at[pl.ds(dst_row, 1), :], sem)


def _dispatch_kernel(ztile_ref, dest_ref, x_ref, xs_hbm, zbuf, zsem, sem):
    def zero_copy(e):
        start = pl.multiple_of(ztile_ref[e], MOE_TM)
        return pltpu.make_async_copy(zbuf, xs_hbm.at[pl.ds(start, MOE_TM), :], zsem)

    @pl.when(pl.program_id(0) == 0)
    def _():
        zbuf[...] = jnp.zeros_like(zbuf)
        for e in range(2 * MOE_EXPERTS):
            @pl.when(ztile_ref[e] >= 0)
            def _():
                zero_copy(e).start()
        for e in range(2 * MOE_EXPERTS):
            @pl.when(ztile_ref[e] >= 0)
            def _():
                zero_copy(e).wait()

    def issue(blk, carry):
        for u in range(ROWS_PER_ISSUE):
            r = blk * ROWS_PER_ISSUE + u
            for k in range(2):
                _row_copy(x_ref, r, xs_hbm, dest_ref[0, 2 * r + k], sem).start()
        return carry

    lax.fori_loop(0, TM // ROWS_PER_ISSUE, issue, 0)
    for k in range(2):
        pltpu.make_async_copy(x_ref, xs_hbm.at[pl.ds(0, TM), :], sem).wait()


def _moe_kernel(te_ref, nt_ref, x_ref, w1_ref, w3_ref, w2_ref, o_ref, w1b, w3b, w2b):
    i = pl.program_id(0)
    prev = te_ref[jnp.maximum(i - 1, 0)]

    @pl.when(jnp.logical_or(i == 0, te_ref[i] != prev))
    def _():
        w1b[...] = w1_ref[...].astype(BF)
        w3b[...] = w3_ref[...].astype(BF)
        w2b[...] = w2_ref[...].astype(BF)

    @pl.when(i < nt_ref[0])
    def _():
        x = x_ref[...].astype(BF)
        h1 = jnp.dot(x, w1b[...], preferred_element_type=F32)
        h3 = jnp.dot(x, w3b[...], preferred_element_type=F32)
        hid = (_silu(h1) * h3).astype(BF)
        o_ref[...] = jnp.dot(hid, w2b[...], preferred_element_type=F32)

    @pl.when(i >= nt_ref[0])
    def _():
        o_ref[...] = jnp.zeros_like(o_ref)


def _combine_kernel(dest_ref, out_hbm, h_ref, mod_ref, wt_ref, ho_ref, buf, sem):
    def issue(blk, carry):
        for u in range(ROWS_PER_ISSUE):
            r = blk * ROWS_PER_ISSUE + u
            for k in range(2):
                _row_copy(out_hbm, dest_ref[0, 2 * r + k], buf, k * TM + r, sem).start()
        return carry

    lax.fori_loop(0, TM // ROWS_PER_ISSUE, issue, 0)
    pltpu.make_async_copy(out_hbm.at[pl.ds(0, 2 * TM), :], buf, sem).wait()
    w = wt_ref[...]
    y = buf[0:TM, :] * w[:, 0:1] + buf[TM:2 * TM, :] * w[:, 1:2]
    ho_ref[...] = h_ref[...] + mod_ref[...][5:6] * y


def _moe(h, mod, xn, ids, wts, cnt, w1, w3, w2, layer):
    batch, t_all, _ = xn.shape
    n_tok = batch * t_all
    tiles_b = t_all // TM
    n_tt = n_tok // TM
    n_tiles = (2 * n_tok + MOE_EXPERTS * (MOE_TM - 1) + MOE_TM - 1) // MOE_TM

    cnt = cnt.reshape(n_tt, LANES)[:, :MOE_EXPERTS].astype(jnp.int32)
    tiles_e = (jnp.sum(cnt, axis=0) + MOE_TM - 1) // MOE_TM
    tile_end = jnp.cumsum(tiles_e)
    tile_start = tile_end - tiles_e
    before = jnp.cumsum(cnt, axis=0) - cnt
    base = (tile_start[None, :] * MOE_TM + before).astype(F32)
    base = jnp.concatenate([base, jnp.zeros((n_tt, LANES - MOE_EXPERTS), F32)], axis=1).reshape(n_tt, 1, LANES)
    n_valid = tile_end[-1]
    tidx = jnp.arange(n_tiles, dtype=jnp.int32)
    te = jnp.minimum(jnp.sum((tile_end[None, :] <= tidx[:, None]).astype(jnp.int32), axis=1), MOE_EXPERTS - 1)
    te = jnp.where(tidx < n_valid, te, jnp.max(jnp.where(tiles_e > 0, jnp.arange(MOE_EXPERTS), 0))).astype(jnp.int32)
    assert n_tiles * MOE_TM - 2 * n_tok <= MOE_EXPERTS * MOE_TM
    tail = n_valid + jnp.arange(MOE_EXPERTS, dtype=jnp.int32)
    ztile = jnp.concatenate([jnp.where(tiles_e > 0, (tile_end - 1) * MOE_TM, -1),
                             jnp.where(tail < n_tiles, tail * MOE_TM, -1)]).astype(jnp.int32)

    dest = pl.pallas_call(
        _plan_kernel,
        grid=(n_tt,),
        in_specs=[pl.BlockSpec((TM, LANES), lambda i: (i, 0)),
                  pl.BlockSpec((None, 1, LANES), lambda i: (i, 0, 0))],
        out_specs=pl.BlockSpec((TM, LANES), lambda i: (i, 0)),
        out_shape=jax.ShapeDtypeStruct((n_tok, LANES), jnp.int32),
        compiler_params=_cparams(("parallel",)),
        name="moe_plan",
    )(ids.reshape(n_tok, LANES), base)
    dest = dest[:, :2].reshape(n_tt, 1, 2 * TM)
    dest_spec = pl.BlockSpec((None, 1, 2 * TM), lambda i, *_: (i, 0, 0), memory_space=pltpu.SMEM)

    xs = pl.pallas_call(
        _dispatch_kernel,
        grid_spec=pltpu.PrefetchScalarGridSpec(
            num_scalar_prefetch=1,
            grid=(n_tt,),
            in_specs=[dest_spec, pl.BlockSpec((TM, D), lambda i, zt: (i, 0))],
            out_specs=pl.BlockSpec(memory_space=pl.ANY),
            scratch_shapes=[pltpu.VMEM((MOE_TM, D), F32), pltpu.SemaphoreType.DMA(()), pltpu.SemaphoreType.DMA(())],
        ),
        out_shape=jax.ShapeDtypeStruct((n_tiles * MOE_TM, D), F32),
        compiler_params=_cparams(("arbitrary",)),
        name="moe_dispatch",
    )(ztile, dest, xn.reshape(n_tok, D))

    out = pl.pallas_call(
        _moe_kernel,
        grid_spec=pltpu.PrefetchScalarGridSpec(
            num_scalar_prefetch=2,
            grid=(n_tiles,),
            in_specs=[pl.BlockSpec((MOE_TM, D), lambda i, te, nt: (i, 0)),
                      pl.BlockSpec((None, None, D, MOE_FF), lambda i, te, nt: (layer, te[i], 0, 0)),
                      pl.BlockSpec((None, None, D, MOE_FF), lambda i, te, nt: (layer, te[i], 0, 0)),
                      pl.BlockSpec((None, None, MOE_FF, D), lambda i, te, nt: (layer, te[i], 0, 0))],
            out_specs=pl.BlockSpec((MOE_TM, D), lambda i, te, nt: (i, 0)),
            scratch_shapes=[pltpu.VMEM((D, MOE_FF), BF), pltpu.VMEM((D, MOE_FF), BF), pltpu.VMEM((MOE_FF, D), BF)],
        ),
        out_shape=jax.ShapeDtypeStruct((n_tiles * MOE_TM, D), F32),
        compiler_params=_cparams(("arbitrary",)),
        name="moe_experts",
    )(te, n_valid.reshape(1).astype(jnp.int32), xs, w1, w3, w2)

    h_new = pl.pallas_call(
        _combine_kernel,
        grid=(n_tt,),
        in_specs=[dest_spec, pl.BlockSpec(memory_space=pl.ANY),
                  pl.BlockSpec((TM, D), lambda i: (i, 0)),
                  pl.BlockSpec((None, None, 8, D), lambda i: (i // tiles_b, jnp.minimum(i % tiles_b, 1), 0, 0)),
                  pl.BlockSpec((TM, LANES), lambda i: (i, 0))],
        out_specs=pl.BlockSpec((TM, D), lambda i: (i, 0)),
        out_shape=jax.ShapeDtypeStruct((n_tok, D), F32),
        scratch_shapes=[pltpu.VMEM((2 * TM, D), F32), pltpu.SemaphoreType.DMA(())],
        compiler_params=_cparams(("arbitrary",)),
        name="moe_combine",
    )(dest, out, h.reshape(n_tok, D), mod, wts.reshape(n_tok, LANES))
    return h_new.reshape(batch, t_all, D)


def kernel(x, c, ctx, c_ctx, ada_w, ada_b, norm1_g, norm2_g, da_w_in, da_w_out, da_q_g, da_k_g, da_lam_q1, da_lam_k1, da_lam_q2, da_lam_k2, da_sub_g, gm_w_in, gm_v_g, gm_w_s, gm_b_s, gm_w_out, ssm_w_in, ssm_conv_w, ssm_conv_b, ssm_dt_bias, ssm_a_log, ssm_d, ssm_out_g, ssm_w_out, mla_w_in, mla_q_norm_g, mla_kv_norm_g, mla_w_uq, mla_w_ukv, mla_q_g, mla_k_g, mla_w_out, moe_w_group, moe_b_group, moe_w_router, moe_b_router, moe_w1, moe_w3, moe_w2):
    batch, seq, _ = x.shape
    assert ctx.shape[1] == CTX and seq % TM == 0 and seq % GRID_W == 0
    h = jnp.concatenate([ctx, x], axis=1)

    rows = ((batch + 1 + 7) // 8) * 8
    cc = jnp.concatenate([c, c_ctx[None, :], jnp.zeros((rows - batch - 1, D), F32)], axis=0)
    ada = _ada_all(cc, ada_w, ada_b)
    mod_l = ada[:, :batch].reshape(DEPTH, batch, 6, D)
    mod_c = jnp.broadcast_to(ada[:, batch].reshape(DEPTH, 1, 6, D), (DEPTH, batch, 6, D))
    mod_all = jnp.stack([mod_c, mod_l], axis=2)
    mod_all = jnp.concatenate([mod_all, jnp.zeros((DEPTH, batch, 2, 2, D), F32)], axis=3)

    pad = LANES - MOE_EXPERTS - MOE_GROUPS
    for i in range(DEPTH):
        kind = i % 4
        mod = mod_all[i]
        n1g = norm1_g[i].reshape(1, D)
        n2g = norm2_g[i].reshape(1, D)
        wr = jnp.concatenate([moe_w_router[i], moe_w_group[i], jnp.zeros((D, pad), F32)], axis=1).astype(BF)
        br = jnp.concatenate([moe_b_router[i], moe_b_group[i], jnp.zeros((pad,), F32)]).reshape(1, LANES)
        if kind == 0:
            a = _diff_attention(h, mod, n1g, da_w_in[0], da_q_g[0], da_k_g[0], da_lam_q1[0], da_lam_k1[0],
                                da_lam_q2[0], da_lam_k2[0], da_sub_g[0], i)
            h, xn, ids, wts, cnt =_outproj(a, da_w_out[0].astype(BF), h, mod, n2g, wr, br)
        elif kind == 1:
            h, xn, ids, wts, cnt =_gmlp_layer(h, mod, n1g, gm_w_in[0], gm_v_g[0], gm_w_s[0], gm_b_s[0], gm_w_out[0],
                                          n2g, wr, br)
        elif kind == 2:
            h, xn, ids, wts, cnt =_mamba_layer(h, mod, n1g, ssm_w_in[0], ssm_conv_w[0], ssm_conv_b[0], ssm_dt_bias[0],
                                           ssm_a_log[0], ssm_d[0], ssm_out_g[0], ssm_w_out[0], n2g, wr, br)
        else:
            a = _mla_attention(h, mod, n1g, mla_w_in[0], mla_q_norm_g[0], mla_kv_norm_g[0], mla_w_uq[0],
                               mla_w_ukv[0], mla_q_g[0], mla_k_g[0])
            h, xn, ids, wts, cnt =_outproj(a, mla_w_out[0].astype(BF), h, mod, n2g, wr, br)
        h = _moe(h, mod, xn, ids, wts, cnt, moe_w1, moe_w3, moe_w2, i)
    return h[:, CTX:, :]
```

```python
import functools
import math

import numpy as np
import jax
import jax.numpy as jnp
from jax import lax
from jax.experimental import pallas as pl
from jax.experimental.pallas import tpu as pltpu

F32 = jnp.float32
BF = jnp.bfloat16
HIGHEST = lax.Precision.HIGHEST

D = 1024
CTX = 256
GRID_W = 64
EPS = 1e-6
ROPE_BASE = 10000.0
DEPTH = 4
LANES = 128
TM = 256

DA_HEADS = 8
DA_HEAD_DIM = 64

GM_CHUNK = 128
GM_GROUPS = 8

SSM_INNER = 2048
SSM_HEADS = 32
SSM_GROUPS = 4
SSM_STATE = 128
SSM_CONV = 5
SSM_CHUNK = 128
SSM_CONV_DIM = SSM_INNER + 2 * SSM_GROUPS * SSM_STATE

MLA_HEADS = 16
MLA_Q_RANK = 384
MLA_KV_RANK = 256
MLA_NOPE = 64
MLA_ROPE = 32
MLA_V = 64
MLA_QK = MLA_NOPE + MLA_ROPE

MOE_GROUPS = 4
MOE_PER_GROUP = 8
MOE_EXPERTS = 32
MOE_FF = 512
MOE_TM = 256

VMEM_LIMIT = 56 * 1024 * 1024


def _cparams(sem):
    return pltpu.CompilerParams(dimension_semantics=sem, vmem_limit_bytes=VMEM_LIMIT)


def _full_spec(arr):
    nd = arr.ndim
    return pl.BlockSpec(arr.shape, lambda *_: (0,) * nd)


def _tok_spec(width, col=0):
    return pl.BlockSpec((None, TM, width), lambda b, t: (b, t, col))


def _mod_spec():
    return pl.BlockSpec((None, None, 8, D), lambda b, t: (b, jnp.minimum(t, 1), 0, 0))


def _silu(x):
    return x * (1.0 / (1.0 + jnp.exp(-x)))


def _norm_mod(h, g, scale, shift):
    ms = jnp.mean(h * h, axis=-1, keepdims=True)
    y = h * lax.rsqrt(ms + EPS) * g
    return y * (1.0 + scale) + shift


def _ada_kernel(c_ref, w_ref, b_ref, o_ref):
    a = _silu(c_ref[...]).astype(BF)
    o_ref[...] = jnp.dot(a, w_ref[...].astype(BF), preferred_element_type=F32) + b_ref[...]


def _ada_all(cc, ada_w, ada_b):
    rows = cc.shape[0]
    tn = 1536
    return pl.pallas_call(
        _ada_kernel,
        grid=(DEPTH, 6 * D // tn),
        in_specs=[pl.BlockSpec((rows, D), lambda l, j: (0, 0)),
                  pl.BlockSpec((None, D, tn), lambda l, j: (l, 0, j)),
                  pl.BlockSpec((None, 1, tn), lambda l, j: (l, 0, j))],
        out_specs=pl.BlockSpec((None, rows, tn), lambda l, j: (l, 0, j)),
        out_shape=jax.ShapeDtypeStruct((DEPTH, rows, 6 * D), F32),
        compiler_params=_cparams(("arbitrary", "arbitrary")),
        name="ada_mod",
    )(cc, ada_w, ada_b.reshape(DEPTH, 1, 6 * D))


def _residual_and_route(delta, h_ref, mod, n2g_ref, wr_ref, br_ref, outs):
    ho_ref, xo_ref, id_ref, wt_ref, cnt_ref = outs
    hn = h_ref[...] + mod[2:3] * delta
    ho_ref[...] = hn
    x2 = _norm_mod(hn, n2g_ref[...], mod[4:5], mod[3:4])
    xo_ref[...] = x2
    lg = jnp.dot(x2.astype(BF), wr_ref[...], preferred_element_type=F32) + br_ref[...]
    lane = lax.broadcasted_iota(jnp.int32, lg.shape, 1)
    neg = jnp.float32(-jnp.inf)
    big = jnp.int32(1 << 20)
    is_g = jnp.logical_and(lane >= MOE_EXPERTS, lane < MOE_EXPERTS + MOE_GROUPS)
    gl = jnp.where(is_g, lg, neg)
    gmax = jnp.max(gl, axis=-1, keepdims=True)
    gidx = jnp.min(jnp.where(gl == gmax, lane, big), axis=-1, keepdims=True) - MOE_EXPERTS
    pg = 1.0 / jnp.sum(jnp.where(is_g, jnp.exp(gl - gmax), 0.0), axis=-1, keepdims=True)
    lo = gidx * MOE_PER_GROUP
    in_grp = jnp.logical_and(lane >= lo, lane < lo + MOE_PER_GROUP)
    el = jnp.where(in_grp, lg, neg)
    e1 = jnp.max(el, axis=-1, keepdims=True)
    i1 = jnp.min(jnp.where(el == e1, lane, big), axis=-1, keepdims=True)
    el2 = jnp.where(lane == i1, neg, el)
    e2 = jnp.max(el2, axis=-1, keepdims=True)
    i2 = jnp.min(jnp.where(el2 == e2, lane, big), axis=-1, keepdims=True)
    t = jnp.exp(e2 - e1)
    w1 = pg / (1.0 + t)
    w2 = pg * t / (1.0 + t)
    id_ref[...] = jnp.where(lane == 0, i1, jnp.where(lane == 1, i2, 0))
    wt_ref[...] = jnp.where(lane == 0, w1, jnp.where(lane == 1, w2, 0.0))
    chosen = jnp.where(jnp.logical_or(lane == i1, lane == i2), 1.0, 0.0)
    cnt_ref[...] = jnp.sum(chosen, axis=0, keepdims=True)


def _tail_in_specs(n2g, wr, br):
    return [_full_spec(n2g), _full_spec(wr), _full_spec(br)]


def _tail_out(batch, t_all):
    specs = [_tok_spec(D), _tok_spec(D), _tok_spec(LANES), _tok_spec(LANES),
             pl.BlockSpec((None, None, 1, LANES), lambda b, t: (b, t, 0, 0))]
    shapes = [jax.ShapeDtypeStruct((batch, t_all, D), F32),
              jax.ShapeDtypeStruct((batch, t_all, D), F32),
              jax.ShapeDtypeStruct((batch, t_all, LANES), jnp.int32),
              jax.ShapeDtypeStruct((batch, t_all, LANES), F32),
              jax.ShapeDtypeStruct((batch, t_all // TM, 1, LANES), F32)]
    return specs, shapes


def _outproj_kernel(a_ref, w_ref, h_ref, mod_ref, n2g_ref, wr_ref, br_ref, *outs):
    delta = jnp.dot(a_ref[...], w_ref[...], preferred_element_type=F32)
    _residual_and_route(delta, h_ref, mod_ref[...], n2g_ref, wr_ref, br_ref, outs)


def _outproj(a, w, h, mod, n2g, wr, br):
    batch, t_all, k = a.shape
    out_specs, out_shapes = _tail_out(batch, t_all)
    return pl.pallas_call(
        _outproj_kernel,
        grid=(batch, t_all // TM),
        in_specs=[_tok_spec(k), _full_spec(w), _tok_spec(D), _mod_spec()] + _tail_in_specs(n2g, wr, br),
        out_specs=out_specs,
        out_shape=out_shapes,
        compiler_params=_cparams(("parallel", "parallel")),
        name="outproj_route",
    )(a, w, h, mod, n2g, wr, br)


def _rope_tables(t_lat, rot_dim):
    rows = t_lat // GRID_W
    row = np.repeat(np.arange(rows, dtype=np.float32), GRID_W)
    col = np.tile(np.arange(GRID_W, dtype=np.float32), rows)
    n_freq = rot_dim // 4
    inv_freq = jnp.asarray(ROPE_BASE, F32) ** (-jnp.arange(n_freq, dtype=F32) / n_freq)
    ang = jnp.concatenate([jnp.asarray(row)[:, None] * inv_freq, jnp.asarray(col)[:, None] * inv_freq], axis=-1)
    return jnp.cos(ang), jnp.sin(ang)


def _with_ctx_identity(cos_l, sin_l):
    cos = jnp.concatenate([jnp.ones((CTX, LANES), F32), cos_l], axis=0)
    sin = jnp.concatenate([jnp.zeros((CTX, LANES), F32), sin_l], axis=0)
    return cos, sin


def _da_perm():
    perm = np.zeros(2 * DA_HEADS * DA_HEAD_DIM, np.int32)
    for h in range(DA_HEADS):
        for m in range(2):
            for i in range(DA_HEAD_DIM):
                new = h * LANES + (i // 32) * 64 + m * 32 + (i % 32)
                perm[new] = (2 * h + m) * DA_HEAD_DIM + i
    return perm


def _da_lane_of_dim():
    lane = np.arange(LANES)
    return (lane // 64) * 32 + lane % 32


def _da_proj_kernel(h_ref, mod_ref, n1g_ref, w_ref, gq_ref, gk_ref, cos_ref, sin_ref, gavg_ref,
                    q_ref, k_ref, v_ref):
    mod = mod_ref[...]
    xn = _norm_mod(h_ref[...], n1g_ref[...], mod[1:2], mod[0:1]).astype(BF)
    cos = cos_ref[...]
    sin = sin_ref[...]
    gavg = gavg_ref[...]
    nq = DA_HEADS * LANES
    scale = DA_HEAD_DIM ** -0.5 * LOG2E
    for dst, off, g_ref, mult in ((q_ref, 0, gq_ref, scale), (k_ref, nq, gk_ref, 1.0)):
        y_all = jnp.dot(xn, w_ref[:, off:off + nq], preferred_element_type=F32)
        for jj in range(DA_HEADS // 2):
            yy = y_all[:, jj * 2 * LANES:(jj + 1) * 2 * LANES]
            ms2 = jnp.dot((yy * yy).astype(BF), gavg, preferred_element_type=F32)
            for j in (2 * jj, 2 * jj + 1):
                y = y_all[:, j * LANES:(j + 1) * LANES]
                ms = ms2[:, (j % 2) * LANES:(j % 2 + 1) * LANES]
                y = y * lax.rsqrt(ms + EPS) * g_ref[...]
                y = y * cos + pltpu.roll(y, 64, 1) * sin
                dst[:, j * LANES:(j + 1) * LANES] = (y * mult).astype(BF)
    v_ref[...] = jnp.dot(xn, w_ref[:, 2 * nq:3 * nq], preferred_element_type=F32).astype(BF)


LOG2E = math.log2(math.e)


def _softmax_pv(q, k, v):
    s = lax.dot_general(q, k, (((1,), (1,)), ((), ())), preferred_element_type=F32)
    e = jnp.exp2(s - jnp.max(s, axis=-1, keepdims=True))
    l = jnp.sum(e, axis=-1, keepdims=True)
    return jnp.dot(e.astype(BF), v, preferred_element_type=F32) / l


def _da_attn_kernel(lam_ref, q_ref, k_ref, v_ref, subg_ref, o_ref, *, lam_init):
    t_all = q_ref.shape[0]
    lane = lax.broadcasted_iota(jnp.int32, (TM, LANES), 1)
    comp0 = ((lane // 32) % 2) == 0
    zero = jnp.zeros((TM, LANES), BF)

    def tile(row0, n_keys):
        q = q_ref[pl.ds(row0, TM), :]
        k = k_ref[0:n_keys, :]
        v = v_ref[0:n_keys, :]
        o = _softmax_pv(jnp.where(comp0, q, zero), k, v) - lam_ref[0] * _softmax_pv(jnp.where(comp0, zero, q), k, v)
        ms = jnp.mean(o * o, axis=-1, keepdims=True)
        o = o * lax.rsqrt(ms + EPS) * subg_ref[...] * (1.0 - lam_init)
        o_ref[pl.ds(row0, TM), :] = o.astype(BF)

    tile(0, CTX)

    def latent_tile(t, carry):
        tile(pl.multiple_of(t * TM, TM), t_all)
        return carry

    lax.fori_loop(CTX // TM, t_all // TM, latent_tile, 0)


def _diff_attention(h, mod, n1g, w_in, q_g, k_g, lam_q1, lam_k1, lam_q2, lam_k2, sub_g, layer_idx):
    batch, t_all, _ = h.shape
    nq = DA_HEADS * LANES
    perm = _da_perm()
    w = jnp.concatenate([w_in[:, :nq][:, perm], w_in[:, nq:2 * nq][:, perm], w_in[:, 2 * nq:]], axis=1).astype(BF)
    ldim = _da_lane_of_dim()
    gq = q_g[ldim].reshape(1, LANES)
    gk = k_g[ldim].reshape(1, LANES)
    cos, sin = _rope_tables(t_all - CTX, DA_HEAD_DIM)
    cos, sin = _with_ctx_identity(jnp.tile(cos, (1, 4)), jnp.concatenate([-sin, -sin, sin, sin], axis=1))
    unit = np.arange(2 * LANES) // LANES * 2 + (np.arange(2 * LANES) // 32) % 2
    gavg = jnp.asarray((unit[:, None] == unit[None, :]).astype(np.float32) / DA_HEAD_DIM).astype(BF)
    tab_spec = pl.BlockSpec((TM, LANES), lambda b, t: (t, 0))
    qkv_shape = jax.ShapeDtypeStruct((batch, t_all, nq), BF)
    q, k, v = pl.pallas_call(
        _da_proj_kernel,
        grid=(batch, t_all // TM),
        in_specs=[_tok_spec(D), _mod_spec(), _full_spec(n1g), _full_spec(w), _full_spec(gq), _full_spec(gk),
                  tab_spec, tab_spec, _full_spec(gavg)],
        out_specs=[_tok_spec(nq)] * 3,
        out_shape=[qkv_shape] * 3,
        compiler_params=_cparams(("parallel", "parallel")),
        name="da_proj",
    )(h, mod, n1g, w, gq, gk, cos, sin, gavg)

    lam_init = 0.8 - 0.6 * math.exp(-0.3 * layer_idx)
    lam = (jnp.exp(jnp.sum(lam_q1 * lam_k1)) - jnp.exp(jnp.sum(lam_q2 * lam_k2)) + lam_init).reshape(1)
    seq_spec = pl.BlockSpec((None, t_all, LANES), lambda b, hh: (b, 0, hh))
    subg = sub_g.reshape(1, LANES)
    return pl.pallas_call(
        functools.partial(_da_attn_kernel, lam_init=lam_init),
        grid=(batch, DA_HEADS),
        in_specs=[pl.BlockSpec(memory_space=pltpu.SMEM), seq_spec, seq_spec, seq_spec,
                  pl.BlockSpec((1, LANES), lambda b, hh: (0, 0))],
        out_specs=seq_spec,
        out_shape=jax.ShapeDtypeStruct((batch, t_all, nq), BF),
        compiler_params=_cparams(("parallel", "parallel")),
        name="da_attn",
    )(lam, q, k, v, subg)


def _gmlp_kernel(h_ref, mod_ref, n1g_ref, win_ref, vg_ref, ws_ref, bs_ref, wout_ref,
                 n2g_ref, wr_ref, br_ref, *outs):
    mod = mod_ref[...]
    xn = _norm_mod(h_ref[...], n1g_ref[...], mod[1:2], mod[0:1]).astype(BF)
    uv =jnp.dot(xn, win_ref[...], preferred_element_type=F32)
    uv = 0.5 * uv * (1.0 + lax.erf(uv * (2.0 ** -0.5)))
    u = uv[:, :D]
    v = uv[:, D:]
    v = (v * lax.rsqrt(jnp.mean(v * v, axis=-1, keepdims=True) + EPS) * vg_ref[...]).astype(BF)
    rows = []
    for c in range(TM // GM_CHUNK):
        cols = []
        for g in range(GM_GROUPS):
            vb = v[c * GM_CHUNK:(c + 1) * GM_CHUNK, g * LANES:(g + 1) * LANES]
            s = jnp.dot(ws_ref[g], vb, preferred_element_type=F32) + bs_ref[g]
            cols.append(s)
        rows.append(jnp.concatenate(cols, axis=1))
    s_all = jnp.concatenate(rows, axis=0)
    delta = jnp.dot((u * s_all).astype(BF), wout_ref[...], preferred_element_type=F32)
    _residual_and_route(delta, h_ref, mod, n2g_ref, wr_ref, br_ref, outs)


def _gmlp_layer(h, mod, n1g, w_in, v_g, w_s, b_s, w_out, n2g, wr, br):
    batch, t_all, _ = h.shape
    win = w_in.astype(BF)
    wout = w_out.astype(BF)
    ws = w_s.astype(BF)
    bs = jnp.broadcast_to(b_s[:, :, None], (GM_GROUPS, GM_CHUNK, LANES)).astype(F32)
    vg = v_g.reshape(1, D)
    out_specs, out_shapes = _tail_out(batch, t_all)
    return pl.pallas_call(
        _gmlp_kernel,
        grid=(batch, t_all // TM),
        in_specs=[_tok_spec(D), _mod_spec(), _full_spec(n1g), _full_spec(win), _full_spec(vg), _full_spec(ws),
                  _full_spec(bs), _full_spec(wout)] + _tail_in_specs(n2g, wr, br),
        out_specs=out_specs,
        out_shape=out_shapes,
        compiler_params=_cparams(("parallel", "parallel")),
        name="gmlp_layer",
    )(h, mod, n1g, win, vg, ws, bs, wout, n2g, wr, br)


SSM_DT_PAD = LANES


def _ssm_proj_kernel(h_ref, mod_ref, n1g_ref, w_ref, dtb_ref, z_ref, xbc_ref, dt_ref):
    mod = mod_ref[...]
    xn = _norm_mod(h_ref[...], n1g_ref[...], mod[1:2], mod[0:1]).astype(BF)
    z_ref[...] = jnp.dot(xn, w_ref[:, :SSM_INNER], preferred_element_type=F32).astype(BF)
    c0 = SSM_INNER
    for j in range(SSM_CONV_DIM // 1024):
        xbc_ref[:, j * 1024:(j + 1) * 1024] = jnp.dot(
            xn, w_ref[:, c0 + j * 1024:c0 + (j + 1) * 1024], preferred_element_type=F32).astype(BF)
    c1 = SSM_INNER + SSM_CONV_DIM
    raw = jnp.dot(xn, w_ref[:, c1:c1 + SSM_DT_PAD], preferred_element_type=F32) + dtb_ref[...]
    dt_ref[...] = jnp.maximum(raw, 0.0) + jnp.log1p(jnp.exp(-jnp.abs(raw)))


CONV_PAD = 8
CONV_W = 512


def _conv_kernel(x_ref, w_ref, b_ref, o_ref, pad_ref):
    t_all = x_ref.shape[0]
    t_lat = t_all - CTX
    lat0 = CTX + 2 * CONV_PAD
    zeros = jnp.zeros((CONV_PAD, CONV_W), F32)
    pad_ref[0:CONV_PAD, :] = zeros
    pad_ref[CONV_PAD + CTX:lat0, :] = jnp.zeros((CONV_PAD, CONV_W), F32)
    pad_ref[lat0 + t_lat:lat0 + t_lat + CONV_PAD, :] = zeros
    pad_ref[CONV_PAD:CONV_PAD + CTX, :] = x_ref[0:CTX, :].astype(F32)
    pad_ref[lat0:lat0 + t_lat, :] = x_ref[CTX:t_all, :].astype(F32)
    half = (SSM_CONV - 1) // 2
    for start, n, dst in ((CONV_PAD, CTX, 0), (lat0, t_lat, CTX)):
        acc = jnp.zeros((n, CONV_W), F32) + b_ref[...]
        for kk in range(SSM_CONV):
            acc = acc + pad_ref[start + kk - half:start + kk - half + n, :] * w_ref[kk:kk + 1, :]
        o_ref[dst:dst + n, :] = _silu(acc).astype(BF)


def _ssd_kernel(dt_ref, a_ref, x_ref, b_ref, c_ref, y_ref, s_ref):
    direction = pl.program_id(1)
    step = pl.program_id(2)

    @pl.when(step == 0)
    def _():
        s_ref[...] = jnp.zeros_like(s_ref)

    q = SSM_CHUNK
    row = lax.broadcasted_iota(jnp.int32, (q, q), 0)
    col = lax.broadcasted_iota(jnp.int32, (q, q), 1)
    sign = 1 - 2 * direction
    tri_b = (col - row) * sign <= 0
    tri = tri_b.astype(F32)
    tri_t = ((row - col) * sign <= 0).astype(F32)
    eye = (row == col).astype(F32)
    lane_lo = col < 64

    dt_t = dt_ref[...]
    a_t = dt_t * a_ref[...]
    cs_rows = jnp.dot(a_t, tri_t, preferred_element_type=F32, precision=HIGHEST)
    tot = jnp.sum(a_t, axis=1, keepdims=True)
    heads_per_group = SSM_HEADS // SSM_GROUPS

    for g in range(SSM_GROUPS):
        bm = b_ref[:, g * SSM_STATE:(g + 1) * SSM_STATE]
        cm = c_ref[:, g * SSM_STATE:(g + 1) * SSM_STATE]
        cb = lax.dot_general(cm, bm, (((1,), (1,)), ((), ())), preferred_element_type=F32)
        bm_t = bm.astype(F32).T.astype(BF)
        for pp in range(heads_per_group // 2):
            p = g * (heads_per_group // 2) + pp
            xp = x_ref[:, p * LANES:(p + 1) * LANES].astype(F32)
            cs_col, dt_col, mats = [], [], []
            for hh in range(2):
                e = 2 * p + hh
                cc = jnp.sum(tri * a_t[e:e + 1, :], axis=1, keepdims=True)
                cs_col.append(cc)
                dt_col.append(jnp.sum(eye * dt_t[e:e + 1, :], axis=1, keepdims=True))
                decay = jnp.exp(jnp.where(tri_b, cc - cs_rows[e:e + 1, :], -jnp.inf))
                mats.append((cb * decay).astype(BF))
            xdt = xp * jnp.where(lane_lo, dt_col[0], dt_col[1])
            xb = xdt.astype(BF)
            y_diag = jnp.where(lane_lo,
                               jnp.dot(mats[0], xb, preferred_element_type=F32),
                               jnp.dot(mats[1], xb, preferred_element_type=F32))
            s_prev = s_ref[p]
            y_off = jnp.dot(cm, s_prev.astype(BF), preferred_element_type=F32)
            y_off = y_off * jnp.exp(jnp.where(lane_lo, cs_col[0], cs_col[1]))
            y_ref[:, p * LANES:(p + 1) * LANES] = y_diag + y_off
            t0 = tot[2 * p:2 * p + 1, :]
            t1 = tot[2 * p + 1:2 * p + 2, :]
            to_end = jnp.exp(jnp.where(lane_lo, t0 - cs_col[0], t1 - cs_col[1]))
            upd = jnp.dot(bm_t, (xdt * to_end).astype(BF), preferred_element_type=F32)
            s_ref[p] = s_prev * jnp.exp(jnp.where(lane_lo[0:1, :], t0, t1)) + upd


def _ssm_finish_kernel(yf_ref, yb_ref, xs_ref, z_ref, d_ref, og_ref, w_ref, h_ref, mod_ref,
                       n2g_ref, wr_ref, br_ref, *outs):
    y = yf_ref[...] + yb_ref[...] + d_ref[...] * xs_ref[...].astype(F32)
    y = y * _silu(z_ref[...].astype(F32))
    y = y * lax.rsqrt(jnp.mean(y * y, axis=-1, keepdims=True) + EPS) * og_ref[...]
    delta = jnp.dot(y.astype(BF), w_ref[...], preferred_element_type=F32)
    _residual_and_route(delta, h_ref, mod_ref[...], n2g_ref, wr_ref, br_ref, outs)


def _mamba_layer(h, mod, n1g, w_in, conv_w, conv_b, dt_bias, a_log, d_skip, out_g, w_out, n2g, wr, br):
    batch, t_all, _ = h.shape
    n_dt = 2 * SSM_HEADS
    c1 = SSM_INNER + SSM_CONV_DIM
    w = jnp.concatenate([w_in, jnp.zeros((D, SSM_DT_PAD - n_dt), F32)], axis=1).astype(BF)
    dtb = jnp.concatenate([dt_bias.reshape(n_dt), jnp.zeros((SSM_DT_PAD - n_dt,), F32)]).reshape(1, SSM_DT_PAD)
    z, xbc, dt = pl.pallas_call(
        _ssm_proj_kernel,
        grid=(batch, t_all // TM),
        in_specs=[_tok_spec(D), _mod_spec(), _full_spec(n1g), _full_spec(w), _full_spec(dtb)],
        out_specs=[_tok_spec(SSM_INNER), _tok_spec(SSM_CONV_DIM), _tok_spec(SSM_DT_PAD)],
        out_shape=[jax.ShapeDtypeStruct((batch, t_all, SSM_INNER), BF),
                   jax.ShapeDtypeStruct((batch, t_all, SSM_CONV_DIM), BF),
                   jax.ShapeDtypeStruct((batch, t_all, SSM_DT_PAD), F32)],
        compiler_params=_cparams(("parallel", "parallel")),
        name="ssm_proj",
    )(h, mod, n1g, w, dtb)
    assert c1 + n_dt == w_in.shape[1]

    cw = jnp.concatenate([conv_w, jnp.zeros((8 - SSM_CONV, SSM_CONV_DIM), F32)], axis=0)
    cbias = conv_b.reshape(1, SSM_CONV_DIM)
    xbc = pl.pallas_call(
        _conv_kernel,
        grid=(batch, SSM_CONV_DIM // CONV_W),
        in_specs=[pl.BlockSpec((None, t_all, CONV_W), lambda b, j: (b, 0, j)),
                  pl.BlockSpec((8, CONV_W), lambda b, j: (0, j)),
                  pl.BlockSpec((1, CONV_W), lambda b, j: (0, j))],
        out_specs=pl.BlockSpec((None, t_all, CONV_W), lambda b, j: (b, 0, j)),
        out_shape=jax.ShapeDtypeStruct((batch, t_all, SSM_CONV_DIM), BF),
        scratch_shapes=[pltpu.VMEM((t_all + 3 * CONV_PAD, CONV_W), F32)],
        compiler_params=_cparams(("parallel", "parallel")),
        name="ssm_conv",
    )(xbc, cw, cbias)

    n_chunks = t_all // SSM_CHUNK
    ctx_chunks = CTX // SSM_CHUNK
    dt_t = jnp.swapaxes(dt[:, :, :n_dt], 1, 2)
    a_neg = jnp.broadcast_to(-jnp.exp(a_log.astype(F32)).reshape(2, SSM_HEADS, 1), (2, SSM_HEADS, SSM_CHUNK))

    def chunk_of(dr, s):
        rev = jnp.where(s < ctx_chunks, ctx_chunks - 1 - s, n_chunks - 1 + ctx_chunks - s)
        return jnp.where(dr == 0, s, rev)

    gs = SSM_GROUPS * SSM_STATE
    y = pl.pallas_call(
        _ssd_kernel,
        grid=(batch, 2, n_chunks),
        in_specs=[pl.BlockSpec((None, SSM_HEADS, SSM_CHUNK), lambda b, dr, s: (b, dr, chunk_of(dr, s))),
                  pl.BlockSpec((None, SSM_HEADS, SSM_CHUNK), lambda b, dr, s: (dr, 0, 0)),
                  pl.BlockSpec((None, SSM_CHUNK, SSM_INNER), lambda b, dr, s: (b, chunk_of(dr, s), 0)),
                  pl.BlockSpec((None, SSM_CHUNK, gs), lambda b, dr, s: (b, chunk_of(dr, s), SSM_INNER // gs)),
                  pl.BlockSpec((None, SSM_CHUNK, gs), lambda b, dr, s: (b, chunk_of(dr, s), SSM_INNER // gs + 1))],
        out_specs=pl.BlockSpec((None, None, SSM_CHUNK, SSM_INNER), lambda b, dr, s: (dr, b, chunk_of(dr, s), 0)),
        out_shape=jax.ShapeDtypeStruct((2, batch, t_all, SSM_INNER), F32),
        scratch_shapes=[pltpu.VMEM((SSM_HEADS // 2, SSM_STATE, LANES), F32)],
        compiler_params=_cparams(("parallel", "parallel", "arbitrary")),
        name="ssd_scan",
    )(dt_t, a_neg, xbc, xbc, xbc)

    dvec = jnp.repeat(d_skip, SSM_INNER // SSM_HEADS).reshape(1, SSM_INNER)
    og = out_g.reshape(1, SSM_INNER)
    wo = w_out.astype(BF)
    out_specs, out_shapes = _tail_out(batch, t_all)
    y_spec = lambda dr: pl.BlockSpec((None, None, TM, SSM_INNER), lambda b, t: (dr, b, t, 0))
    return pl.pallas_call(
        _ssm_finish_kernel,
        grid=(batch, t_all // TM),
        in_specs=[y_spec(0), y_spec(1), _tok_spec(SSM_INNER), _tok_spec(SSM_INNER), _full_spec(dvec),
                  _full_spec(og), _full_spec(wo), _tok_spec(D), _mod_spec()] + _tail_in_specs(n2g, wr, br),
        out_specs=out_specs,
        out_shape=out_shapes,
        compiler_params=_cparams(("parallel", "parallel")),
        name="ssm_finish",
    )(y, y, xbc, z, dvec, og, wo, h, mod, n2g, wr, br)


def _mla_lane_src():
    src = np.full(LANES, -1, np.int32)
    src[0:16] = MLA_NOPE + np.arange(16)
    src[16:64] = np.arange(48)
    src[64:80] = MLA_NOPE + 16 + np.arange(16)
    src[80:96] = 48 + np.arange(16)
    return src


def _mla_proj_kernel(h_ref, mod_ref, n1g_ref, win_ref, qng_ref, kvng_ref, wuq_ref, wuk_ref, wuv_ref,
                     gq_ref, gk_ref, cos_ref, sin_ref, avg_ref, q_ref, k_ref, v_ref):
    mod = mod_ref[...]
    xn = _norm_mod(h_ref[...], n1g_ref[...], mod[1:2], mod[0:1]).astype(BF)
    lat = jnp.dot(xn, win_ref[...], preferred_element_type=F32)
    cq = lat[:, :MLA_Q_RANK]
    ckv = lat[:, MLA_Q_RANK:MLA_Q_RANK + MLA_KV_RANK]
    kpe = lat[:, MLA_Q_RANK + MLA_KV_RANK:]
    cq = (cq * lax.rsqrt(jnp.mean(cq * cq, axis=-1, keepdims=True) + EPS) * qng_ref[...]).astype(BF)
    ckv = (ckv * lax.rsqrt(jnp.mean(ckv * ckv, axis=-1, keepdims=True) + EPS) * kvng_ref[...]).astype(BF)
    cos = cos_ref[...]
    sin = sin_ref[...]
    avg = avg_ref[...]
    scale = MLA_QK ** -0.5 * LOG2E
    qa = jnp.dot(cq, wuq_ref[...], preferred_element_type=F32)
    ka = jnp.dot(ckv, wuk_ref[...], preferred_element_type=F32)
    kpe2 = jnp.concatenate([kpe, kpe], axis=1)
    for jj in range(MLA_HEADS // 2):
        for dst, src, add, g_ref, mult in ((q_ref, qa, None, gq_ref, scale), (k_ref, ka, kpe2, gk_ref, 1.0)):
            yy = src[:, jj * 2 * LANES:(jj + 1) * 2 * LANES]
            if add is not None:
                yy = yy + add
            ms2 = jnp.dot((yy * yy).astype(BF), avg, preferred_element_type=F32) * (1.0 / MLA_QK)
            for hh in range(2):
                j = 2 * jj + hh
                y = yy[:, hh * LANES:(hh + 1) * LANES]
                y = y * lax.rsqrt(ms2[:, hh * LANES:(hh + 1) * LANES] + EPS) * g_ref[...]
                y = y * cos + pltpu.roll(y, 64, 1) * sin
                dst[:, j * LANES:(j + 1) * LANES] = (y * mult).astype(BF)
    v_ref[...] = jnp.dot(ckv, wuv_ref[...], preferred_element_type=F32).astype(BF)


def _mla_attn_kernel(q_ref, k_ref, v_ref, o_ref):
    t_all = q_ref.shape[0]
    lane = lax.broadcasted_iota(jnp.int32, (TM, LANES), 1)

    def tile(row0, n_keys):
        v = v_ref[0:n_keys, :]
        outs = [_softmax_pv(q_ref[pl.ds(row0, TM), hh * LANES:(hh + 1) * LANES],
                            k_ref[0:n_keys, hh * LANES:(hh + 1) * LANES], v)
                for hh in range(2)]
        o_ref[pl.ds(row0, TM), :] = jnp.where(lane < MLA_V, outs[0], outs[1]).astype(BF)

    tile(0, CTX)

    def latent_tile(t, carry):
        tile(pl.multiple_of(t * TM, TM), t_all)
        return carry

    lax.fori_loop(CTX // TM, t_all // TM, latent_tile, 0)


def _mla_attention(h, mod, n1g, w_in, q_norm_g, kv_norm_g, w_uq, w_ukv, q_g, k_g):
    batch, t_all, _ = h.shape
    src = _mla_lane_src()
    used = src >= 0
    srcc = np.where(used, src, 0)
    nh = MLA_HEADS * LANES
    kpe_cols = np.where(srcc >= MLA_NOPE, MLA_Q_RANK + MLA_KV_RANK + srcc - MLA_NOPE, 0)
    kpe_used = np.logical_and(used, src >= MLA_NOPE)
    w_kpe = jnp.where(jnp.asarray(kpe_used)[None, :], w_in[:, kpe_cols], 0.0)
    win = jnp.concatenate([w_in[:, :MLA_Q_RANK + MLA_KV_RANK], w_kpe], axis=1).astype(BF)
    q_cols = (np.arange(MLA_HEADS)[:, None] * MLA_QK + srcc[None, :]).reshape(-1)
    q_mask = jnp.asarray(np.tile(used, MLA_HEADS))[None, :]
    wuq = jnp.where(q_mask, w_uq[:, q_cols], 0.0).astype(BF)
    nope_used = np.logical_and(used, src < MLA_NOPE)
    k_cols = (np.arange(MLA_HEADS)[:, None] * (MLA_NOPE + MLA_V) + np.where(nope_used, srcc, 0)[None, :]).reshape(-1)
    k_mask = jnp.asarray(np.tile(nope_used, MLA_HEADS))[None, :]
    wuk = jnp.where(k_mask, w_ukv[:, k_cols], 0.0).astype(BF)
    v_cols = (np.arange(MLA_HEADS)[:, None] * (MLA_NOPE + MLA_V) + MLA_NOPE + np.arange(MLA_V)[None, :]).reshape(-1)
    wuv = w_ukv[:, v_cols].astype(BF)
    gq = jnp.where(jnp.asarray(used), q_g[srcc], 0.0).reshape(1, LANES)
    gk = jnp.where(jnp.asarray(used), k_g[srcc], 0.0).reshape(1, LANES)
    cos16, sin16 = _rope_tables(t_all - CTX, MLA_ROPE)
    t_lat = t_all - CTX
    cos_l = jnp.ones((t_lat, LANES), F32).at[:, 0:16].set(cos16).at[:, 64:80].set(cos16)
    sin_l = jnp.zeros((t_lat, LANES), F32).at[:, 0:16].set(-sin16).at[:, 64:80].set(sin16)
    cos, sin = _with_ctx_identity(cos_l, sin_l)
    head_of_lane = np.arange(2 * LANES) // LANES
    avg = jnp.asarray((head_of_lane[:, None] == head_of_lane[None, :]).astype(np.float32)).astype(BF)
    qng = q_norm_g.reshape(1, MLA_Q_RANK)
    kvng = kv_norm_g.reshape(1, MLA_KV_RANK)
    tab_spec = pl.BlockSpec((TM, LANES), lambda b, t: (t, 0))
    q, k, v = pl.pallas_call(
        _mla_proj_kernel,
        grid=(batch, t_all // TM),
        in_specs=[_tok_spec(D), _mod_spec(), _full_spec(n1g), _full_spec(win), _full_spec(qng), _full_spec(kvng),
                  _full_spec(wuq), _full_spec(wuk), _full_spec(wuv), _full_spec(gq), _full_spec(gk),
                  tab_spec, tab_spec, _full_spec(avg)],
        out_specs=[_tok_spec(nh), _tok_spec(nh), _tok_spec(MLA_HEADS * MLA_V)],
        out_shape=[jax.ShapeDtypeStruct((batch, t_all, nh), BF), jax.ShapeDtypeStruct((batch, t_all, nh), BF),
                   jax.ShapeDtypeStruct((batch, t_all, MLA_HEADS * MLA_V), BF)],
        compiler_params=_cparams(("parallel", "parallel")),
        name="mla_proj",
    )(h, mod, n1g, win, qng, kvng, wuq, wuk, wuv, gq, gk, cos, sin, avg)

    return pl.pallas_call(
        _mla_attn_kernel,
        grid=(batch, MLA_HEADS // 2),
        in_specs=[pl.BlockSpec((None, t_all, 2 * LANES), lambda b, p: (b, 0, p)),
                  pl.BlockSpec((None, t_all, 2 * LANES), lambda b, p: (b, 0, p)),
                  pl.BlockSpec((None, t_all, LANES), lambda b, p: (b, 0, p))],
        out_specs=pl.BlockSpec((None, t_all, LANES), lambda b, p: (b, 0, p)),
        out_shape=jax.ShapeDtypeStruct((batch, t_all, MLA_HEADS * MLA_V), BF),
        compiler_params=_cparams(("parallel", "parallel")),
        name="mla_attn",
    )(q, k, v)


def _plan_kernel(ids_ref, base_ref, dest_ref):
    ids = ids_ref[...]
    lane = lax.broadcasted_iota(jnp.int32, ids.shape, 1)
    sel1 = lane == ids[:, 0:1]
    sel2 = lane == ids[:, 1:2]
    chosen = jnp.where(jnp.logical_or(sel1, sel2), 1.0, 0.0).astype(BF)
    r = lax.broadcasted_iota(jnp.int32, (TM, TM), 0)
    c = lax.broadcasted_iota(jnp.int32, (TM, TM), 1)
    earlier = jnp.where(c < r, 1.0, 0.0).astype(BF)
    row = jnp.dot(earlier, chosen, preferred_element_type=F32) + base_ref[...]
    d1 = jnp.sum(jnp.where(sel1, row, 0.0), axis=-1, keepdims=True)
    d2 = jnp.sum(jnp.where(sel2, row, 0.0), axis=-1, keepdims=True)
    dest_ref[...] = jnp.where(lane == 0, d1, jnp.where(lane == 1, d2, 0.0)).astype(jnp.int32)


ROWS_PER_ISSUE = 8


def _row_copy(src, src_row, dst, dst_row, sem):
    return pltpu.make_async_copy(src.at[pl.ds(src_row, 1), :], dst.at[pl.ds(dst_row, 1), :], sem)


def _dispatch_kernel(ztile_ref, dest_ref, x_ref, xs_hbm, zbuf, zsem, sem):
    def zero_copy(e):
        start = pl.multiple_of(ztile_ref[e], MOE_TM)
        return pltpu.make_async_copy(zbuf, xs_hbm.at[pl.ds(start, MOE_TM), :], zsem)

    @pl.when(pl.program_id(0) == 0)
    def _():
        zbuf[...] = jnp.zeros_like(zbuf)
        for e in range(2 * MOE_EXPERTS):
            @pl.when(ztile_ref[e] >= 0)
            def _():
                zero_copy(e).start()
        for e in range(2 * MOE_EXPERTS):
            @pl.when(ztile_ref[e] >= 0)
            def _():
                zero_copy(e).wait()

    def issue(blk, carry):
        for u in range(ROWS_PER_ISSUE):
            r = blk * ROWS_PER_ISSUE + u
            for k in range(2):
                _row_copy(x_ref, r, xs_hbm, dest_ref[0, 2 * r + k], sem).start(priority=k)
        return carry

    lax.fori_loop(0, TM // ROWS_PER_ISSUE, issue, 0)
    for k in range(2):
        pltpu.make_async_copy(x_ref, xs_hbm.at[pl.ds(0, TM), :], sem).wait()


def _moe_kernel(te_ref, nt_ref, x_ref, w1_ref, w3_ref, w2_ref, o_ref, w1b, w3b, w2b):
    i = pl.program_id(0)
    prev = te_ref[jnp.maximum(i - 1, 0)]

    @pl.when(jnp.logical_or(i == 0, te_ref[i] != prev))
    def _():
        w1b[...] = w1_ref[...].astype(BF)
        w3b[...] = w3_ref[...].astype(BF)
        w2b[...] = w2_ref[...].astype(BF)

    @pl.when(i < nt_ref[0])
    def _():
        x = x_ref[...].astype(BF)
        h1 = jnp.dot(x, w1b[...], preferred_element_type=F32)
        h3 = jnp.dot(x, w3b[...], preferred_element_type=F32)
        hid = (_silu(h1) * h3).astype(BF)
        o_ref[...] = jnp.dot(hid, w2b[...], preferred_element_type=F32)

    @pl.when(i >= nt_ref[0])
    def _():
        o_ref[...] = jnp.zeros_like(o_ref)


def _combine_kernel(dest_ref, dnext_ref, out_hbm, h_ref, mod_ref, wt_ref, ho_ref, buf, sem, *, n_steps):
    i = pl.program_id(0)
    slot = i % 2

    def fetch(d_ref, s):
        def issue(blk, carry):
            for u in range(ROWS_PER_ISSUE):
                r = blk * ROWS_PER_ISSUE + u
                for k in range(2):
                    _row_copy(out_hbm, d_ref[0, 2 * r + k], buf.at[s], k * TM + r, sem.at[s]).start(priority=k)
            return carry

        lax.fori_loop(0, TM // ROWS_PER_ISSUE, issue, 0)

    @pl.when(i == 0)
    def _():
        fetch(dest_ref, 0)

    @pl.when(i + 1 < n_steps)
    def _():
        fetch(dnext_ref, 1 - slot)

    pltpu.make_async_copy(out_hbm.at[pl.ds(0, 2 * TM), :], buf.at[slot], sem.at[slot]).wait()
    w = wt_ref[...]
    y = buf[slot, 0:TM, :] * w[:, 0:1] + buf[slot, TM:2 * TM, :] * w[:, 1:2]
    ho_ref[...] = h_ref[...] + mod_ref[...][5:6] * y


def _moe(h, mod, xn, ids, wts, cnt, w1, w3, w2, layer, latent_only):
    batch, t_all, _ = xn.shape
    n_tok = batch * t_all
    tiles_b = t_all // TM
    n_tt = n_tok // TM
    n_tiles = (2 * n_tok + MOE_EXPERTS * (MOE_TM - 1) + MOE_TM - 1) // MOE_TM

    cnt = cnt.reshape(n_tt, LANES)[:, :MOE_EXPERTS].astype(jnp.int32)
    tiles_e = (jnp.sum(cnt, axis=0) + MOE_TM - 1) // MOE_TM
    tile_end = jnp.cumsum(tiles_e)
    tile_start = tile_end - tiles_e
    before = jnp.cumsum(cnt, axis=0) - cnt
    base = (tile_start[None, :] * MOE_TM + before).astype(F32)
    base = jnp.concatenate([base, jnp.zeros((n_tt, LANES - MOE_EXPERTS), F32)], axis=1).reshape(n_tt, 1, LANES)
    n_valid = tile_end[-1]
    tidx = jnp.arange(n_tiles, dtype=jnp.int32)
    te = jnp.minimum(jnp.sum((tile_end[None, :] <= tidx[:, None]).astype(jnp.int32), axis=1), MOE_EXPERTS - 1)
    te = jnp.where(tidx < n_valid, te, jnp.max(jnp.where(tiles_e > 0, jnp.arange(MOE_EXPERTS), 0))).astype(jnp.int32)
    assert n_tiles * MOE_TM - 2 * n_tok <= MOE_EXPERTS * MOE_TM
    tail = n_valid + jnp.arange(MOE_EXPERTS, dtype=jnp.int32)
    ztile = jnp.concatenate([jnp.where(tiles_e > 0, (tile_end - 1) * MOE_TM, -1),
                             jnp.where(tail < n_tiles, tail * MOE_TM, -1)]).astype(jnp.int32)

    dest = pl.pallas_call(
        _plan_kernel,
        grid=(n_tt,),
        in_specs=[pl.BlockSpec((TM, LANES), lambda i: (i, 0)),
                  pl.BlockSpec((None, 1, LANES), lambda i: (i, 0, 0))],
        out_specs=pl.BlockSpec((TM, LANES), lambda i: (i, 0)),
        out_shape=jax.ShapeDtypeStruct((n_tok, LANES), jnp.int32),
        compiler_params=_cparams(("parallel",)),
        name="moe_plan",
    )(ids.reshape(n_tok, LANES), base)
    dest = dest[:, :2].reshape(n_tt, 1, 2 * TM)
    dest_spec = pl.BlockSpec((None, 1, 2 * TM), lambda i, *_: (i, 0, 0), memory_space=pltpu.SMEM)

    xs = pl.pallas_call(
        _dispatch_kernel,
        grid_spec=pltpu.PrefetchScalarGridSpec(
            num_scalar_prefetch=1,
            grid=(n_tt,),
            in_specs=[dest_spec, pl.BlockSpec((TM, D), lambda i, zt: (i, 0))],
            out_specs=pl.BlockSpec(memory_space=pl.ANY),
            scratch_shapes=[pltpu.VMEM((MOE_TM, D), F32), pltpu.SemaphoreType.DMA(()), pltpu.SemaphoreType.DMA(())],
        ),
        out_shape=jax.ShapeDtypeStruct((n_tiles * MOE_TM, D), F32),
        compiler_params=_cparams(("arbitrary",)),
        name="moe_dispatch",
    )(ztile, dest, xn.reshape(n_tok, D))

    out = pl.pallas_call(
        _moe_kernel,
        grid_spec=pltpu.PrefetchScalarGridSpec(
            num_scalar_prefetch=2,
            grid=(n_tiles,),
            in_specs=[pl.BlockSpec((MOE_TM, D), lambda i, te, nt: (i, 0)),
                      pl.BlockSpec((None, None, D, MOE_FF), lambda i, te, nt: (layer, te[i], 0, 0)),
                      pl.BlockSpec((None, None, D, MOE_FF), lambda i, te, nt: (layer, te[i], 0, 0)),
                      pl.BlockSpec((None, None, MOE_FF, D), lambda i, te, nt: (layer, te[i], 0, 0))],
            out_specs=pl.BlockSpec((MOE_TM, D), lambda i, te, nt: (i, 0)),
            scratch_shapes=[pltpu.VMEM((D, MOE_FF), BF), pltpu.VMEM((D, MOE_FF), BF), pltpu.VMEM((MOE_FF, D), BF)],
        ),
        out_shape=jax.ShapeDtypeStruct((n_tiles * MOE_TM, D), F32),
        compiler_params=_cparams(("arbitrary",)),
        name="moe_experts",
    )(te, n_valid.reshape(1).astype(jnp.int32), xs, w1, w3, w2)

    lat_b = tiles_b - 1
    n_steps = batch * lat_b if latent_only else n_tt

    def tile_of(i):
        return (i // lat_b) * tiles_b + 1 + i % lat_b if latent_only else i

    def next_tile_of(i):
        return tile_of(jnp.minimum(i + 1, n_steps - 1))

    h_new = pl.pallas_call(
        functools.partial(_combine_kernel, n_steps=n_steps),
        grid=(n_steps,),
        in_specs=[pl.BlockSpec((None, 1, 2 * TM), lambda i: (tile_of(i), 0, 0), memory_space=pltpu.SMEM),
                  pl.BlockSpec((None, 1, 2 * TM), lambda i: (next_tile_of(i), 0, 0), memory_space=pltpu.SMEM),
                  pl.BlockSpec(memory_space=pl.ANY),
                  pl.BlockSpec((TM, D), lambda i: (tile_of(i), 0)),
                  pl.BlockSpec((None, None, 8, D),
                               lambda i: (tile_of(i) // tiles_b, jnp.minimum(tile_of(i) % tiles_b, 1), 0, 0)),
                  pl.BlockSpec((TM, LANES), lambda i: (tile_of(i), 0))],
        out_specs=pl.BlockSpec((TM, D), lambda i: (i, 0)),
        out_shape=jax.ShapeDtypeStruct((n_steps * TM, D), F32),
        scratch_shapes=[pltpu.VMEM((2, 2 * TM, D), F32), pltpu.SemaphoreType.DMA((2,))],
        compiler_params=_cparams(("arbitrary",)),
        name="moe_combine",
    )(dest, dest, out, h.reshape(n_tok, D), mod, wts.reshape(n_tok, LANES))
    return h_new.reshape(batch, n_steps * TM // batch, D)


def kernel(x, c, ctx, c_ctx, ada_w, ada_b, norm1_g, norm2_g, da_w_in, da_w_out, da_q_g, da_k_g, da_lam_q1, da_lam_k1, da_lam_q2, da_lam_k2, da_sub_g, gm_w_in, gm_v_g, gm_w_s, gm_b_s, gm_w_out, ssm_w_in, ssm_conv_w, ssm_conv_b, ssm_dt_bias, ssm_a_log, ssm_d, ssm_out_g, ssm_w_out, mla_w_in, mla_q_norm_g, mla_kv_norm_g, mla_w_uq, mla_w_ukv, mla_q_g, mla_k_g, mla_w_out, moe_w_group, moe_b_group, moe_w_router, moe_b_router, moe_w1, moe_w3, moe_w2):
    batch, seq, _ = x.shape
    assert ctx.shape[1] == CTX and seq % TM == 0 and seq % GRID_W == 0
    h = jnp.concatenate([ctx, x], axis=1)

    rows = ((batch + 1 + 7) // 8) * 8
    cc = jnp.concatenate([c, c_ctx[None, :], jnp.zeros((rows - batch - 1, D), F32)], axis=0)
    ada = _ada_all(cc, ada_w, ada_b)
    mod_l = ada[:, :batch].reshape(DEPTH, batch, 6, D)
    mod_c = jnp.broadcast_to(ada[:, batch].reshape(DEPTH, 1, 6, D), (DEPTH, batch, 6, D))
    mod_all = jnp.stack([mod_c, mod_l], axis=2)
    mod_all = jnp.concatenate([mod_all, jnp.zeros((DEPTH, batch, 2, 2, D), F32)], axis=3)

    pad = LANES - MOE_EXPERTS - MOE_GROUPS
    for i in range(DEPTH):
        kind = i % 4
        mod = mod_all[i]
        n1g = norm1_g[i].reshape(1, D)
        n2g = norm2_g[i].reshape(1, D)
        wr = jnp.concatenate([moe_w_router[i], moe_w_group[i], jnp.zeros((D, pad), F32)], axis=1).astype(BF)
        br = jnp.concatenate([moe_b_router[i], moe_b_group[i], jnp.zeros((pad,), F32)]).reshape(1, LANES)
        if kind == 0:
            a = _diff_attention(h, mod, n1g, da_w_in[0], da_q_g[0], da_k_g[0], da_lam_q1[0], da_lam_k1[0],
                                da_lam_q2[0], da_lam_k2[0], da_sub_g[0], i)
            h, xn, ids, wts, cnt =_outproj(a, da_w_out[0].astype(BF), h, mod, n2g, wr, br)
        elif kind == 1:
            h, xn, ids, wts, cnt =_gmlp_layer(h, mod, n1g, gm_w_in[0], gm_v_g[0], gm_w_s[0], gm_b_s[0], gm_w_out[0],
                                          n2g, wr, br)
        elif kind == 2:
            h, xn, ids, wts, cnt =_mamba_layer(h, mod, n1g, ssm_w_in[0], ssm_conv_w[0], ssm_conv_b[0], ssm_dt_bias[0],
                                           ssm_a_log[0], ssm_d[0], ssm_out_g[0], ssm_w_out[0], n2g, wr, br)
        else:
            a = _mla_attention(h, mod, n1g, mla_w_in[0], mla_q_norm_g[0], mla_kv_norm_g[0], mla_w_uq[0],
                               mla_w_ukv[0], mla_q_g[0], mla_k_g[0])
            h, xn, ids, wts, cnt =_outproj(a, mla_w_out[0].astype(BF), h, mod, n2g, wr, br)
        h = _moe(h, mod, xn, ids, wts, cnt, moe_w1, moe_w3, moe_w2, i, latent_only=(i == DEPTH - 1))
    return h
```

```python
import functools
import math

import numpy as np
import jax
import jax.numpy as jnp
from jax import lax
from jax.experimental import pallas as pl
from jax.experimental.pallas import tpu as pltpu

F32 = jnp.float32
BF = jnp.bfloat16
HIGHEST = lax.Precision.HIGHEST

D = 1024
CTX = 256
GRID_W = 64
EPS = 1e-6
ROPE_BASE = 10000.0
DEPTH = 4
LANES = 128
TM = 256

DA_HEADS = 8
DA_HEAD_DIM = 64

GM_CHUNK = 128
GM_GROUPS = 8

SSM_INNER = 2048
SSM_HEADS = 32
SSM_GROUPS = 4
SSM_STATE = 128
SSM_CONV = 5
SSM_CHUNK = 128
SSM_CONV_DIM = SSM_INNER + 2 * SSM_GROUPS * SSM_STATE

MLA_HEADS = 16
MLA_Q_RANK = 384
MLA_KV_RANK = 256
MLA_NOPE = 64
MLA_ROPE = 32
MLA_V = 64
MLA_QK = MLA_NOPE + MLA_ROPE

MOE_GROUPS = 4
MOE_PER_GROUP = 8
MOE_EXPERTS = 32
MOE_FF = 512
MOE_TM = 256

VMEM_LIMIT = 56 * 1024 * 1024


def _cparams(sem):
    return pltpu.CompilerParams(dimension_semantics=sem, vmem_limit_bytes=VMEM_LIMIT)


def _full_spec(arr):
    nd = arr.ndim
    return pl.BlockSpec(arr.shape, lambda *_: (0,) * nd)


def _tok_spec(width, col=0):
    return pl.BlockSpec((None, TM, width), lambda b, t: (b, t, col))


def _mod_spec():
    return pl.BlockSpec((None, None, 8, D), lambda b, t: (b, jnp.minimum(t, 1), 0, 0))


def _silu(x):
    return x * (1.0 / (1.0 + jnp.exp(-x)))


def _norm_mod(h, g, scale, shift):
    ms = jnp.mean(h * h, axis=-1, keepdims=True)
    y = h * lax.rsqrt(ms + EPS) * g
    return y * (1.0 + scale) + shift


def _ada_kernel(c_ref, w_ref, b_ref, o_ref):
    a = _silu(c_ref[...]).astype(BF)
    o_ref[...] = jnp.dot(a, w_ref[...].astype(BF), preferred_element_type=F32) + b_ref[...]


def _ada_all(cc, ada_w, ada_b):
    rows = cc.shape[0]
    tn = 1536
    return pl.pallas_call(
        _ada_kernel,
        grid=(DEPTH, 6 * D // tn),
        in_specs=[pl.BlockSpec((rows, D), lambda l, j: (0, 0)),
                  pl.BlockSpec((None, D, tn), lambda l, j: (l, 0, j)),
                  pl.BlockSpec((None, 1, tn), lambda l, j: (l, 0, j))],
        out_specs=pl.BlockSpec((None, rows, tn), lambda l, j: (l, 0, j)),
        out_shape=jax.ShapeDtypeStruct((DEPTH, rows, 6 * D), F32),
        compiler_params=_cparams(("arbitrary", "arbitrary")),
        name="ada_mod",
    )(cc, ada_w, ada_b.reshape(DEPTH, 1, 6 * D))


def _residual_and_route(delta, h_ref, mod, n2g_ref, wr_ref, br_ref, outs):
    ho_ref, xo_ref, id_ref, wt_ref, cnt_ref = outs
    hn = h_ref[...] + mod[2:3] * delta
    ho_ref[...] = hn
    x2 = _norm_mod(hn, n2g_ref[...], mod[4:5], mod[3:4])
    xo_ref[...] = x2
    lg = jnp.dot(x2.astype(BF), wr_ref[...], preferred_element_type=F32) + br_ref[...]
    lane = lax.broadcasted_iota(jnp.int32, lg.shape, 1)
    neg = jnp.float32(-jnp.inf)
    big = jnp.int32(1 << 20)
    is_g = jnp.logical_and(lane >= MOE_EXPERTS, lane < MOE_EXPERTS + MOE_GROUPS)
    gl = jnp.where(is_g, lg, neg)
    gmax = jnp.max(gl, axis=-1, keepdims=True)
    gidx = jnp.min(jnp.where(gl == gmax, lane, big), axis=-1, keepdims=True) - MOE_EXPERTS
    pg = 1.0 / jnp.sum(jnp.where(is_g, jnp.exp(gl - gmax), 0.0), axis=-1, keepdims=True)
    lo = gidx * MOE_PER_GROUP
    in_grp = jnp.logical_and(lane >= lo, lane < lo + MOE_PER_GROUP)
    el = jnp.where(in_grp, lg, neg)
    e1 = jnp.max(el, axis=-1, keepdims=True)
    i1 = jnp.min(jnp.where(el == e1, lane, big), axis=-1, keepdims=True)
    el2 = jnp.where(lane == i1, neg, el)
    e2 = jnp.max(el2, axis=-1, keepdims=True)
    i2 = jnp.min(jnp.where(el2 == e2, lane, big), axis=-1, keepdims=True)
    t = jnp.exp(e2 - e1)
    w1 = pg / (1.0 + t)
    w2 = pg * t / (1.0 + t)
    wt_ref[...] = jnp.where(lane == 0, w1, jnp.where(lane == 1, w2, 0.0))
    sel1 = lane == i1
    sel2 = lane == i2
    chosen = jnp.where(jnp.logical_or(sel1, sel2), 1.0, 0.0)
    cnt_ref[...] = jnp.sum(chosen, axis=0, keepdims=True)
    rows = lg.shape[0]
    r = lax.broadcasted_iota(jnp.int32, (rows, rows), 0)
    c = lax.broadcasted_iota(jnp.int32, (rows, rows), 1)
    earlier = jnp.where(c < r, 1.0, 0.0).astype(BF)
    before = jnp.dot(earlier, chosen.astype(BF), preferred_element_type=F32)
    rank1 = jnp.sum(jnp.where(sel1, before, 0.0), axis=-1, keepdims=True).astype(jnp.int32)
    rank2 = jnp.sum(jnp.where(sel2, before, 0.0), axis=-1, keepdims=True).astype(jnp.int32)
    id_ref[...] = jnp.where(lane == 0, i1, jnp.where(lane == 1, i2,
                            jnp.where(lane == 2, rank1, jnp.where(lane == 3, rank2, 0))))


def _tail_in_specs(n2g, wr, br):
    return [_full_spec(n2g), _full_spec(wr), _full_spec(br)]


def _tail_out(batch, t_all):
    specs = [_tok_spec(D), _tok_spec(D), _tok_spec(LANES), _tok_spec(LANES),
             pl.BlockSpec((None, None, 1, LANES), lambda b, t: (b, t, 0, 0))]
    shapes = [jax.ShapeDtypeStruct((batch, t_all, D), F32),
              jax.ShapeDtypeStruct((batch, t_all, D), F32),
              jax.ShapeDtypeStruct((batch, t_all, LANES), jnp.int32),
              jax.ShapeDtypeStruct((batch, t_all, LANES), F32),
              jax.ShapeDtypeStruct((batch, t_all // TM, 1, LANES), F32)]
    return specs, shapes


def _outproj_kernel(a_ref, w_ref, h_ref, mod_ref, n2g_ref, wr_ref, br_ref, *outs):
    delta = jnp.dot(a_ref[...], w_ref[...], preferred_element_type=F32)
    _residual_and_route(delta, h_ref, mod_ref[...], n2g_ref, wr_ref, br_ref, outs)


def _outproj(a, w, h, mod, n2g, wr, br):
    batch, t_all, k = a.shape
    out_specs, out_shapes = _tail_out(batch, t_all)
    return pl.pallas_call(
        _outproj_kernel,
        grid=(batch, t_all // TM),
        in_specs=[_tok_spec(k), _full_spec(w), _tok_spec(D), _mod_spec()] + _tail_in_specs(n2g, wr, br),
        out_specs=out_specs,
        out_shape=out_shapes,
        compiler_params=_cparams(("parallel", "parallel")),
        name="outproj_route",
    )(a, w, h, mod, n2g, wr, br)


def _rope_tables(t_lat, rot_dim):
    rows = t_lat // GRID_W
    row = np.repeat(np.arange(rows, dtype=np.float32), GRID_W)
    col = np.tile(np.arange(GRID_W, dtype=np.float32), rows)
    n_freq = rot_dim // 4
    inv_freq = jnp.asarray(ROPE_BASE, F32) ** (-jnp.arange(n_freq, dtype=F32) / n_freq)
    ang = jnp.concatenate([jnp.asarray(row)[:, None] * inv_freq, jnp.asarray(col)[:, None] * inv_freq], axis=-1)
    return jnp.cos(ang), jnp.sin(ang)


def _with_ctx_identity(cos_l, sin_l):
    cos = jnp.concatenate([jnp.ones((CTX, LANES), F32), cos_l], axis=0)
    sin = jnp.concatenate([jnp.zeros((CTX, LANES), F32), sin_l], axis=0)
    return cos, sin


def _da_perm():
    perm = np.zeros(2 * DA_HEADS * DA_HEAD_DIM, np.int32)
    for h in range(DA_HEADS):
        for m in range(2):
            for i in range(DA_HEAD_DIM):
                new = h * LANES + (i // 32) * 64 + m * 32 + (i % 32)
                perm[new] = (2 * h + m) * DA_HEAD_DIM + i
    return perm


def _da_lane_of_dim():
    lane = np.arange(LANES)
    return (lane // 64) * 32 + lane % 32


def _da_proj_kernel(h_ref, mod_ref, n1g_ref, w_ref, gq_ref, gk_ref, cos_ref, sin_ref, gavg_ref,
                    q_ref, k_ref, v_ref):
    mod = mod_ref[...]
    xn = _norm_mod(h_ref[...], n1g_ref[...], mod[1:2], mod[0:1]).astype(BF)
    cos = cos_ref[...]
    sin = sin_ref[...]
    gavg = gavg_ref[...]
    nq = DA_HEADS * LANES
    scale = DA_HEAD_DIM ** -0.5 * LOG2E
    for dst, off, g_ref, mult in ((q_ref, 0, gq_ref, scale), (k_ref, nq, gk_ref, 1.0)):
        y_all = jnp.dot(xn, w_ref[:, off:off + nq], preferred_element_type=F32)
        for jj in range(DA_HEADS // 2):
            yy = y_all[:, jj * 2 * LANES:(jj + 1) * 2 * LANES]
            ms2 = jnp.dot((yy * yy).astype(BF), gavg, preferred_element_type=F32)
            for j in (2 * jj, 2 * jj + 1):
                y = y_all[:, j * LANES:(j + 1) * LANES]
                ms = ms2[:, (j % 2) * LANES:(j % 2 + 1) * LANES]
                y = y * lax.rsqrt(ms + EPS) * g_ref[...]
                y = y * cos + pltpu.roll(y, 64, 1) * sin
                dst[:, j * LANES:(j + 1) * LANES] = (y * mult).astype(BF)
    v_ref[...] = jnp.dot(xn, w_ref[:, 2 * nq:3 * nq], preferred_element_type=F32).astype(BF)


LOG2E = math.log2(math.e)


def _softmax_pv(q, k, v):
    s = lax.dot_general(q, k, (((1,), (1,)), ((), ())), preferred_element_type=F32)
    e = jnp.exp2(s - jnp.max(s, axis=-1, keepdims=True))
    l = jnp.sum(e, axis=-1, keepdims=True)
    return jnp.dot(e.astype(BF), v, preferred_element_type=F32) / l


ATTN_TILES_PER_ITER = 4


def _for_each_latent_tile(tile_fn, t_all):
    n_lat = (t_all - CTX) // TM
    per = ATTN_TILES_PER_ITER if n_lat % ATTN_TILES_PER_ITER == 0 else 1

    def body(j, carry):
        for u in range(per):
            tile_fn(pl.multiple_of(CTX + (j * per + u) * TM, TM), t_all)
        return carry

    lax.fori_loop(0, n_lat // per, body, 0)


def _da_attn_kernel(lam_ref, q_ref, k_ref, v_ref, subg_ref, o_ref, *, lam_init):
    t_all = q_ref.shape[0]
    lane = lax.broadcasted_iota(jnp.int32, (TM, LANES), 1)
    comp0 = ((lane // 32) % 2) == 0
    zero = jnp.zeros((TM, LANES), BF)

    def tile(row0, n_keys):
        q = q_ref[pl.ds(row0, TM), :]
        k = k_ref[0:n_keys, :]
        v = v_ref[0:n_keys, :]
        o = _softmax_pv(jnp.where(comp0, q, zero), k, v) - lam_ref[0] * _softmax_pv(jnp.where(comp0, zero, q), k, v)
        ms = jnp.mean(o * o, axis=-1, keepdims=True)
        o = o * lax.rsqrt(ms + EPS) * subg_ref[...] * (1.0 - lam_init)
        o_ref[pl.ds(row0, TM), :] = o.astype(BF)

    tile(0, CTX)

    _for_each_latent_tile(tile, t_all)


def _diff_attention(h, mod, n1g, w_in, q_g, k_g, lam_q1, lam_k1, lam_q2, lam_k2, sub_g, layer_idx):
    batch, t_all, _ = h.shape
    nq = DA_HEADS * LANES
    perm = _da_perm()
    w = jnp.concatenate([w_in[:, :nq][:, perm], w_in[:, nq:2 * nq][:, perm], w_in[:, 2 * nq:]], axis=1).astype(BF)
    ldim = _da_lane_of_dim()
    gq = q_g[ldim].reshape(1, LANES)
    gk = k_g[ldim].reshape(1, LANES)
    cos, sin = _rope_tables(t_all - CTX, DA_HEAD_DIM)
    cos, sin = _with_ctx_identity(jnp.tile(cos, (1, 4)), jnp.concatenate([-sin, -sin, sin, sin], axis=1))
    unit = np.arange(2 * LANES) // LANES * 2 + (np.arange(2 * LANES) // 32) % 2
    gavg = jnp.asarray((unit[:, None] == unit[None, :]).astype(np.float32) / DA_HEAD_DIM).astype(BF)
    tab_spec = pl.BlockSpec((TM, LANES), lambda b, t: (t, 0))
    qkv_shape = jax.ShapeDtypeStruct((batch, t_all, nq), BF)
    q, k, v = pl.pallas_call(
        _da_proj_kernel,
        grid=(batch, t_all // TM),
        in_specs=[_tok_spec(D), _mod_spec(), _full_spec(n1g), _full_spec(w), _full_spec(gq), _full_spec(gk),
                  tab_spec, tab_spec, _full_spec(gavg)],
        out_specs=[_tok_spec(nq)] * 3,
        out_shape=[qkv_shape] * 3,
        compiler_params=_cparams(("parallel", "parallel")),
        name="da_proj",
    )(h, mod, n1g, w, gq, gk, cos, sin, gavg)

    lam_init = 0.8 - 0.6 * math.exp(-0.3 * layer_idx)
    lam = (jnp.exp(jnp.sum(lam_q1 * lam_k1)) - jnp.exp(jnp.sum(lam_q2 * lam_k2)) + lam_init).reshape(1)
    seq_spec = pl.BlockSpec((None, t_all, LANES), lambda b, hh: (b, 0, hh))
    subg = sub_g.reshape(1, LANES)
    return pl.pallas_call(
        functools.partial(_da_attn_kernel, lam_init=lam_init),
        grid=(batch, DA_HEADS),
        in_specs=[pl.BlockSpec(memory_space=pltpu.SMEM), seq_spec, seq_spec, seq_spec,
                  pl.BlockSpec((1, LANES), lambda b, hh: (0, 0))],
        out_specs=seq_spec,
        out_shape=jax.ShapeDtypeStruct((batch, t_all, nq), BF),
        compiler_params=_cparams(("parallel", "parallel")),
        name="da_attn",
    )(lam, q, k, v, subg)


def _gmlp_kernel(h_ref, mod_ref, n1g_ref, win_ref, vg_ref, ws_ref, bs_ref, wout_ref,
                 n2g_ref, wr_ref, br_ref, *outs):
    mod = mod_ref[...]
    xn = _norm_mod(h_ref[...], n1g_ref[...], mod[1:2], mod[0:1]).astype(BF)
    uv =jnp.dot(xn, win_ref[...], preferred_element_type=F32)
    uv = 0.5 * uv * (1.0 + lax.erf(uv * (2.0 ** -0.5)))
    u = uv[:, :D]
    v = uv[:, D:]
    v = (v * lax.rsqrt(jnp.mean(v * v, axis=-1, keepdims=True) + EPS) * vg_ref[...]).astype(BF)
    rows = []
    for c in range(TM // GM_CHUNK):
        cols = []
        for g in range(GM_GROUPS):
            vb = v[c * GM_CHUNK:(c + 1) * GM_CHUNK, g * LANES:(g + 1) * LANES]
            s = jnp.dot(ws_ref[g], vb, preferred_element_type=F32) + bs_ref[g]
            cols.append(s)
        rows.append(jnp.concatenate(cols, axis=1))
    s_all = jnp.concatenate(rows, axis=0)
    delta = jnp.dot((u * s_all).astype(BF), wout_ref[...], preferred_element_type=F32)
    _residual_and_route(delta, h_ref, mod, n2g_ref, wr_ref, br_ref, outs)


def _gmlp_layer(h, mod, n1g, w_in, v_g, w_s, b_s, w_out, n2g, wr, br):
    batch, t_all, _ = h.shape
    win = w_in.astype(BF)
    wout = w_out.astype(BF)
    ws = w_s.astype(BF)
    bs = jnp.broadcast_to(b_s[:, :, None], (GM_GROUPS, GM_CHUNK, LANES)).astype(F32)
    vg = v_g.reshape(1, D)
    out_specs, out_shapes = _tail_out(batch, t_all)
    return pl.pallas_call(
        _gmlp_kernel,
        grid=(batch, t_all // TM),
        in_specs=[_tok_spec(D), _mod_spec(), _full_spec(n1g), _full_spec(win), _full_spec(vg), _full_spec(ws),
                  _full_spec(bs), _full_spec(wout)] + _tail_in_specs(n2g, wr, br),
        out_specs=out_specs,
        out_shape=out_shapes,
        compiler_params=_cparams(("parallel", "parallel")),
        name="gmlp_layer",
    )(h, mod, n1g, win, vg, ws, bs, wout, n2g, wr, br)


SSM_DT_PAD = LANES


def _ssm_proj_kernel(h_ref, mod_ref, n1g_ref, w_ref, dtb_ref, z_ref, xbc_ref, dt_ref):
    mod = mod_ref[...]
    xn = _norm_mod(h_ref[...], n1g_ref[...], mod[1:2], mod[0:1]).astype(BF)
    z_ref[...] = jnp.dot(xn, w_ref[:, :SSM_INNER], preferred_element_type=F32).astype(BF)
    c0 = SSM_INNER
    for j in range(SSM_CONV_DIM // 1024):
        xbc_ref[:, j * 1024:(j + 1) * 1024] = jnp.dot(
            xn, w_ref[:, c0 + j * 1024:c0 + (j + 1) * 1024], preferred_element_type=F32).astype(BF)
    c1 = SSM_INNER + SSM_CONV_DIM
    raw = jnp.dot(xn, w_ref[:, c1:c1 + SSM_DT_PAD], preferred_element_type=F32) + dtb_ref[...]
    dt_ref[...] = jnp.maximum(raw, 0.0) + jnp.log1p(jnp.exp(-jnp.abs(raw)))


CONV_PAD = 8
CONV_W = 512


def _conv_kernel(x_ref, w_ref, b_ref, o_ref, pad_ref):
    t_all = x_ref.shape[0]
    t_lat = t_all - CTX
    lat0 = CTX + 2 * CONV_PAD
    zeros = jnp.zeros((CONV_PAD, CONV_W), F32)
    pad_ref[0:CONV_PAD, :] = zeros
    pad_ref[CONV_PAD + CTX:lat0, :] = jnp.zeros((CONV_PAD, CONV_W), F32)
    pad_ref[lat0 + t_lat:lat0 + t_lat + CONV_PAD, :] = zeros
    pad_ref[CONV_PAD:CONV_PAD + CTX, :] = x_ref[0:CTX, :].astype(F32)
    pad_ref[lat0:lat0 + t_lat, :] = x_ref[CTX:t_all, :].astype(F32)
    half = (SSM_CONV - 1) // 2
    for start, n, dst in ((CONV_PAD, CTX, 0), (lat0, t_lat, CTX)):
        acc = jnp.zeros((n, CONV_W), F32) + b_ref[...]
        for kk in range(SSM_CONV):
            acc = acc + pad_ref[start + kk - half:start + kk - half + n, :] * w_ref[kk:kk + 1, :]
        o_ref[dst:dst + n, :] = _silu(acc).astype(BF)


def _ssd_kernel(dt_ref, a_ref, x_ref, b_ref, c_ref, y_ref, s_ref):
    direction = pl.program_id(1)
    step = pl.program_id(2)

    @pl.when(step == 0)
    def _():
        s_ref[...] = jnp.zeros_like(s_ref)

    q = SSM_CHUNK
    row = lax.broadcasted_iota(jnp.int32, (q, q), 0)
    col = lax.broadcasted_iota(jnp.int32, (q, q), 1)
    sign = 1 - 2 * direction
    tri_b = (col - row) * sign <= 0
    tri = tri_b.astype(F32)
    tri_t = ((row - col) * sign <= 0).astype(F32)
    eye = (row == col).astype(F32)
    lane_lo = col < 64

    dt_t = dt_ref[...]
    a_t = dt_t * a_ref[...]
    cs_rows = jnp.dot(a_t, tri_t, preferred_element_type=F32, precision=HIGHEST)
    tot = jnp.sum(a_t, axis=1, keepdims=True)
    heads_per_group = SSM_HEADS // SSM_GROUPS

    for g in range(SSM_GROUPS):
        bm = b_ref[:, g * SSM_STATE:(g + 1) * SSM_STATE]
        cm = c_ref[:, g * SSM_STATE:(g + 1) * SSM_STATE]
        cb = lax.dot_general(cm, bm, (((1,), (1,)), ((), ())), preferred_element_type=F32)
        bm_t = bm.astype(F32).T.astype(BF)
        for pp in range(heads_per_group // 2):
            p = g * (heads_per_group // 2) + pp
            xp = x_ref[:, p * LANES:(p + 1) * LANES].astype(F32)
            cs_col, dt_col, mats = [], [], []
            for hh in range(2):
                e = 2 * p + hh
                cc = jnp.sum(tri * a_t[e:e + 1, :], axis=1, keepdims=True)
                cs_col.append(cc)
                dt_col.append(jnp.sum(eye * dt_t[e:e + 1, :], axis=1, keepdims=True))
                decay = jnp.exp(jnp.where(tri_b, cc - cs_rows[e:e + 1, :], -jnp.inf))
                mats.append((cb * decay).astype(BF))
            xdt = xp * jnp.where(lane_lo, dt_col[0], dt_col[1])
            xb = xdt.astype(BF)
            zb = jnp.zeros_like(xb)
            y_diag = jnp.dot(jnp.concatenate(mats, axis=1),
                             jnp.concatenate([jnp.where(lane_lo, xb, zb), jnp.where(lane_lo, zb, xb)], axis=0),
                             preferred_element_type=F32)
            s_prev = s_ref[p]
            y_off = jnp.dot(cm, s_prev.astype(BF), preferred_element_type=F32)
            y_off = y_off * jnp.exp(jnp.where(lane_lo, cs_col[0], cs_col[1]))
            y_ref[:, p * LANES:(p + 1) * LANES] = (y_diag + y_off).astype(BF)
            t0 = tot[2 * p:2 * p + 1, :]
            t1 = tot[2 * p + 1:2 * p + 2, :]
            to_end = jnp.exp(jnp.where(lane_lo, t0 - cs_col[0], t1 - cs_col[1]))
            upd = jnp.dot(bm_t, (xdt * to_end).astype(BF), preferred_element_type=F32)
            s_ref[p] = s_prev * jnp.exp(jnp.where(lane_lo[0:1, :], t0, t1)) + upd


def _ssm_finish_kernel(yf_ref, yb_ref, xs_ref, z_ref, d_ref, og_ref, w_ref, h_ref, mod_ref,
                       n2g_ref, wr_ref, br_ref, *outs):
    y = yf_ref[...].astype(F32) + yb_ref[...].astype(F32) + d_ref[...] * xs_ref[...].astype(F32)
    y = y * _silu(z_ref[...].astype(F32))
    y = y * lax.rsqrt(jnp.mean(y * y, axis=-1, keepdims=True) + EPS) * og_ref[...]
    delta = jnp.dot(y.astype(BF), w_ref[...], preferred_element_type=F32)
    _residual_and_route(delta, h_ref, mod_ref[...], n2g_ref, wr_ref, br_ref, outs)


def _mamba_layer(h, mod, n1g, w_in, conv_w, conv_b, dt_bias, a_log, d_skip, out_g, w_out, n2g, wr, br):
    batch, t_all, _ = h.shape
    n_dt = 2 * SSM_HEADS
    c1 = SSM_INNER + SSM_CONV_DIM
    w = jnp.concatenate([w_in, jnp.zeros((D, SSM_DT_PAD - n_dt), F32)], axis=1).astype(BF)
    dtb = jnp.concatenate([dt_bias.reshape(n_dt), jnp.zeros((SSM_DT_PAD - n_dt,), F32)]).reshape(1, SSM_DT_PAD)
    z, xbc, dt = pl.pallas_call(
        _ssm_proj_kernel,
        grid=(batch, t_all // TM),
        in_specs=[_tok_spec(D), _mod_spec(), _full_spec(n1g), _full_spec(w), _full_spec(dtb)],
        out_specs=[_tok_spec(SSM_INNER), _tok_spec(SSM_CONV_DIM), _tok_spec(SSM_DT_PAD)],
        out_shape=[jax.ShapeDtypeStruct((batch, t_all, SSM_INNER), BF),
                   jax.ShapeDtypeStruct((batch, t_all, SSM_CONV_DIM), BF),
                   jax.ShapeDtypeStruct((batch, t_all, SSM_DT_PAD), F32)],
        compiler_params=_cparams(("parallel", "parallel")),
        name="ssm_proj",
    )(h, mod, n1g, w, dtb)
    assert c1 + n_dt == w_in.shape[1]

    cw = jnp.concatenate([conv_w, jnp.zeros((8 - SSM_CONV, SSM_CONV_DIM), F32)], axis=0)
    cbias = conv_b.reshape(1, SSM_CONV_DIM)
    xbc = pl.pallas_call(
        _conv_kernel,
        grid=(batch, SSM_CONV_DIM // CONV_W),
        in_specs=[pl.BlockSpec((None, t_all, CONV_W), lambda b, j: (b, 0, j)),
                  pl.BlockSpec((8, CONV_W), lambda b, j: (0, j)),
                  pl.BlockSpec((1, CONV_W), lambda b, j: (0, j))],
        out_specs=pl.BlockSpec((None, t_all, CONV_W), lambda b, j: (b, 0, j)),
        out_shape=jax.ShapeDtypeStruct((batch, t_all, SSM_CONV_DIM), BF),
        scratch_shapes=[pltpu.VMEM((t_all + 3 * CONV_PAD, CONV_W), F32)],
        compiler_params=_cparams(("parallel", "parallel")),
        name="ssm_conv",
    )(xbc, cw, cbias)

    n_chunks = t_all // SSM_CHUNK
    ctx_chunks = CTX // SSM_CHUNK
    dt_t = jnp.swapaxes(dt[:, :, :n_dt], 1, 2)
    a_neg = jnp.broadcast_to(-jnp.exp(a_log.astype(F32)).reshape(2, SSM_HEADS, 1), (2, SSM_HEADS, SSM_CHUNK))

    def chunk_of(dr, s):
        rev = jnp.where(s < ctx_chunks, ctx_chunks - 1 - s, n_chunks - 1 + ctx_chunks - s)
        return jnp.where(dr == 0, s, rev)

    gs = SSM_GROUPS * SSM_STATE
    y = pl.pallas_call(
        _ssd_kernel,
        grid=(batch, 2, n_chunks),
        in_specs=[pl.BlockSpec((None, SSM_HEADS, SSM_CHUNK), lambda b, dr, s: (b, dr, chunk_of(dr, s))),
                  pl.BlockSpec((None, SSM_HEADS, SSM_CHUNK), lambda b, dr, s: (dr, 0, 0)),
                  pl.BlockSpec((None, SSM_CHUNK, SSM_INNER), lambda b, dr, s: (b, chunk_of(dr, s), 0)),
                  pl.BlockSpec((None, SSM_CHUNK, gs), lambda b, dr, s: (b, chunk_of(dr, s), SSM_INNER // gs)),
                  pl.BlockSpec((None, SSM_CHUNK, gs), lambda b, dr, s: (b, chunk_of(dr, s), SSM_INNER // gs + 1))],
        out_specs=pl.BlockSpec((None, None, SSM_CHUNK, SSM_INNER), lambda b, dr, s: (dr, b, chunk_of(dr, s), 0)),
        out_shape=jax.ShapeDtypeStruct((2, batch, t_all, SSM_INNER), BF),
        scratch_shapes=[pltpu.VMEM((SSM_HEADS // 2, SSM_STATE, LANES), F32)],
        compiler_params=_cparams(("parallel", "parallel", "arbitrary")),
        name="ssd_scan",
    )(dt_t, a_neg, xbc, xbc, xbc)

    dvec = jnp.repeat(d_skip, SSM_INNER // SSM_HEADS).reshape(1, SSM_INNER)
    og = out_g.reshape(1, SSM_INNER)
    wo = w_out.astype(BF)
    out_specs, out_shapes = _tail_out(batch, t_all)
    y_spec = lambda dr: pl.BlockSpec((None, None, TM, SSM_INNER), lambda b, t: (dr, b, t, 0))
    return pl.pallas_call(
        _ssm_finish_kernel,
        grid=(batch, t_all // TM),
        in_specs=[y_spec(0), y_spec(1), _tok_spec(SSM_INNER), _tok_spec(SSM_INNER), _full_spec(dvec),
                  _full_spec(og), _full_spec(wo), _tok_spec(D), _mod_spec()] + _tail_in_specs(n2g, wr, br),
        out_specs=out_specs,
        out_shape=out_shapes,
        compiler_params=_cparams(("parallel", "parallel")),
        name="ssm_finish",
    )(y, y, xbc, z, dvec, og, wo, h, mod, n2g, wr, br)


def _mla_lane_src():
    src = np.full(LANES, -1, np.int32)
    src[0:16] = MLA_NOPE + np.arange(16)
    src[16:64] = np.arange(48)
    src[64:80] = MLA_NOPE + 16 + np.arange(16)
    src[80:96] = 48 + np.arange(16)
    return src


def _mla_proj_kernel(h_ref, mod_ref, n1g_ref, win_ref, qng_ref, kvng_ref, wuq_ref, wuk_ref, wuv_ref,
                     gq_ref, gk_ref, cos_ref, sin_ref, avg_ref, q_ref, k_ref, v_ref):
    mod = mod_ref[...]
    xn = _norm_mod(h_ref[...], n1g_ref[...], mod[1:2], mod[0:1]).astype(BF)
    lat = jnp.dot(xn, win_ref[...], preferred_element_type=F32)
    cq = lat[:, :MLA_Q_RANK]
    ckv = lat[:, MLA_Q_RANK:MLA_Q_RANK + MLA_KV_RANK]
    kpe = lat[:, MLA_Q_RANK + MLA_KV_RANK:]
    cq = (cq * lax.rsqrt(jnp.mean(cq * cq, axis=-1, keepdims=True) + EPS) * qng_ref[...]).astype(BF)
    ckv = (ckv * lax.rsqrt(jnp.mean(ckv * ckv, axis=-1, keepdims=True) + EPS) * kvng_ref[...]).astype(BF)
    cos = cos_ref[...]
    sin = sin_ref[...]
    avg = avg_ref[...]
    scale = MLA_QK ** -0.5 * LOG2E
    qa = jnp.dot(cq, wuq_ref[...], preferred_element_type=F32)
    ka = jnp.dot(ckv, wuk_ref[...], preferred_element_type=F32)
    kpe2 = jnp.concatenate([kpe, kpe], axis=1)
    for jj in range(MLA_HEADS // 2):
        for dst, src, add, g_ref, mult in ((q_ref, qa, None, gq_ref, scale), (k_ref, ka, kpe2, gk_ref, 1.0)):
            yy = src[:, jj * 2 * LANES:(jj + 1) * 2 * LANES]
            if add is not None:
                yy = yy + add
            ms2 = jnp.dot((yy * yy).astype(BF), avg, preferred_element_type=F32) * (1.0 / MLA_QK)
            for hh in range(2):
                j = 2 * jj + hh
                y = yy[:, hh * LANES:(hh + 1) * LANES]
                y = y * lax.rsqrt(ms2[:, hh * LANES:(hh + 1) * LANES] + EPS) * g_ref[...]
                y = y * cos + pltpu.roll(y, 64, 1) * sin
                dst[:, j * LANES:(j + 1) * LANES] = (y * mult).astype(BF)
    v_ref[...] = jnp.dot(ckv, wuv_ref[...], preferred_element_type=F32).astype(BF)


def _mla_attn_kernel(q_ref, k_ref, v_ref, o_ref):
    t_all = q_ref.shape[0]
    lane = lax.broadcasted_iota(jnp.int32, (TM, LANES), 1)

    def tile(row0, n_keys):
        v = v_ref[0:n_keys, :]
        outs = [_softmax_pv(q_ref[pl.ds(row0, TM), hh * LANES:(hh + 1) * LANES],
                            k_ref[0:n_keys, hh * LANES:(hh + 1) * LANES], v)
                for hh in range(2)]
        o_ref[pl.ds(row0, TM), :] = jnp.where(lane < MLA_V, outs[0], outs[1]).astype(BF)

    tile(0, CTX)

    _for_each_latent_tile(tile, t_all)


def _mla_attention(h, mod, n1g, w_in, q_norm_g, kv_norm_g, w_uq, w_ukv, q_g, k_g):
    batch, t_all, _ = h.shape
    src = _mla_lane_src()
    used = src >= 0
    srcc = np.where(used, src, 0)
    nh = MLA_HEADS * LANES
    kpe_cols = np.where(srcc >= MLA_NOPE, MLA_Q_RANK + MLA_KV_RANK + srcc - MLA_NOPE, 0)
    kpe_used = np.logical_and(used, src >= MLA_NOPE)
    w_kpe = jnp.where(jnp.asarray(kpe_used)[None, :], w_in[:, kpe_cols], 0.0)
    win = jnp.concatenate([w_in[:, :MLA_Q_RANK + MLA_KV_RANK], w_kpe], axis=1).astype(BF)
    q_cols = (np.arange(MLA_HEADS)[:, None] * MLA_QK + srcc[None, :]).reshape(-1)
    q_mask = jnp.asarray(np.tile(used, MLA_HEADS))[None, :]
    wuq = jnp.where(q_mask, w_uq[:, q_cols], 0.0).astype(BF)
    nope_used = np.logical_and(used, src < MLA_NOPE)
    k_cols = (np.arange(MLA_HEADS)[:, None] * (MLA_NOPE + MLA_V) + np.where(nope_used, srcc, 0)[None, :]).reshape(-1)
    k_mask = jnp.asarray(np.tile(nope_used, MLA_HEADS))[None, :]
    wuk = jnp.where(k_mask, w_ukv[:, k_cols], 0.0).astype(BF)
    v_cols = (np.arange(MLA_HEADS)[:, None] * (MLA_NOPE + MLA_V) + MLA_NOPE + np.arange(MLA_V)[None, :]).reshape(-1)
    wuv = w_ukv[:, v_cols].astype(BF)
    gq = jnp.where(jnp.asarray(used), q_g[srcc], 0.0).reshape(1, LANES)
    gk = jnp.where(jnp.asarray(used), k_g[srcc], 0.0).reshape(1, LANES)
    cos16, sin16 = _rope_tables(t_all - CTX, MLA_ROPE)
    t_lat = t_all - CTX
    cos_l = jnp.ones((t_lat, LANES), F32).at[:, 0:16].set(cos16).at[:, 64:80].set(cos16)
    sin_l = jnp.zeros((t_lat, LANES), F32).at[:, 0:16].set(-sin16).at[:, 64:80].set(sin16)
    cos, sin = _with_ctx_identity(cos_l, sin_l)
    head_of_lane = np.arange(2 * LANES) // LANES
    avg = jnp.asarray((head_of_lane[:, None] == head_of_lane[None, :]).astype(np.float32)).astype(BF)
    qng = q_norm_g.reshape(1, MLA_Q_RANK)
    kvng = kv_norm_g.reshape(1, MLA_KV_RANK)
    tab_spec = pl.BlockSpec((TM, LANES), lambda b, t: (t, 0))
    q, k, v = pl.pallas_call(
        _mla_proj_kernel,
        grid=(batch, t_all // TM),
        in_specs=[_tok_spec(D), _mod_spec(), _full_spec(n1g), _full_spec(win), _full_spec(qng), _full_spec(kvng),
                  _full_spec(wuq), _full_spec(wuk), _full_spec(wuv), _full_spec(gq), _full_spec(gk),
                  tab_spec, tab_spec, _full_spec(avg)],
        out_specs=[_tok_spec(nh), _tok_spec(nh), _tok_spec(MLA_HEADS * MLA_V)],
        out_shape=[jax.ShapeDtypeStruct((batch, t_all, nh), BF), jax.ShapeDtypeStruct((batch, t_all, nh), BF),
                   jax.ShapeDtypeStruct((batch, t_all, MLA_HEADS * MLA_V), BF)],
        compiler_params=_cparams(("parallel", "parallel")),
        name="mla_proj",
    )(h, mod, n1g, win, qng, kvng, wuq, wuk, wuv, gq, gk, cos, sin, avg)

    return pl.pallas_call(
        _mla_attn_kernel,
        grid=(batch, MLA_HEADS // 2),
        in_specs=[pl.BlockSpec((None, t_all, 2 * LANES), lambda b, p: (b, 0, p)),
                  pl.BlockSpec((None, t_all, 2 * LANES), lambda b, p: (b, 0, p)),
                  pl.BlockSpec((None, t_all, LANES), lambda b, p: (b, 0, p))],
        out_specs=pl.BlockSpec((None, t_all, LANES), lambda b, p: (b, 0, p)),
        out_shape=jax.ShapeDtypeStruct((batch, t_all, MLA_HEADS * MLA_V), BF),
        compiler_params=_cparams(("parallel", "parallel")),
        name="mla_attn",
    )(q, k, v)


ROWS_PER_ISSUE = 8


def _row_copy(src, src_row, dst, dst_row, sem):
    return pltpu.make_async_copy(src.at[pl.ds(src_row, 1), :], dst.at[pl.ds(dst_row, 1), :], sem)


def _dispatch_kernel(ztile_ref, dest_ref, x_ref, xs_hbm, zbuf, zsem, sem):
    def zero_copy(e):
        start = pl.multiple_of(ztile_ref[e], MOE_TM)
        return pltpu.make_async_copy(zbuf, xs_hbm.at[pl.ds(start, MOE_TM), :], zsem)

    @pl.when(pl.program_id(0) == 0)
    def _():
        zbuf[...] = jnp.zeros_like(zbuf)
        for e in range(2 * MOE_EXPERTS):
            @pl.when(ztile_ref[e] >= 0)
            def _():
                zero_copy(e).start()
        for e in range(2 * MOE_EXPERTS):
            @pl.when(ztile_ref[e] >= 0)
            def _():
                zero_copy(e).wait()

    def issue(blk, carry):
        for u in range(ROWS_PER_ISSUE):
            r = blk * ROWS_PER_ISSUE + u
            for k in range(2):
                _row_copy(x_ref, r, xs_hbm, dest_ref[0, 2 * r + k], sem).start(priority=k)
        return carry

    lax.fori_loop(0, TM // ROWS_PER_ISSUE, issue, 0)
    for k in range(2):
        pltpu.make_async_copy(x_ref, xs_hbm.at[pl.ds(0, TM), :], sem).wait()


def _moe_kernel(te_ref, nt_ref, x_ref, w1_ref, w3_ref, w2_ref, o_ref, w1b, w3b, w2b):
    i = pl.program_id(0)
    prev = te_ref[jnp.maximum(i - 1, 0)]

    @pl.when(jnp.logical_or(i == 0, te_ref[i] != prev))
    def _():
        w1b[...] = w1_ref[...].astype(BF)
        w3b[...] = w3_ref[...].astype(BF)
        w2b[...] = w2_ref[...].astype(BF)

    @pl.when(i < nt_ref[0])
    def _():
        x = x_ref[...].astype(BF)
        h1 = jnp.dot(x, w1b[...], preferred_element_type=F32)
        h3 = jnp.dot(x, w3b[...], preferred_element_type=F32)
        hid = (_silu(h1) * h3).astype(BF)
        o_ref[...] = jnp.dot(hid, w2b[...], preferred_element_type=F32)

    @pl.when(i >= nt_ref[0])
    def _():
        o_ref[...] = jnp.zeros_like(o_ref)


def _combine_kernel(dest_ref, dnext_ref, out_hbm, h_ref, mod_ref, wt_ref, ho_ref, buf, sem, *, n_steps):
    i = pl.program_id(0)
    slot = i % 2

    def fetch(d_ref, s):
        def issue(blk, carry):
            for u in range(ROWS_PER_ISSUE):
                r = blk * ROWS_PER_ISSUE + u
                for k in range(2):
                    _row_copy(out_hbm, d_ref[0, 2 * r + k], buf.at[s], k * TM + r, sem.at[s]).start(priority=k)
            return carry

        lax.fori_loop(0, TM // ROWS_PER_ISSUE, issue, 0)

    @pl.when(i == 0)
    def _():
        fetch(dest_ref, 0)

    @pl.when(i + 1 < n_steps)
    def _():
        fetch(dnext_ref, 1 - slot)

    pltpu.make_async_copy(out_hbm.at[pl.ds(0, 2 * TM), :], buf.at[slot], sem.at[slot]).wait()
    w = wt_ref[...]
    y = buf[slot, 0:TM, :] * w[:, 0:1] + buf[slot, TM:2 * TM, :] * w[:, 1:2]
    ho_ref[...] = h_ref[...] + mod_ref[...][5:6] * y


def _moe(h, mod, xn, ids, wts, cnt, w1, w3, w2, layer, latent_only):
    batch, t_all, _ = xn.shape
    n_tok = batch * t_all
    tiles_b = t_all // TM
    n_tt = n_tok // TM
    n_tiles = (2 * n_tok + MOE_EXPERTS * (MOE_TM - 1) + MOE_TM - 1) // MOE_TM

    cnt = cnt.reshape(n_tt, LANES)[:, :MOE_EXPERTS].astype(jnp.int32)
    tiles_e = (jnp.sum(cnt, axis=0) + MOE_TM - 1) // MOE_TM
    tile_end = jnp.cumsum(tiles_e)
    tile_start = tile_end - tiles_e
    before = jnp.cumsum(cnt, axis=0) - cnt
    base = (tile_start[None, :] * MOE_TM + before).reshape(n_tt * MOE_EXPERTS)
    n_valid = tile_end[-1]
    tidx = jnp.arange(n_tiles, dtype=jnp.int32)
    te = jnp.minimum(jnp.sum((tile_end[None, :] <= tidx[:, None]).astype(jnp.int32), axis=1), MOE_EXPERTS - 1)
    te = jnp.where(tidx < n_valid, te, jnp.max(jnp.where(tiles_e > 0, jnp.arange(MOE_EXPERTS), 0))).astype(jnp.int32)
    assert n_tiles * MOE_TM - 2 * n_tok <= MOE_EXPERTS * MOE_TM
    tail = n_valid + jnp.arange(MOE_EXPERTS, dtype=jnp.int32)
    ztile = jnp.concatenate([jnp.where(tiles_e > 0, (tile_end - 1) * MOE_TM, -1),
                             jnp.where(tail < n_tiles, tail * MOE_TM, -1)]).astype(jnp.int32)

    picks = ids.reshape(n_tt, TM, LANES)
    slot = jnp.arange(n_tt, dtype=jnp.int32)[:, None, None] * MOE_EXPERTS + picks[:, :, 0:2]
    dest = (jnp.take(base, slot, axis=0) + picks[:, :, 2:4]).reshape(n_tt, 1, 2 * TM)
    dest_spec = pl.BlockSpec((None, 1, 2 * TM), lambda i, *_: (i, 0, 0), memory_space=pltpu.SMEM)

    xs = pl.pallas_call(
        _dispatch_kernel,
        grid_spec=pltpu.PrefetchScalarGridSpec(
            num_scalar_prefetch=1,
            grid=(n_tt,),
            in_specs=[dest_spec, pl.BlockSpec((TM, D), lambda i, zt: (i, 0))],
            out_specs=pl.BlockSpec(memory_space=pl.ANY),
            scratch_shapes=[pltpu.VMEM((MOE_TM, D), F32), pltpu.SemaphoreType.DMA(()), pltpu.SemaphoreType.DMA(())],
        ),
        out_shape=jax.ShapeDtypeStruct((n_tiles * MOE_TM, D), F32),
        compiler_params=_cparams(("arbitrary",)),
        name="moe_dispatch",
    )(ztile, dest, xn.reshape(n_tok, D))

    out = pl.pallas_call(
        _moe_kernel,
        grid_spec=pltpu.PrefetchScalarGridSpec(
            num_scalar_prefetch=2,
            grid=(n_tiles,),
            in_specs=[pl.BlockSpec((MOE_TM, D), lambda i, te, nt: (i, 0)),
                      pl.BlockSpec((None, None, D, MOE_FF), lambda i, te, nt: (layer, te[i], 0, 0)),
                      pl.BlockSpec((None, None, D, MOE_FF), lambda i, te, nt: (layer, te[i], 0, 0)),
                      pl.BlockSpec((None, None, MOE_FF, D), lambda i, te, nt: (layer, te[i], 0, 0))],
            out_specs=pl.BlockSpec((MOE_TM, D), lambda i, te, nt: (i, 0)),
            scratch_shapes=[pltpu.VMEM((D, MOE_FF), BF), pltpu.VMEM((D, MOE_FF), BF), pltpu.VMEM((MOE_FF, D), BF)],
        ),
        out_shape=jax.ShapeDtypeStruct((n_tiles * MOE_TM, D), F32),
        compiler_params=_cparams(("arbitrary",)),
        name="moe_experts",
    )(te, n_valid.reshape(1).astype(jnp.int32), xs, w1, w3, w2)

    lat_b = tiles_b - 1
    n_steps = batch * lat_b if latent_only else n_tt

    def tile_of(i):
        return (i // lat_b) * tiles_b + 1 + i % lat_b if latent_only else i

    def next_tile_of(i):
        return tile_of(jnp.minimum(i + 1, n_steps - 1))

    h_new = pl.pallas_call(
        functools.partial(_combine_kernel, n_steps=n_steps),
        grid=(n_steps,),
        in_specs=[pl.BlockSpec((None, 1, 2 * TM), lambda i: (tile_of(i), 0, 0), memory_space=pltpu.SMEM),
                  pl.BlockSpec((None, 1, 2 * TM), lambda i: (next_tile_of(i), 0, 0), memory_space=pltpu.SMEM),
                  pl.BlockSpec(memory_space=pl.ANY),
                  pl.BlockSpec((TM, D), lambda i: (tile_of(i), 0)),
                  pl.BlockSpec((None, None, 8, D),
                               lambda i: (tile_of(i) // tiles_b, jnp.minimum(tile_of(i) % tiles_b, 1), 0, 0)),
                  pl.BlockSpec((TM, LANES), lambda i: (tile_of(i), 0))],
        out_specs=pl.BlockSpec((TM, D), lambda i: (i, 0)),
        out_shape=jax.ShapeDtypeStruct((n_steps * TM, D), F32),
        scratch_shapes=[pltpu.VMEM((2, 2 * TM, D), F32), pltpu.SemaphoreType.DMA((2,))],
        compiler_params=_cparams(("arbitrary",)),
        name="moe_combine",
    )(dest, dest, out, h.reshape(n_tok, D), mod, wts.reshape(n_tok, LANES))
    return h_new.reshape(batch, n_steps * TM // batch, D)


def kernel(x, c, ctx, c_ctx, ada_w, ada_b, norm1_g, norm2_g, da_w_in, da_w_out, da_q_g, da_k_g, da_lam_q1, da_lam_k1, da_lam_q2, da_lam_k2, da_sub_g, gm_w_in, gm_v_g, gm_w_s, gm_b_s, gm_w_out, ssm_w_in, ssm_conv_w, ssm_conv_b, ssm_dt_bias, ssm_a_log, ssm_d, ssm_out_g, ssm_w_out, mla_w_in, mla_q_norm_g, mla_kv_norm_g, mla_w_uq, mla_w_ukv, mla_q_g, mla_k_g, mla_w_out, moe_w_group, moe_b_group, moe_w_router, moe_b_router, moe_w1, moe_w3, moe_w2):
    batch, seq, _ = x.shape
    assert ctx.shape[1] == CTX and seq % TM == 0 and seq % GRID_W == 0
    h = jnp.concatenate([ctx, x], axis=1)

    rows = ((batch + 1 + 7) // 8) * 8
    cc = jnp.concatenate([c, c_ctx[None, :], jnp.zeros((rows - batch - 1, D), F32)], axis=0)
    ada = _ada_all(cc, ada_w, ada_b)
    mod_l = ada[:, :batch].reshape(DEPTH, batch, 6, D)
    mod_c = jnp.broadcast_to(ada[:, batch].reshape(DEPTH, 1, 6, D), (DEPTH, batch, 6, D))
    mod_all = jnp.stack([mod_c, mod_l], axis=2)
    mod_all = jnp.concatenate([mod_all, jnp.zeros((DEPTH, batch, 2, 2, D), F32)], axis=3)

    pad = LANES - MOE_EXPERTS - MOE_GROUPS
    for i in range(DEPTH):
        kind = i % 4
        mod = mod_all[i]
        n1g = norm1_g[i].reshape(1, D)
        n2g = norm2_g[i].reshape(1, D)
        wr = jnp.concatenate([moe_w_router[i], moe_w_group[i], jnp.zeros((D, pad), F32)], axis=1).astype(BF)
        br = jnp.concatenate([moe_b_router[i], moe_b_group[i], jnp.zeros((pad,), F32)]).reshape(1, LANES)
        if kind == 0:
            a = _diff_attention(h, mod, n1g, da_w_in[0], da_q_g[0], da_k_g[0], da_lam_q1[0], da_lam_k1[0],
                                da_lam_q2[0], da_lam_k2[0], da_sub_g[0], i)
            h, xn, ids, wts, cnt =_outproj(a, da_w_out[0].astype(BF), h, mod, n2g, wr, br)
        elif kind == 1:
            h, xn, ids, wts, cnt =_gmlp_layer(h, mod, n1g, gm_w_in[0], gm_v_g[0], gm_w_s[0], gm_b_s[0], gm_w_out[0],
                                          n2g, wr, br)
        elif kind == 2:
            h, xn, ids, wts, cnt =_mamba_layer(h, mod, n1g, ssm_w_in[0], ssm_conv_w[0], ssm_conv_b[0], ssm_dt_bias[0],
                                           ssm_a_log[0], ssm_d[0], ssm_out_g[0], ssm_w_out[0], n2g, wr, br)
        else:
            a = _mla_attention(h, mod, n1g, mla_w_in[0], mla_q_norm_g[0], mla_kv_norm_g[0], mla_w_uq[0],
                               mla_w_ukv[0], mla_q_g[0], mla_k_g[0])
            h, xn, ids, wts, cnt =_outproj(a, mla_w_out[0].astype(BF), h, mod, n2g, wr, br)
        h = _moe(h, mod, xn, ids, wts, cnt, moe_w1, moe_w3, moe_w2, i, latent_only=(i == DEPTH - 1))
    return h
```

```python
import functools
import math

import numpy as np
import jax
import jax.numpy as jnp
from jax import lax
from jax.experimental import pallas as pl
from jax.experimental.pallas import tpu as pltpu

F32 = jnp.float32
BF = jnp.bfloat16
HIGHEST = lax.Precision.HIGHEST

D = 1024
CTX = 256
GRID_W = 64
EPS = 1e-6
ROPE_BASE = 10000.0
DEPTH = 4
LANES = 128
TM = 256

DA_HEADS = 8
DA_HEAD_DIM = 64

GM_CHUNK = 128
GM_GROUPS = 8

SSM_INNER = 2048
SSM_HEADS = 32
SSM_GROUPS = 4
SSM_STATE = 128
SSM_CONV = 5
SSM_CHUNK = 128
SSM_CONV_DIM = SSM_INNER + 2 * SSM_GROUPS * SSM_STATE

MLA_HEADS = 16
MLA_Q_RANK = 384
MLA_KV_RANK = 256
MLA_NOPE = 64
MLA_ROPE = 32
MLA_V = 64
MLA_QK = MLA_NOPE + MLA_ROPE

MOE_GROUPS = 4
MOE_PER_GROUP = 8
MOE_EXPERTS = 32
MOE_FF = 512
MOE_TM = 256

VMEM_LIMIT = 56 * 1024 * 1024


def _cparams(sem):
    return pltpu.CompilerParams(dimension_semantics=sem, vmem_limit_bytes=VMEM_LIMIT)


def _full_spec(arr):
    nd = arr.ndim
    return pl.BlockSpec(arr.shape, lambda *_: (0,) * nd)


def _tok_spec(width, col=0):
    return pl.BlockSpec((None, TM, width), lambda b, t: (b, t, col))


def _mod_spec():
    return pl.BlockSpec((None, None, 8, D), lambda b, t: (b, jnp.minimum(t, 1), 0, 0))


def _silu(x):
    return x * (1.0 / (1.0 + jnp.exp(-x)))


def _norm_mod(h, g, scale, shift):
    ms = jnp.mean(h * h, axis=-1, keepdims=True)
    y = h * lax.rsqrt(ms + EPS) * g
    return y * (1.0 + scale) + shift


def _ada_kernel(c_ref, w_ref, b_ref, o_ref):
    a = _silu(c_ref[...]).astype(BF)
    o_ref[...] = jnp.dot(a, w_ref[...].astype(BF), preferred_element_type=F32) + b_ref[...]


def _ada_all(cc, ada_w, ada_b):
    rows = cc.shape[0]
    tn = 1536
    return pl.pallas_call(
        _ada_kernel,
        grid=(DEPTH, 6 * D // tn),
        in_specs=[pl.BlockSpec((rows, D), lambda l, j: (0, 0)),
                  pl.BlockSpec((None, D, tn), lambda l, j: (l, 0, j)),
                  pl.BlockSpec((None, 1, tn), lambda l, j: (l, 0, j))],
        out_specs=pl.BlockSpec((None, rows, tn), lambda l, j: (l, 0, j)),
        out_shape=jax.ShapeDtypeStruct((DEPTH, rows, 6 * D), F32),
        compiler_params=_cparams(("arbitrary", "arbitrary")),
        name="ada_mod",
    )(cc, ada_w, ada_b.reshape(DEPTH, 1, 6 * D))


def _residual_and_route(delta, h_ref, mod, n2g_ref, wr_ref, br_ref, outs):
    ho_ref, xo_ref, id_ref, wt_ref, cnt_ref = outs
    hn = h_ref[...] + mod[2:3] * delta
    ho_ref[...] = hn
    x2 = _norm_mod(hn, n2g_ref[...], mod[4:5], mod[3:4])
    xo_ref[...] = x2
    lg = jnp.dot(x2.astype(BF), wr_ref[...], preferred_element_type=F32) + br_ref[...]
    lane = lax.broadcasted_iota(jnp.int32, lg.shape, 1)
    neg = jnp.float32(-jnp.inf)
    big = jnp.int32(1 << 20)
    is_g = jnp.logical_and(lane >= MOE_EXPERTS, lane < MOE_EXPERTS + MOE_GROUPS)
    gl = jnp.where(is_g, lg, neg)
    gmax = jnp.max(gl, axis=-1, keepdims=True)
    gidx = jnp.min(jnp.where(gl == gmax, lane, big), axis=-1, keepdims=True) - MOE_EXPERTS
    pg = 1.0 / jnp.sum(jnp.where(is_g, jnp.exp(gl - gmax), 0.0), axis=-1, keepdims=True)
    lo = gidx * MOE_PER_GROUP
    in_grp = jnp.logical_and(lane >= lo, lane < lo + MOE_PER_GROUP)
    el = jnp.where(in_grp, lg, neg)
    e1 = jnp.max(el, axis=-1, keepdims=True)
    i1 = jnp.min(jnp.where(el == e1, lane, big), axis=-1, keepdims=True)
    el2 = jnp.where(lane == i1, neg, el)
    e2 = jnp.max(el2, axis=-1, keepdims=True)
    i2 = jnp.min(jnp.where(el2 == e2, lane, big), axis=-1, keepdims=True)
    t = jnp.exp(e2 - e1)
    w1 = pg / (1.0 + t)
    w2 = pg * t / (1.0 + t)
    wt_ref[...] = jnp.where(lane == 0, w1, jnp.where(lane == 1, w2, 0.0))
    sel1 = lane == i1
    sel2 = lane == i2
    chosen = jnp.where(jnp.logical_or(sel1, sel2), 1.0, 0.0)
    cnt_ref[...] = jnp.sum(chosen, axis=0, keepdims=True)
    rows = lg.shape[0]
    r = lax.broadcasted_iota(jnp.int32, (rows, rows), 0)
    c = lax.broadcasted_iota(jnp.int32, (rows, rows), 1)
    earlier = jnp.where(c < r, 1.0, 0.0).astype(BF)
    before = jnp.dot(earlier, chosen.astype(BF), preferred_element_type=F32)
    rank1 = jnp.sum(jnp.where(sel1, before, 0.0), axis=-1, keepdims=True).astype(jnp.int32)
    rank2 = jnp.sum(jnp.where(sel2, before, 0.0), axis=-1, keepdims=True).astype(jnp.int32)
    id_ref[...] = jnp.where(lane == 0, i1, jnp.where(lane == 1, i2,
                            jnp.where(lane == 2, rank1, jnp.where(lane == 3, rank2, 0))))


def _tail_in_specs(n2g, wr, br):
    return [_full_spec(n2g), _full_spec(wr), _full_spec(br)]


def _tail_out(batch, t_all):
    specs = [_tok_spec(D), _tok_spec(D), _tok_spec(LANES), _tok_spec(LANES),
             pl.BlockSpec((None, None, 1, LANES), lambda b, t: (b, t, 0, 0))]
    shapes = [jax.ShapeDtypeStruct((batch, t_all, D), F32),
              jax.ShapeDtypeStruct((batch, t_all, D), F32),
              jax.ShapeDtypeStruct((batch, t_all, LANES), jnp.int32),
              jax.ShapeDtypeStruct((batch, t_all, LANES), F32),
              jax.ShapeDtypeStruct((batch, t_all // TM, 1, LANES), F32)]
    return specs, shapes


def _outproj_kernel(a_ref, w_ref, h_ref, mod_ref, n2g_ref, wr_ref, br_ref, *outs):
    delta = jnp.dot(a_ref[...], w_ref[...], preferred_element_type=F32)
    _residual_and_route(delta, h_ref, mod_ref[...], n2g_ref, wr_ref, br_ref, outs)


def _outproj(a, w, h, mod, n2g, wr, br):
    batch, t_all, k = a.shape
    out_specs, out_shapes = _tail_out(batch, t_all)
    return pl.pallas_call(
        _outproj_kernel,
        grid=(batch, t_all // TM),
        in_specs=[_tok_spec(k), _full_spec(w), _tok_spec(D), _mod_spec()] + _tail_in_specs(n2g, wr, br),
        out_specs=out_specs,
        out_shape=out_shapes,
        compiler_params=_cparams(("parallel", "parallel")),
        name="outproj_route",
    )(a, w, h, mod, n2g, wr, br)


def _rope_tables(t_lat, rot_dim):
    rows = t_lat // GRID_W
    row = np.repeat(np.arange(rows, dtype=np.float32), GRID_W)
    col = np.tile(np.arange(GRID_W, dtype=np.float32), rows)
    n_freq = rot_dim // 4
    inv_freq = jnp.asarray(ROPE_BASE, F32) ** (-jnp.arange(n_freq, dtype=F32) / n_freq)
    ang = jnp.concatenate([jnp.asarray(row)[:, None] * inv_freq, jnp.asarray(col)[:, None] * inv_freq], axis=-1)
    return jnp.cos(ang), jnp.sin(ang)


def _with_ctx_identity(cos_l, sin_l):
    cos = jnp.concatenate([jnp.ones((CTX, LANES), F32), cos_l], axis=0)
    sin = jnp.concatenate([jnp.zeros((CTX, LANES), F32), sin_l], axis=0)
    return cos, sin


def _da_perm():
    perm = np.zeros(2 * DA_HEADS * DA_HEAD_DIM, np.int32)
    for h in range(DA_HEADS):
        for m in range(2):
            for i in range(DA_HEAD_DIM):
                new = h * LANES + (i // 32) * 64 + m * 32 + (i % 32)
                perm[new] = (2 * h + m) * DA_HEAD_DIM + i
    return perm


def _da_lane_of_dim():
    lane = np.arange(LANES)
    return (lane // 64) * 32 + lane % 32


def _da_proj_kernel(h_ref, mod_ref, n1g_ref, w_ref, gq_ref, gk_ref, cos_ref, sin_ref, gavg_ref,
                    q_ref, k_ref, v_ref):
    mod = mod_ref[...]
    xn = _norm_mod(h_ref[...], n1g_ref[...], mod[1:2], mod[0:1]).astype(BF)
    cos = cos_ref[...]
    sin = sin_ref[...]
    gavg = gavg_ref[...]
    nq = DA_HEADS * LANES
    scale = DA_HEAD_DIM ** -0.5 * LOG2E
    for dst, off, g_ref, mult in ((q_ref, 0, gq_ref, scale), (k_ref, nq, gk_ref, 1.0)):
        y_all = jnp.dot(xn, w_ref[:, off:off + nq], preferred_element_type=F32)
        for jj in range(DA_HEADS // 2):
            yy = y_all[:, jj * 2 * LANES:(jj + 1) * 2 * LANES]
            ms2 = jnp.dot((yy * yy).astype(BF), gavg, preferred_element_type=F32)
            for j in (2 * jj, 2 * jj + 1):
                y = y_all[:, j * LANES:(j + 1) * LANES]
                ms = ms2[:, (j % 2) * LANES:(j % 2 + 1) * LANES]
                y = y * lax.rsqrt(ms + EPS) * g_ref[...]
                y = y * cos + pltpu.roll(y, 64, 1) * sin
                dst[:, j * LANES:(j + 1) * LANES] = (y * mult).astype(BF)
    v_ref[...] = jnp.dot(xn, w_ref[:, 2 * nq:3 * nq], preferred_element_type=F32).astype(BF)


LOG2E = math.log2(math.e)


def _softmax_pv(q, k, v):
    s = lax.dot_general(q, k, (((1,), (1,)), ((), ())), preferred_element_type=F32)
    e = jnp.exp2(s - jnp.max(s, axis=-1, keepdims=True))
    l = jnp.sum(e, axis=-1, keepdims=True)
    return jnp.dot(e.astype(BF), v, preferred_element_type=F32) / l


ATTN_TILES_PER_ITER = 4


def _for_each_latent_tile(tile_fn, t_all):
    n_lat = (t_all - CTX) // TM
    per = ATTN_TILES_PER_ITER if n_lat % ATTN_TILES_PER_ITER == 0 else 1

    def body(j, carry):
        for u in range(per):
            tile_fn(pl.multiple_of(CTX + (j * per + u) * TM, TM), t_all)
        return carry

    lax.fori_loop(0, n_lat // per, body, 0)


def _da_attn_kernel(lam_ref, q_ref, k_ref, v_ref, subg_ref, o_ref, *, lam_init):
    t_all = q_ref.shape[0]
    lane = lax.broadcasted_iota(jnp.int32, (TM, LANES), 1)
    comp0 = ((lane // 32) % 2) == 0
    zero = jnp.zeros((TM, LANES), BF)

    def tile(row0, n_keys):
        q = q_ref[pl.ds(row0, TM), :]
        k = k_ref[0:n_keys, :]
        v = v_ref[0:n_keys, :]
        o = _softmax_pv(jnp.where(comp0, q, zero), k, v) - lam_ref[0] * _softmax_pv(jnp.where(comp0, zero, q), k, v)
        ms = jnp.mean(o * o, axis=-1, keepdims=True)
        o = o * lax.rsqrt(ms + EPS) * subg_ref[...] * (1.0 - lam_init)
        o_ref[pl.ds(row0, TM), :] = o.astype(BF)

    tile(0, CTX)

    _for_each_latent_tile(tile, t_all)


def _diff_attention(h, mod, n1g, w_in, q_g, k_g, lam_q1, lam_k1, lam_q2, lam_k2, sub_g, layer_idx):
    batch, t_all, _ = h.shape
    nq = DA_HEADS * LANES
    perm = _da_perm()
    w = jnp.concatenate([w_in[:, :nq][:, perm], w_in[:, nq:2 * nq][:, perm], w_in[:, 2 * nq:]], axis=1).astype(BF)
    ldim = _da_lane_of_dim()
    gq = q_g[ldim].reshape(1, LANES)
    gk = k_g[ldim].reshape(1, LANES)
    cos, sin = _rope_tables(t_all - CTX, DA_HEAD_DIM)
    cos, sin = _with_ctx_identity(jnp.tile(cos, (1, 4)), jnp.concatenate([-sin, -sin, sin, sin], axis=1))
    unit = np.arange(2 * LANES) // LANES * 2 + (np.arange(2 * LANES) // 32) % 2
    gavg = jnp.asarray((unit[:, None] == unit[None, :]).astype(np.float32) / DA_HEAD_DIM).astype(BF)
    tab_spec = pl.BlockSpec((TM, LANES), lambda b, t: (t, 0))
    qkv_shape = jax.ShapeDtypeStruct((batch, t_all, nq), BF)
    q, k, v = pl.pallas_call(
        _da_proj_kernel,
        grid=(batch, t_all // TM),
        in_specs=[_tok_spec(D), _mod_spec(), _full_spec(n1g), _full_spec(w), _full_spec(gq), _full_spec(gk),
                  tab_spec, tab_spec, _full_spec(gavg)],
        out_specs=[_tok_spec(nq)] * 3,
        out_shape=[qkv_shape] * 3,
        compiler_params=_cparams(("parallel", "parallel")),
        name="da_proj",
    )(h, mod, n1g, w, gq, gk, cos, sin, gavg)

    lam_init = 0.8 - 0.6 * math.exp(-0.3 * layer_idx)
    lam = (jnp.exp(jnp.sum(lam_q1 * lam_k1)) - jnp.exp(jnp.sum(lam_q2 * lam_k2)) + lam_init).reshape(1)
    seq_spec = pl.BlockSpec((None, t_all, LANES), lambda b, hh: (b, 0, hh))
    subg = sub_g.reshape(1, LANES)
    return pl.pallas_call(
        functools.partial(_da_attn_kernel, lam_init=lam_init),
        grid=(batch, DA_HEADS),
        in_specs=[pl.BlockSpec(memory_space=pltpu.SMEM), seq_spec, seq_spec, seq_spec,
                  pl.BlockSpec((1, LANES), lambda b, hh: (0, 0))],
        out_specs=seq_spec,
        out_shape=jax.ShapeDtypeStruct((batch, t_all, nq), BF),
        compiler_params=_cparams(("parallel", "parallel")),
        name="da_attn",
    )(lam, q, k, v, subg)


def _gmlp_kernel(h_ref, mod_ref, n1g_ref, win_ref, vg_ref, ws_ref, bs_ref, wout_ref,
                 n2g_ref, wr_ref, br_ref, *outs):
    mod = mod_ref[...]
    xn = _norm_mod(h_ref[...], n1g_ref[...], mod[1:2], mod[0:1]).astype(BF)
    uv =jnp.dot(xn, win_ref[...], preferred_element_type=F32)
    uv = 0.5 * uv * (1.0 + lax.erf(uv * (2.0 ** -0.5)))
    u = uv[:, :D]
    v = uv[:, D:]
    v = (v * lax.rsqrt(jnp.mean(v * v, axis=-1, keepdims=True) + EPS) * vg_ref[...]).astype(BF)
    rows = []
    for c in range(TM // GM_CHUNK):
        cols = []
        for g in range(GM_GROUPS):
            vb = v[c * GM_CHUNK:(c + 1) * GM_CHUNK, g * LANES:(g + 1) * LANES]
            s = jnp.dot(ws_ref[g], vb, preferred_element_type=F32) + bs_ref[g]
            cols.append(s)
        rows.append(jnp.concatenate(cols, axis=1))
    s_all = jnp.concatenate(rows, axis=0)
    delta = jnp.dot((u * s_all).astype(BF), wout_ref[...], preferred_element_type=F32)
    _residual_and_route(delta, h_ref, mod, n2g_ref, wr_ref, br_ref, outs)


def _gmlp_layer(h, mod, n1g, w_in, v_g, w_s, b_s, w_out, n2g, wr, br):
    batch, t_all, _ = h.shape
    win = w_in.astype(BF)
    wout = w_out.astype(BF)
    ws = w_s.astype(BF)
    bs = jnp.broadcast_to(b_s[:, :, None], (GM_GROUPS, GM_CHUNK, LANES)).astype(F32)
    vg = v_g.reshape(1, D)
    out_specs, out_shapes = _tail_out(batch, t_all)
    return pl.pallas_call(
        _gmlp_kernel,
        grid=(batch, t_all // TM),
        in_specs=[_tok_spec(D), _mod_spec(), _full_spec(n1g), _full_spec(win), _full_spec(vg), _full_spec(ws),
                  _full_spec(bs), _full_spec(wout)] + _tail_in_specs(n2g, wr, br),
        out_specs=out_specs,
        out_shape=out_shapes,
        compiler_params=_cparams(("parallel", "parallel")),
        name="gmlp_layer",
    )(h, mod, n1g, win, vg, ws, bs, wout, n2g, wr, br)


SSM_DT_PAD = LANES


def _ssm_proj_kernel(h_ref, mod_ref, n1g_ref, w_ref, dtb_ref, z_ref, xbc_ref, dt_ref):
    mod = mod_ref[...]
    xn = _norm_mod(h_ref[...], n1g_ref[...], mod[1:2], mod[0:1]).astype(BF)
    z_ref[...] = jnp.dot(xn, w_ref[:, :SSM_INNER], preferred_element_type=F32).astype(BF)
    c0 = SSM_INNER
    for j in range(SSM_CONV_DIM // 1024):
        xbc_ref[:, j * 1024:(j + 1) * 1024] = jnp.dot(
            xn, w_ref[:, c0 + j * 1024:c0 + (j + 1) * 1024], preferred_element_type=F32).astype(BF)
    c1 = SSM_INNER + SSM_CONV_DIM
    raw = jnp.dot(xn, w_ref[:, c1:c1 + SSM_DT_PAD], preferred_element_type=F32) + dtb_ref[...]
    dt_ref[...] = jnp.maximum(raw, 0.0) + jnp.log1p(jnp.exp(-jnp.abs(raw)))


CONV_PAD = 8
CONV_W = 512


def _conv_kernel(x_ref, w_ref, b_ref, o_ref, pad_ref):
    t_all = x_ref.shape[0]
    t_lat = t_all - CTX
    lat0 = CTX + 2 * CONV_PAD
    zeros = jnp.zeros((CONV_PAD, CONV_W), F32)
    pad_ref[0:CONV_PAD, :] = zeros
    pad_ref[CONV_PAD + CTX:lat0, :] = jnp.zeros((CONV_PAD, CONV_W), F32)
    pad_ref[lat0 + t_lat:lat0 + t_lat + CONV_PAD, :] = zeros
    pad_ref[CONV_PAD:CONV_PAD + CTX, :] = x_ref[0:CTX, :].astype(F32)
    pad_ref[lat0:lat0 + t_lat, :] = x_ref[CTX:t_all, :].astype(F32)
    half = (SSM_CONV - 1) // 2
    padded = pad_ref[...]
    rows = padded.shape[0]
    taps = [padded if kk == half else pltpu.roll(padded, (half - kk) % rows, 0) for kk in range(SSM_CONV)]
    for start, n, dst in ((CONV_PAD, CTX, 0), (lat0, t_lat, CTX)):
        acc = jnp.zeros((n, CONV_W), F32) + b_ref[...]
        for kk in range(SSM_CONV):
            acc = acc + taps[kk][start:start + n, :] * w_ref[kk:kk + 1, :]
        o_ref[dst:dst + n, :] = _silu(acc).astype(BF)


def _ssd_kernel(dt_ref, a_ref, x_ref, b_ref, c_ref, y_ref, s_ref):
    direction = pl.program_id(1)
    step = pl.program_id(2)

    @pl.when(step == 0)
    def _():
        s_ref[...] = jnp.zeros_like(s_ref)

    q = SSM_CHUNK
    row = lax.broadcasted_iota(jnp.int32, (q, q), 0)
    col = lax.broadcasted_iota(jnp.int32, (q, q), 1)
    sign = 1 - 2 * direction
    tri_b = (col - row) * sign <= 0
    tri = tri_b.astype(F32)
    tri_t = ((row - col) * sign <= 0).astype(F32)
    eye = (row == col).astype(F32)
    lane_lo = col < 64

    dt_t = dt_ref[...]
    a_t = dt_t * a_ref[...]
    cs_rows = jnp.dot(a_t, tri_t, preferred_element_type=F32, precision=HIGHEST)
    tot = jnp.sum(a_t, axis=1, keepdims=True)
    heads_per_group = SSM_HEADS // SSM_GROUPS

    for g in range(SSM_GROUPS):
        bm = b_ref[:, g * SSM_STATE:(g + 1) * SSM_STATE]
        cm = c_ref[:, g * SSM_STATE:(g + 1) * SSM_STATE]
        cb = lax.dot_general(cm, bm, (((1,), (1,)), ((), ())), preferred_element_type=F32)
        bm_t = bm.astype(F32).T.astype(BF)
        for pp in range(heads_per_group // 2):
            p = g * (heads_per_group // 2) + pp
            xp = x_ref[:, p * LANES:(p + 1) * LANES].astype(F32)
            cs_col, dt_col, mats = [], [], []
            for hh in range(2):
                e = 2 * p + hh
                cc = jnp.sum(tri * a_t[e:e + 1, :], axis=1, keepdims=True)
                cs_col.append(cc)
                dt_col.append(jnp.sum(eye * dt_t[e:e + 1, :], axis=1, keepdims=True))
                decay = jnp.exp(jnp.where(tri_b, cc - cs_rows[e:e + 1, :], -jnp.inf))
                mats.append((cb * decay).astype(BF))
            xdt = xp * jnp.where(lane_lo, dt_col[0], dt_col[1])
            xb = xdt.astype(BF)
            zb = jnp.zeros_like(xb)
            y_diag = jnp.dot(jnp.concatenate(mats, axis=1),
                             jnp.concatenate([jnp.where(lane_lo, xb, zb), jnp.where(lane_lo, zb, xb)], axis=0),
                             preferred_element_type=F32)
            s_prev = s_ref[p]
            y_off = jnp.dot(cm, s_prev.astype(BF), preferred_element_type=F32)
            y_off = y_off * jnp.exp(jnp.where(lane_lo, cs_col[0], cs_col[1]))
            y_ref[:, p * LANES:(p + 1) * LANES] = (y_diag + y_off).astype(BF)
            t0 = tot[2 * p:2 * p + 1, :]
            t1 = tot[2 * p + 1:2 * p + 2, :]
            to_end = jnp.exp(jnp.where(lane_lo, t0 - cs_col[0], t1 - cs_col[1]))
            upd = jnp.dot(bm_t, (xdt * to_end).astype(BF), preferred_element_type=F32)
            s_ref[p] = s_prev * jnp.exp(jnp.where(lane_lo[0:1, :], t0, t1)) + upd


def _ssm_finish_kernel(yf_ref, yb_ref, xs_ref, z_ref, d_ref, og_ref, w_ref, h_ref, mod_ref,
                       n2g_ref, wr_ref, br_ref, *outs):
    y = yf_ref[...].astype(F32) + yb_ref[...].astype(F32) + d_ref[...] * xs_ref[...].astype(F32)
    y = y * _silu(z_ref[...].astype(F32))
    y = y * lax.rsqrt(jnp.mean(y * y, axis=-1, keepdims=True) + EPS) * og_ref[...]
    delta = jnp.dot(y.astype(BF), w_ref[...], preferred_element_type=F32)
    _residual_and_route(delta, h_ref, mod_ref[...], n2g_ref, wr_ref, br_ref, outs)


def _mamba_layer(h, mod, n1g, w_in, conv_w, conv_b, dt_bias, a_log, d_skip, out_g, w_out, n2g, wr, br):
    batch, t_all, _ = h.shape
    n_dt = 2 * SSM_HEADS
    c1 = SSM_INNER + SSM_CONV_DIM
    w = jnp.concatenate([w_in, jnp.zeros((D, SSM_DT_PAD - n_dt), F32)], axis=1).astype(BF)
    dtb = jnp.concatenate([dt_bias.reshape(n_dt), jnp.zeros((SSM_DT_PAD - n_dt,), F32)]).reshape(1, SSM_DT_PAD)
    z, xbc, dt = pl.pallas_call(
        _ssm_proj_kernel,
        grid=(batch, t_all // TM),
        in_specs=[_tok_spec(D), _mod_spec(), _full_spec(n1g), _full_spec(w), _full_spec(dtb)],
        out_specs=[_tok_spec(SSM_INNER), _tok_spec(SSM_CONV_DIM), _tok_spec(SSM_DT_PAD)],
        out_shape=[jax.ShapeDtypeStruct((batch, t_all, SSM_INNER), BF),
                   jax.ShapeDtypeStruct((batch, t_all, SSM_CONV_DIM), BF),
                   jax.ShapeDtypeStruct((batch, t_all, SSM_DT_PAD), F32)],
        compiler_params=_cparams(("parallel", "parallel")),
        name="ssm_proj",
    )(h, mod, n1g, w, dtb)
    assert c1 + n_dt == w_in.shape[1]

    cw = jnp.concatenate([conv_w, jnp.zeros((8 - SSM_CONV, SSM_CONV_DIM), F32)], axis=0)
    cbias = conv_b.reshape(1, SSM_CONV_DIM)
    xbc = pl.pallas_call(
        _conv_kernel,
        grid=(batch, SSM_CONV_DIM // CONV_W),
        in_specs=[pl.BlockSpec((None, t_all, CONV_W), lambda b, j: (b, 0, j)),
                  pl.BlockSpec((8, CONV_W), lambda b, j: (0, j)),
                  pl.BlockSpec((1, CONV_W), lambda b, j: (0, j))],
        out_specs=pl.BlockSpec((None, t_all, CONV_W), lambda b, j: (b, 0, j)),
        out_shape=jax.ShapeDtypeStruct((batch, t_all, SSM_CONV_DIM), BF),
        scratch_shapes=[pltpu.VMEM((t_all + 3 * CONV_PAD, CONV_W), F32)],
        compiler_params=_cparams(("parallel", "parallel")),
        name="ssm_conv",
    )(xbc, cw, cbias)

    n_chunks = t_all // SSM_CHUNK
    ctx_chunks = CTX // SSM_CHUNK
    dt_t = jnp.swapaxes(dt[:, :, :n_dt], 1, 2)
    a_neg = jnp.broadcast_to(-jnp.exp(a_log.astype(F32)).reshape(2, SSM_HEADS, 1), (2, SSM_HEADS, SSM_CHUNK))

    def chunk_of(dr, s):
        rev = jnp.where(s < ctx_chunks, ctx_chunks - 1 - s, n_chunks - 1 + ctx_chunks - s)
        return jnp.where(dr == 0, s, rev)

    gs = SSM_GROUPS * SSM_STATE
    y = pl.pallas_call(
        _ssd_kernel,
        grid=(batch, 2, n_chunks),
        in_specs=[pl.BlockSpec((None, SSM_HEADS, SSM_CHUNK), lambda b, dr, s: (b, dr, chunk_of(dr, s))),
                  pl.BlockSpec((None, SSM_HEADS, SSM_CHUNK), lambda b, dr, s: (dr, 0, 0)),
                  pl.BlockSpec((None, SSM_CHUNK, SSM_INNER), lambda b, dr, s: (b, chunk_of(dr, s), 0)),
                  pl.BlockSpec((None, SSM_CHUNK, gs), lambda b, dr, s: (b, chunk_of(dr, s), SSM_INNER // gs)),
                  pl.BlockSpec((None, SSM_CHUNK, gs), lambda b, dr, s: (b, chunk_of(dr, s), SSM_INNER // gs + 1))],
        out_specs=pl.BlockSpec((None, None, SSM_CHUNK, SSM_INNER), lambda b, dr, s: (dr, b, chunk_of(dr, s), 0)),
        out_shape=jax.ShapeDtypeStruct((2, batch, t_all, SSM_INNER), BF),
        scratch_shapes=[pltpu.VMEM((SSM_HEADS // 2, SSM_STATE, LANES), F32)],
        compiler_params=_cparams(("parallel", "parallel", "arbitrary")),
        name="ssd_scan",
    )(dt_t, a_neg, xbc, xbc, xbc)

    dvec = jnp.repeat(d_skip, SSM_INNER // SSM_HEADS).reshape(1, SSM_INNER)
    og = out_g.reshape(1, SSM_INNER)
    wo = w_out.astype(BF)
    out_specs, out_shapes = _tail_out(batch, t_all)
    y_spec = lambda dr: pl.BlockSpec((None, None, TM, SSM_INNER), lambda b, t: (dr, b, t, 0))
    return pl.pallas_call(
        _ssm_finish_kernel,
        grid=(batch, t_all // TM),
        in_specs=[y_spec(0), y_spec(1), _tok_spec(SSM_INNER), _tok_spec(SSM_INNER), _full_spec(dvec),
                  _full_spec(og), _full_spec(wo), _tok_spec(D), _mod_spec()] + _tail_in_specs(n2g, wr, br),
        out_specs=out_specs,
        out_shape=out_shapes,
        compiler_params=_cparams(("parallel", "parallel")),
        name="ssm_finish",
    )(y, y, xbc, z, dvec, og, wo, h, mod, n2g, wr, br)


def _mla_lane_src():
    src = np.full(LANES, -1, np.int32)
    src[0:16] = MLA_NOPE + np.arange(16)
    src[16:64] = np.arange(48)
    src[64:80] = MLA_NOPE + 16 + np.arange(16)
    src[80:96] = 48 + np.arange(16)
    return src


def _mla_proj_kernel(h_ref, mod_ref, n1g_ref, win_ref, qng_ref, kvng_ref, wuq_ref, wuk_ref, wuv_ref,
                     gq_ref, gk_ref, cos_ref, sin_ref, avg_ref, q_ref, k_ref, v_ref):
    mod = mod_ref[...]
    xn = _norm_mod(h_ref[...], n1g_ref[...], mod[1:2], mod[0:1]).astype(BF)
    lat = jnp.dot(xn, win_ref[...], preferred_element_type=F32)
    cq = lat[:, :MLA_Q_RANK]
    ckv = lat[:, MLA_Q_RANK:MLA_Q_RANK + MLA_KV_RANK]
    kpe = lat[:, MLA_Q_RANK + MLA_KV_RANK:]
    cq = (cq * lax.rsqrt(jnp.mean(cq * cq, axis=-1, keepdims=True) + EPS) * qng_ref[...]).astype(BF)
    ckv = (ckv * lax.rsqrt(jnp.mean(ckv * ckv, axis=-1, keepdims=True) + EPS) * kvng_ref[...]).astype(BF)
    cos = cos_ref[...]
    sin = sin_ref[...]
    avg = avg_ref[...]
    scale = MLA_QK ** -0.5 * LOG2E
    qa = jnp.dot(cq, wuq_ref[...], preferred_element_type=F32)
    ka = jnp.dot(ckv, wuk_ref[...], preferred_element_type=F32)
    kpe2 = jnp.concatenate([kpe, kpe], axis=1)
    for jj in range(MLA_HEADS // 2):
        for dst, src, add, g_ref, mult in ((q_ref, qa, None, gq_ref, scale), (k_ref, ka, kpe2, gk_ref, 1.0)):
            yy = src[:, jj * 2 * LANES:(jj + 1) * 2 * LANES]
            if add is not None:
                yy = yy + add
            ms2 = jnp.dot((yy * yy).astype(BF), avg, preferred_element_type=F32) * (1.0 / MLA_QK)
            for hh in range(2):
                j = 2 * jj + hh
                y = yy[:, hh * LANES:(hh + 1) * LANES]
                y = y * lax.rsqrt(ms2[:, hh * LANES:(hh + 1) * LANES] + EPS) * g_ref[...]
                y = y * cos + pltpu.roll(y, 64, 1) * sin
                dst[:, j * LANES:(j + 1) * LANES] = (y * mult).astype(BF)
    v_ref[...] = jnp.dot(ckv, wuv_ref[...], preferred_element_type=F32).astype(BF)


def _mla_attn_kernel(q_ref, k_ref, v_ref, o_ref):
    t_all = q_ref.shape[0]
    lane = lax.broadcasted_iota(jnp.int32, (TM, LANES), 1)

    def tile(row0, n_keys):
        v = v_ref[0:n_keys, :]
        outs = [_softmax_pv(q_ref[pl.ds(row0, TM), hh * LANES:(hh + 1) * LANES],
                            k_ref[0:n_keys, hh * LANES:(hh + 1) * LANES], v)
                for hh in range(2)]
        o_ref[pl.ds(row0, TM), :] = jnp.where(lane < MLA_V, outs[0], outs[1]).astype(BF)

    tile(0, CTX)

    _for_each_latent_tile(tile, t_all)


def _mla_attention(h, mod, n1g, w_in, q_norm_g, kv_norm_g, w_uq, w_ukv, q_g, k_g):
    batch, t_all, _ = h.shape
    src = _mla_lane_src()
    used = src >= 0
    srcc = np.where(used, src, 0)
    nh = MLA_HEADS * LANES
    kpe_cols = np.where(srcc >= MLA_NOPE, MLA_Q_RANK + MLA_KV_RANK + srcc - MLA_NOPE, 0)
    kpe_used = np.logical_and(used, src >= MLA_NOPE)
    w_kpe = jnp.where(jnp.asarray(kpe_used)[None, :], w_in[:, kpe_cols], 0.0)
    win = jnp.concatenate([w_in[:, :MLA_Q_RANK + MLA_KV_RANK], w_kpe], axis=1).astype(BF)
    q_cols = (np.arange(MLA_HEADS)[:, None] * MLA_QK + srcc[None, :]).reshape(-1)
    q_mask = jnp.asarray(np.tile(used, MLA_HEADS))[None, :]
    wuq = jnp.where(q_mask, w_uq[:, q_cols], 0.0).astype(BF)
    nope_used = np.logical_and(used, src < MLA_NOPE)
    k_cols = (np.arange(MLA_HEADS)[:, None] * (MLA_NOPE + MLA_V) + np.where(nope_used, srcc, 0)[None, :]).reshape(-1)
    k_mask = jnp.asarray(np.tile(nope_used, MLA_HEADS))[None, :]
    wuk = jnp.where(k_mask, w_ukv[:, k_cols], 0.0).astype(BF)
    v_cols = (np.arange(MLA_HEADS)[:, None] * (MLA_NOPE + MLA_V) + MLA_NOPE + np.arange(MLA_V)[None, :]).reshape(-1)
    wuv = w_ukv[:, v_cols].astype(BF)
    gq = jnp.where(jnp.asarray(used), q_g[srcc], 0.0).reshape(1, LANES)
    gk = jnp.where(jnp.asarray(used), k_g[srcc], 0.0).reshape(1, LANES)
    cos16, sin16 = _rope_tables(t_all - CTX, MLA_ROPE)
    t_lat = t_all - CTX
    cos_l = jnp.ones((t_lat, LANES), F32).at[:, 0:16].set(cos16).at[:, 64:80].set(cos16)
    sin_l = jnp.zeros((t_lat, LANES), F32).at[:, 0:16].set(-sin16).at[:, 64:80].set(sin16)
    cos, sin = _with_ctx_identity(cos_l, sin_l)
    head_of_lane = np.arange(2 * LANES) // LANES
    avg = jnp.asarray((head_of_lane[:, None] == head_of_lane[None, :]).astype(np.float32)).astype(BF)
    qng = q_norm_g.reshape(1, MLA_Q_RANK)
    kvng = kv_norm_g.reshape(1, MLA_KV_RANK)
    tab_spec = pl.BlockSpec((TM, LANES), lambda b, t: (t, 0))
    q, k, v = pl.pallas_call(
        _mla_proj_kernel,
        grid=(batch, t_all // TM),
        in_specs=[_tok_spec(D), _mod_spec(), _full_spec(n1g), _full_spec(win), _full_spec(qng), _full_spec(kvng),
                  _full_spec(wuq), _full_spec(wuk), _full_spec(wuv), _full_spec(gq), _full_spec(gk),
                  tab_spec, tab_spec, _full_spec(avg)],
        out_specs=[_tok_spec(nh), _tok_spec(nh), _tok_spec(MLA_HEADS * MLA_V)],
        out_shape=[jax.ShapeDtypeStruct((batch, t_all, nh), BF), jax.ShapeDtypeStruct((batch, t_all, nh), BF),
                   jax.ShapeDtypeStruct((batch, t_all, MLA_HEADS * MLA_V), BF)],
        compiler_params=_cparams(("parallel", "parallel")),
        name="mla_proj",
    )(h, mod, n1g, win, qng, kvng, wuq, wuk, wuv, gq, gk, cos, sin, avg)

    return pl.pallas_call(
        _mla_attn_kernel,
        grid=(batch, MLA_HEADS // 2),
        in_specs=[pl.BlockSpec((None, t_all, 2 * LANES), lambda b, p: (b, 0, p)),
                  pl.BlockSpec((None, t_all, 2 * LANES), lambda b, p: (b, 0, p)),
                  pl.BlockSpec((None, t_all, LANES), lambda b, p: (b, 0, p))],
        out_specs=pl.BlockSpec((None, t_all, LANES), lambda b, p: (b, 0, p)),
        out_shape=jax.ShapeDtypeStruct((batch, t_all, MLA_HEADS * MLA_V), BF),
        compiler_params=_cparams(("parallel", "parallel")),
        name="mla_attn",
    )(q, k, v)


ROWS_PER_ISSUE = 8


def _row_copy(src, src_row, dst, dst_row, sem):
    return pltpu.make_async_copy(src.at[pl.ds(src_row, 1), :], dst.at[pl.ds(dst_row, 1), :], sem)


def _dispatch_kernel(ztile_ref, dest_ref, x_ref, xs_hbm, zbuf, zsem, sem):
    def zero_copy(e):
        start = pl.multiple_of(ztile_ref[e], MOE_TM)
        return pltpu.make_async_copy(zbuf, xs_hbm.at[pl.ds(start, MOE_TM), :], zsem)

    @pl.when(pl.program_id(0) == 0)
    def _():
        zbuf[...] = jnp.zeros_like(zbuf)
        for e in range(2 * MOE_EXPERTS):
            @pl.when(ztile_ref[e] >= 0)
            def _():
                zero_copy(e).start()
        for e in range(2 * MOE_EXPERTS):
            @pl.when(ztile_ref[e] >= 0)
            def _():
                zero_copy(e).wait()

    def issue(blk, carry):
        for u in range(ROWS_PER_ISSUE):
            r = blk * ROWS_PER_ISSUE + u
            for k in range(2):
                _row_copy(x_ref, r, xs_hbm, dest_ref[0, 2 * r + k], sem).start(priority=k)
        return carry

    lax.fori_loop(0, TM // ROWS_PER_ISSUE, issue, 0)
    for k in range(2):
        pltpu.make_async_copy(x_ref, xs_hbm.at[pl.ds(0, TM), :], sem).wait()


def _moe_kernel(te_ref, nt_ref, x_ref, w1_ref, w3_ref, w2_ref, o_ref, w1b, w3b, w2b):
    i = pl.program_id(0)
    prev = te_ref[jnp.maximum(i - 1, 0)]

    @pl.when(jnp.logical_or(i == 0, te_ref[i] != prev))
    def _():
        w1b[...] = w1_ref[...].astype(BF)
        w3b[...] = w3_ref[...].astype(BF)
        w2b[...] = w2_ref[...].astype(BF)

    @pl.when(i < nt_ref[0])
    def _():
        x = x_ref[...].astype(BF)
        h1 = jnp.dot(x, w1b[...], preferred_element_type=F32)
        h3 = jnp.dot(x, w3b[...], preferred_element_type=F32)
        hid = (_silu(h1) * h3).astype(BF)
        o_ref[...] = jnp.dot(hid, w2b[...], preferred_element_type=F32)

    @pl.when(i >= nt_ref[0])
    def _():
        o_ref[...] = jnp.zeros_like(o_ref)


def _combine_kernel(dest_ref, dnext_ref, out_hbm, h_ref, mod_ref, wt_ref, ho_ref, buf, sem, *, n_steps):
    i = pl.program_id(0)
    slot = i % 2

    def fetch(d_ref, s):
        def issue(blk, carry):
            for u in range(ROWS_PER_ISSUE):
                r = blk * ROWS_PER_ISSUE + u
                for k in range(2):
                    _row_copy(out_hbm, d_ref[0, 2 * r + k], buf.at[s], k * TM + r, sem.at[s]).start(priority=k)
            return carry

        lax.fori_loop(0, TM // ROWS_PER_ISSUE, issue, 0)

    @pl.when(i == 0)
    def _():
        fetch(dest_ref, 0)

    @pl.when(i + 1 < n_steps)
    def _():
        fetch(dnext_ref, 1 - slot)

    pltpu.make_async_copy(out_hbm.at[pl.ds(0, 2 * TM), :], buf.at[slot], sem.at[slot]).wait()
    w = wt_ref[...]
    y = buf[slot, 0:TM, :] * w[:, 0:1] + buf[slot, TM:2 * TM, :] * w[:, 1:2]
    ho_ref[...] = h_ref[...] + mod_ref[...][5:6] * y


def _moe(h, mod, xn, ids, wts, cnt, w1, w3, w2, layer, latent_only):
    batch, t_all, _ = xn.shape
    n_tok = batch * t_all
    tiles_b = t_all // TM
    n_tt = n_tok // TM
    n_tiles = (2 * n_tok + MOE_EXPERTS * (MOE_TM - 1) + MOE_TM - 1) // MOE_TM

    cnt = cnt.reshape(n_tt, LANES)[:, :MOE_EXPERTS].astype(jnp.int32)
    tiles_e = (jnp.sum(cnt, axis=0) + MOE_TM - 1) // MOE_TM
    tile_end = jnp.cumsum(tiles_e)
    tile_start = tile_end - tiles_e
    before = jnp.cumsum(cnt, axis=0) - cnt
    base = tile_start[None, :] * MOE_TM + before
    n_valid = tile_end[-1]
    tidx = jnp.arange(n_tiles, dtype=jnp.int32)
    te = jnp.minimum(jnp.sum((tile_end[None, :] <= tidx[:, None]).astype(jnp.int32), axis=1), MOE_EXPERTS - 1)
    te = jnp.where(tidx < n_valid, te, jnp.max(jnp.where(tiles_e > 0, jnp.arange(MOE_EXPERTS), 0))).astype(jnp.int32)
    assert n_tiles * MOE_TM - 2 * n_tok <= MOE_EXPERTS * MOE_TM
    tail = n_valid + jnp.arange(MOE_EXPERTS, dtype=jnp.int32)
    ztile = jnp.concatenate([jnp.where(tiles_e > 0, (tile_end - 1) * MOE_TM, -1),
                             jnp.where(tail < n_tiles, tail * MOE_TM, -1)]).astype(jnp.int32)

    picks = ids.reshape(n_tt, TM, LANES)
    expert = picks[:, :, 0:2].reshape(n_tt, 1, 2 * TM)
    rank = picks[:, :, 2:4].reshape(n_tt, 1, 2 * TM)
    is_pick = expert == jnp.arange(MOE_EXPERTS, dtype=jnp.int32)[None, :, None]
    dest = jnp.sum(jnp.where(is_pick, base[:, :, None], 0), axis=1, keepdims=True) + rank
    dest_spec = pl.BlockSpec((None, 1, 2 * TM), lambda i, *_: (i, 0, 0), memory_space=pltpu.SMEM)

    xs = pl.pallas_call(
        _dispatch_kernel,
        grid_spec=pltpu.PrefetchScalarGridSpec(
            num_scalar_prefetch=1,
            grid=(n_tt,),
            in_specs=[dest_spec, pl.BlockSpec((TM, D), lambda i, zt: (i, 0))],
            out_specs=pl.BlockSpec(memory_space=pl.ANY),
            scratch_shapes=[pltpu.VMEM((MOE_TM, D), F32), pltpu.SemaphoreType.DMA(()), pltpu.SemaphoreType.DMA(())],
        ),
        out_shape=jax.ShapeDtypeStruct((n_tiles * MOE_TM, D), F32),
        compiler_params=_cparams(("arbitrary",)),
        name="moe_dispatch",
    )(ztile, dest, xn.reshape(n_tok, D))

    out = pl.pallas_call(
        _moe_kernel,
        grid_spec=pltpu.PrefetchScalarGridSpec(
            num_scalar_prefetch=2,
            grid=(n_tiles,),
            in_specs=[pl.BlockSpec((MOE_TM, D), lambda i, te, nt: (i, 0)),
                      pl.BlockSpec((None, None, D, MOE_FF), lambda i, te, nt: (layer, te[i], 0, 0)),
                      pl.BlockSpec((None, None, D, MOE_FF), lambda i, te, nt: (layer, te[i], 0, 0)),
                      pl.BlockSpec((None, None, MOE_FF, D), lambda i, te, nt: (layer, te[i], 0, 0))],
            out_specs=pl.BlockSpec((MOE_TM, D), lambda i, te, nt: (i, 0)),
            scratch_shapes=[pltpu.VMEM((D, MOE_FF), BF), pltpu.VMEM((D, MOE_FF), BF), pltpu.VMEM((MOE_FF, D), BF)],
        ),
        out_shape=jax.ShapeDtypeStruct((n_tiles * MOE_TM, D), F32),
        compiler_params=_cparams(("arbitrary",)),
        name="moe_experts",
    )(te, n_valid.reshape(1).astype(jnp.int32), xs, w1, w3, w2)

    lat_b = tiles_b - 1
    n_steps = batch * lat_b if latent_only else n_tt

    def tile_of(i):
        return (i // lat_b) * tiles_b + 1 + i % lat_b if latent_only else i

    def next_tile_of(i):
        return tile_of(jnp.minimum(i + 1, n_steps - 1))

    h_new = pl.pallas_call(
        functools.partial(_combine_kernel, n_steps=n_steps),
        grid=(n_steps,),
        in_specs=[pl.BlockSpec((None, 1, 2 * TM), lambda i: (tile_of(i), 0, 0), memory_space=pltpu.SMEM),
                  pl.BlockSpec((None, 1, 2 * TM), lambda i: (next_tile_of(i), 0, 0), memory_space=pltpu.SMEM),
                  pl.BlockSpec(memory_space=pl.ANY),
                  pl.BlockSpec((TM, D), lambda i: (tile_of(i), 0)),
                  pl.BlockSpec((None, None, 8, D),
                               lambda i: (tile_of(i) // tiles_b, jnp.minimum(tile_of(i) % tiles_b, 1), 0, 0)),
                  pl.BlockSpec((TM, LANES), lambda i: (tile_of(i), 0))],
        out_specs=pl.BlockSpec((TM, D), lambda i: (i, 0)),
        out_shape=jax.ShapeDtypeStruct((n_steps * TM, D), F32),
        scratch_shapes=[pltpu.VMEM((2, 2 * TM, D), F32), pltpu.SemaphoreType.DMA((2,))],
        compiler_params=_cparams(("arbitrary",)),
        name="moe_combine",
    )(dest, dest, out, h.reshape(n_tok, D), mod, wts.reshape(n_tok, LANES))
    return h_new.reshape(batch, n_steps * TM // batch, D)


def kernel(x, c, ctx, c_ctx, ada_w, ada_b, norm1_g, norm2_g, da_w_in, da_w_out, da_q_g, da_k_g, da_lam_q1, da_lam_k1, da_lam_q2, da_lam_k2, da_sub_g, gm_w_in, gm_v_g, gm_w_s, gm_b_s, gm_w_out, ssm_w_in, ssm_conv_w, ssm_conv_b, ssm_dt_bias, ssm_a_log, ssm_d, ssm_out_g, ssm_w_out, mla_w_in, mla_q_norm_g, mla_kv_norm_g, mla_w_uq, mla_w_ukv, mla_q_g, mla_k_g, mla_w_out, moe_w_group, moe_b_group, moe_w_router, moe_b_router, moe_w1, moe_w3, moe_w2):
    batch, seq, _ = x.shape
    assert ctx.shape[1] == CTX and seq % TM == 0 and seq % GRID_W == 0
    h = jnp.concatenate([ctx, x], axis=1)

    rows = ((batch + 1 + 7) // 8) * 8
    cc = jnp.concatenate([c, c_ctx[None, :], jnp.zeros((rows - batch - 1, D), F32)], axis=0)
    ada = _ada_all(cc, ada_w, ada_b)
    mod_l = ada[:, :batch].reshape(DEPTH, batch, 6, D)
    mod_c = jnp.broadcast_to(ada[:, batch].reshape(DEPTH, 1, 6, D), (DEPTH, batch, 6, D))
    mod_all = jnp.stack([mod_c, mod_l], axis=2)
    mod_all = jnp.concatenate([mod_all, jnp.zeros((DEPTH, batch, 2, 2, D), F32)], axis=3)

    pad = LANES - MOE_EXPERTS - MOE_GROUPS
    for i in range(DEPTH):
        kind = i % 4
        mod = mod_all[i]
        n1g = norm1_g[i].reshape(1, D)
        n2g = norm2_g[i].reshape(1, D)
        wr = jnp.concatenate([moe_w_router[i], moe_w_group[i], jnp.zeros((D, pad), F32)], axis=1).astype(BF)
        br = jnp.concatenate([moe_b_router[i], moe_b_group[i], jnp.zeros((pad,), F32)]).reshape(1, LANES)
        if kind == 0:
            a = _diff_attention(h, mod, n1g, da_w_in[0], da_q_g[0], da_k_g[0], da_lam_q1[0], da_lam_k1[0],
                                da_lam_q2[0], da_lam_k2[0], da_sub_g[0], i)
            h, xn, ids, wts, cnt =_outproj(a, da_w_out[0].astype(BF), h, mod, n2g, wr, br)
        elif kind == 1:
            h, xn, ids, wts, cnt =_gmlp_layer(h, mod, n1g, gm_w_in[0], gm_v_g[0], gm_w_s[0], gm_b_s[0], gm_w_out[0],
                                          n2g, wr, br)
        elif kind == 2:
            h, xn, ids, wts, cnt =_mamba_layer(h, mod, n1g, ssm_w_in[0], ssm_conv_w[0], ssm_conv_b[0], ssm_dt_bias[0],
                                           ssm_a_log[0], ssm_d[0], ssm_out_g[0], ssm_w_out[0], n2g, wr, br)
        else:
            a = _mla_attention(h, mod, n1g, mla_w_in[0], mla_q_norm_g[0], mla_kv_norm_g[0], mla_w_uq[0],
                               mla_w_ukv[0], mla_q_g[0], mla_k_g[0])
            h, xn, ids, wts, cnt =_outproj(a, mla_w_out[0].astype(BF), h, mod, n2g, wr, br)
        h = _moe(h, mod, xn, ids, wts, cnt, moe_w1, moe_w3, moe_w2, i, latent_only=(i == DEPTH - 1))
    return h
```

```python
import functools
import math

import numpy as np
import jax
import jax.numpy as jnp
from jax import lax
from jax.experimental import pallas as pl
from jax.experimental.pallas import tpu as pltpu

F32 = jnp.float32
BF = jnp.bfloat16
HIGHEST = lax.Precision.HIGHEST

D = 1024
CTX = 256
GRID_W = 64
EPS = 1e-6
ROPE_BASE = 10000.0
DEPTH = 4
LANES = 128
TM = 256

DA_HEADS = 8
DA_HEAD_DIM = 64

GM_CHUNK = 128
GM_GROUPS = 8

SSM_INNER = 2048
SSM_HEADS = 32
SSM_GROUPS = 4
SSM_STATE = 128
SSM_CONV = 5
SSM_CHUNK = 128
SSM_CONV_DIM = SSM_INNER + 2 * SSM_GROUPS * SSM_STATE

MLA_HEADS = 16
MLA_Q_RANK = 384
MLA_KV_RANK = 256
MLA_NOPE = 64
MLA_ROPE = 32
MLA_V = 64
MLA_QK = MLA_NOPE + MLA_ROPE

MOE_GROUPS = 4
MOE_PER_GROUP = 8
MOE_EXPERTS = 32
MOE_FF = 512
MOE_TM = 256

VMEM_LIMIT = 56 * 1024 * 1024


def _cparams(sem):
    return pltpu.CompilerParams(dimension_semantics=sem, vmem_limit_bytes=VMEM_LIMIT)


def _full_spec(arr):
    nd = arr.ndim
    return pl.BlockSpec(arr.shape, lambda *_: (0,) * nd)


def _tok_spec(width, col=0):
    return pl.BlockSpec((None, TM, width), lambda b, t: (b, t, col))


def _mod_spec():
    return pl.BlockSpec((None, None, 8, D), lambda b, t: (b, jnp.minimum(t, 1), 0, 0))


def _silu(x):
    return x * (1.0 / (1.0 + jnp.exp(-x)))


def _norm_mod(h, g, scale, shift):
    ms = jnp.mean(h * h, axis=-1, keepdims=True)
    y = h * lax.rsqrt(ms + EPS) * g
    return y * (1.0 + scale) + shift


def _ada_kernel(c_ref, w_ref, b_ref, o_ref):
    a = _silu(c_ref[...]).astype(BF)
    o_ref[...] = jnp.dot(a, w_ref[...].astype(BF), preferred_element_type=F32) + b_ref[...]


def _ada_all(cc, ada_w, ada_b):
    rows = cc.shape[0]
    tn = 1536
    return pl.pallas_call(
        _ada_kernel,
        grid=(DEPTH, 6 * D // tn),
        in_specs=[pl.BlockSpec((rows, D), lambda l, j: (0, 0)),
                  pl.BlockSpec((None, D, tn), lambda l, j: (l, 0, j)),
                  pl.BlockSpec((None, 1, tn), lambda l, j: (l, 0, j))],
        out_specs=pl.BlockSpec((None, rows, tn), lambda l, j: (l, 0, j)),
        out_shape=jax.ShapeDtypeStruct((DEPTH, rows, 6 * D), F32),
        compiler_params=_cparams(("arbitrary", "arbitrary")),
        name="ada_mod",
    )(cc, ada_w, ada_b.reshape(DEPTH, 1, 6 * D))


def _residual_and_route(delta, h_ref, mod, n2g_ref, wr_ref, br_ref, outs):
    ho_ref, xo_ref, id_ref, wt_ref, cnt_ref = outs
    hn = h_ref[...] + mod[2:3] * delta
    ho_ref[...] = hn
    x2 = _norm_mod(hn, n2g_ref[...], mod[4:5], mod[3:4])
    xo_ref[...] = x2
    lg = jnp.dot(x2.astype(BF), wr_ref[...], preferred_element_type=F32) + br_ref[...]
    lane = lax.broadcasted_iota(jnp.int32, lg.shape, 1)
    neg = jnp.float32(-jnp.inf)
    big = jnp.int32(1 << 20)
    is_g = jnp.logical_and(lane >= MOE_EXPERTS, lane < MOE_EXPERTS + MOE_GROUPS)
    gl = jnp.where(is_g, lg, neg)
    gmax = jnp.max(gl, axis=-1, keepdims=True)
    gidx = jnp.min(jnp.where(gl == gmax, lane, big), axis=-1, keepdims=True) - MOE_EXPERTS
    pg = 1.0 / jnp.sum(jnp.where(is_g, jnp.exp(gl - gmax), 0.0), axis=-1, keepdims=True)
    lo = gidx * MOE_PER_GROUP
    in_grp = jnp.logical_and(lane >= lo, lane < lo + MOE_PER_GROUP)
    el = jnp.where(in_grp, lg, neg)
    e1 = jnp.max(el, axis=-1, keepdims=True)
    i1 = jnp.min(jnp.where(el == e1, lane, big), axis=-1, keepdims=True)
    el2 = jnp.where(lane == i1, neg, el)
    e2 = jnp.max(el2, axis=-1, keepdims=True)
    i2 = jnp.min(jnp.where(el2 == e2, lane, big), axis=-1, keepdims=True)
    t = jnp.exp(e2 - e1)
    w1 = pg / (1.0 + t)
    w2 = pg * t / (1.0 + t)
    wt_ref[...] = jnp.where(lane == 0, w1, jnp.where(lane == 1, w2, 0.0))
    sel1 = lane == i1
    sel2 = lane == i2
    chosen = jnp.where(jnp.logical_or(sel1, sel2), 1.0, 0.0)
    cnt_ref[...] = jnp.sum(chosen, axis=0, keepdims=True)
    rows = lg.shape[0]
    r = lax.broadcasted_iota(jnp.int32, (rows, rows), 0)
    c = lax.broadcasted_iota(jnp.int32, (rows, rows), 1)
    earlier = jnp.where(c < r, 1.0, 0.0).astype(BF)
    before = jnp.dot(earlier, chosen.astype(BF), preferred_element_type=F32)
    rank1 = jnp.sum(jnp.where(sel1, before, 0.0), axis=-1, keepdims=True).astype(jnp.int32)
    rank2 = jnp.sum(jnp.where(sel2, before, 0.0), axis=-1, keepdims=True).astype(jnp.int32)
    id_ref[...] = jnp.where(lane == 0, i1, jnp.where(lane == 1, i2,
                            jnp.where(lane == 2, rank1, jnp.where(lane == 3, rank2, 0))))


def _tail_in_specs(n2g, wr, br):
    return [_full_spec(n2g), _full_spec(wr), _full_spec(br)]


def _tail_out(batch, t_all):
    specs = [_tok_spec(D), _tok_spec(D), _tok_spec(LANES), _tok_spec(LANES),
             pl.BlockSpec((None, None, 1, LANES), lambda b, t: (b, t, 0, 0))]
    shapes = [jax.ShapeDtypeStruct((batch, t_all, D), F32),
              jax.ShapeDtypeStruct((batch, t_all, D), F32),
              jax.ShapeDtypeStruct((batch, t_all, LANES), jnp.int32),
              jax.ShapeDtypeStruct((batch, t_all, LANES), F32),
              jax.ShapeDtypeStruct((batch, t_all // TM, 1, LANES), F32)]
    return specs, shapes


def _outproj_kernel(a_ref, w_ref, h_ref, mod_ref, n2g_ref, wr_ref, br_ref, *outs):
    delta = jnp.dot(a_ref[...], w_ref[...], preferred_element_type=F32)
    _residual_and_route(delta, h_ref, mod_ref[...], n2g_ref, wr_ref, br_ref, outs)


def _outproj(a, w, h, mod, n2g, wr, br):
    batch, t_all, k = a.shape
    out_specs, out_shapes = _tail_out(batch, t_all)
    return pl.pallas_call(
        _outproj_kernel,
        grid=(batch, t_all // TM),
        in_specs=[_tok_spec(k), _full_spec(w), _tok_spec(D), _mod_spec()] + _tail_in_specs(n2g, wr, br),
        out_specs=out_specs,
        out_shape=out_shapes,
        compiler_params=_cparams(("parallel", "parallel")),
        name="outproj_route",
    )(a, w, h, mod, n2g, wr, br)


def _rope_tables(t_lat, rot_dim):
    rows = t_lat // GRID_W
    row = np.repeat(np.arange(rows, dtype=np.float32), GRID_W)
    col = np.tile(np.arange(GRID_W, dtype=np.float32), rows)
    n_freq = rot_dim // 4
    inv_freq = jnp.asarray(ROPE_BASE, F32) ** (-jnp.arange(n_freq, dtype=F32) / n_freq)
    ang = jnp.concatenate([jnp.asarray(row)[:, None] * inv_freq, jnp.asarray(col)[:, None] * inv_freq], axis=-1)
    return jnp.cos(ang), jnp.sin(ang)


def _with_ctx_identity(cos_l, sin_l):
    cos = jnp.concatenate([jnp.ones((CTX, LANES), F32), cos_l], axis=0)
    sin = jnp.concatenate([jnp.zeros((CTX, LANES), F32), sin_l], axis=0)
    return cos, sin


def _da_perm():
    perm = np.zeros(2 * DA_HEADS * DA_HEAD_DIM, np.int32)
    for h in range(DA_HEADS):
        for m in range(2):
            for i in range(DA_HEAD_DIM):
                new = h * LANES + (i // 32) * 64 + m * 32 + (i % 32)
                perm[new] = (2 * h + m) * DA_HEAD_DIM + i
    return perm


def _da_lane_of_dim():
    lane = np.arange(LANES)
    return (lane // 64) * 32 + lane % 32


def _da_proj_kernel(h_ref, mod_ref, n1g_ref, w_ref, gq_ref, gk_ref, cos_ref, sin_ref, gavg_ref,
                    q_ref, k_ref, v_ref):
    mod = mod_ref[...]
    xn = _norm_mod(h_ref[...], n1g_ref[...], mod[1:2], mod[0:1]).astype(BF)
    cos = cos_ref[...]
    sin = sin_ref[...]
    gavg = gavg_ref[...]
    nq = DA_HEADS * LANES
    scale = DA_HEAD_DIM ** -0.5 * LOG2E
    for dst, off, g_ref, mult in ((q_ref, 0, gq_ref, scale), (k_ref, nq, gk_ref, 1.0)):
        y_all = jnp.dot(xn, w_ref[:, off:off + nq], preferred_element_type=F32)
        for jj in range(DA_HEADS // 2):
            yy = y_all[:, jj * 2 * LANES:(jj + 1) * 2 * LANES]
            ms2 = jnp.dot((yy * yy).astype(BF), gavg, preferred_element_type=F32)
            for j in (2 * jj, 2 * jj + 1):
                y = y_all[:, j * LANES:(j + 1) * LANES]
                ms = ms2[:, (j % 2) * LANES:(j % 2 + 1) * LANES]
                y = y * lax.rsqrt(ms + EPS) * g_ref[...]
                y = y * cos + pltpu.roll(y, 64, 1) * sin
                dst[:, j * LANES:(j + 1) * LANES] = (y * mult).astype(BF)
    v_ref[...] = jnp.dot(xn, w_ref[:, 2 * nq:3 * nq], preferred_element_type=F32).astype(BF)


LOG2E = math.log2(math.e)


def _softmax_pv(q, k, v):
    s = lax.dot_general(q, k, (((1,), (1,)), ((), ())), preferred_element_type=F32)
    e = jnp.exp2(s - jnp.max(s, axis=-1, keepdims=True))
    l = jnp.sum(e, axis=-1, keepdims=True)
    return jnp.dot(e.astype(BF), v, preferred_element_type=F32) / l


ATTN_TILES_PER_ITER = 8


def _for_each_latent_tile(tile_fn, t_all):
    n_lat = (t_all - CTX) // TM
    per = ATTN_TILES_PER_ITER if n_lat % ATTN_TILES_PER_ITER == 0 else 1

    def body(j, carry):
        for u in range(per):
            tile_fn(pl.multiple_of(CTX + (j * per + u) * TM, TM), t_all)
        return carry

    lax.fori_loop(0, n_lat // per, body, 0)


def _da_attn_kernel(lam_ref, q_ref, k_ref, v_ref, subg_ref, o_ref, *, lam_init):
    t_all = q_ref.shape[0]
    lane = lax.broadcasted_iota(jnp.int32, (TM, LANES), 1)
    comp0 = ((lane // 32) % 2) == 0
    zero = jnp.zeros((TM, LANES), BF)

    def tile(row0, n_keys):
        q = q_ref[pl.ds(row0, TM), :]
        k = k_ref[0:n_keys, :]
        v = v_ref[0:n_keys, :]
        o = _softmax_pv(jnp.where(comp0, q, zero), k, v) - lam_ref[0] * _softmax_pv(jnp.where(comp0, zero, q), k, v)
        ms = jnp.mean(o * o, axis=-1, keepdims=True)
        o = o * lax.rsqrt(ms + EPS) * subg_ref[...] * (1.0 - lam_init)
        o_ref[pl.ds(row0, TM), :] = o.astype(BF)

    tile(0, CTX)

    _for_each_latent_tile(tile, t_all)


def _diff_attention(h, mod, n1g, w_in, q_g, k_g, lam_q1, lam_k1, lam_q2, lam_k2, sub_g, layer_idx):
    batch, t_all, _ = h.shape
    nq = DA_HEADS * LANES
    perm = _da_perm()
    w = jnp.concatenate([w_in[:, :nq][:, perm], w_in[:, nq:2 * nq][:, perm], w_in[:, 2 * nq:]], axis=1).astype(BF)
    ldim = _da_lane_of_dim()
    gq = q_g[ldim].reshape(1, LANES)
    gk = k_g[ldim].reshape(1, LANES)
    cos, sin = _rope_tables(t_all - CTX, DA_HEAD_DIM)
    cos, sin = _with_ctx_identity(jnp.tile(cos, (1, 4)), jnp.concatenate([-sin, -sin, sin, sin], axis=1))
    unit = np.arange(2 * LANES) // LANES * 2 + (np.arange(2 * LANES) // 32) % 2
    gavg = jnp.asarray((unit[:, None] == unit[None, :]).astype(np.float32) / DA_HEAD_DIM).astype(BF)
    tab_spec = pl.BlockSpec((TM, LANES), lambda b, t: (t, 0))
    qkv_shape = jax.ShapeDtypeStruct((batch, t_all, nq), BF)
    q, k, v = pl.pallas_call(
        _da_proj_kernel,
        grid=(batch, t_all // TM),
        in_specs=[_tok_spec(D), _mod_spec(), _full_spec(n1g), _full_spec(w), _full_spec(gq), _full_spec(gk),
                  tab_spec, tab_spec, _full_spec(gavg)],
        out_specs=[_tok_spec(nq)] * 3,
        out_shape=[qkv_shape] * 3,
        compiler_params=_cparams(("parallel", "parallel")),
        name="da_proj",
    )(h, mod, n1g, w, gq, gk, cos, sin, gavg)

    lam_init = 0.8 - 0.6 * math.exp(-0.3 * layer_idx)
    lam = (jnp.exp(jnp.sum(lam_q1 * lam_k1)) - jnp.exp(jnp.sum(lam_q2 * lam_k2)) + lam_init).reshape(1)
    seq_spec = pl.BlockSpec((None, t_all, LANES), lambda b, hh: (b, 0, hh))
    subg = sub_g.reshape(1, LANES)
    return pl.pallas_call(
        functools.partial(_da_attn_kernel, lam_init=lam_init),
        grid=(batch, DA_HEADS),
        in_specs=[pl.BlockSpec(memory_space=pltpu.SMEM), seq_spec, seq_spec, seq_spec,
                  pl.BlockSpec((1, LANES), lambda b, hh: (0, 0))],
        out_specs=seq_spec,
        out_shape=jax.ShapeDtypeStruct((batch, t_all, nq), BF),
        compiler_params=_cparams(("parallel", "parallel")),
        name="da_attn",
    )(lam, q, k, v, subg)


def _gmlp_kernel(h_ref, mod_ref, n1g_ref, win_ref, vg_ref, ws_ref, bs_ref, wout_ref,
                 n2g_ref, wr_ref, br_ref, *outs):
    mod = mod_ref[...]
    xn = _norm_mod(h_ref[...], n1g_ref[...], mod[1:2], mod[0:1]).astype(BF)
    uv =jnp.dot(xn, win_ref[...], preferred_element_type=F32)
    uv = 0.5 * uv * (1.0 + lax.erf(uv * (2.0 ** -0.5)))
    u = uv[:, :D]
    v = uv[:, D:]
    v = (v * lax.rsqrt(jnp.mean(v * v, axis=-1, keepdims=True) + EPS) * vg_ref[...]).astype(BF)
    rows = []
    for c in range(TM // GM_CHUNK):
        cols = []
        for g in range(GM_GROUPS):
            vb = v[c * GM_CHUNK:(c + 1) * GM_CHUNK, g * LANES:(g + 1) * LANES]
            s = jnp.dot(ws_ref[g], vb, preferred_element_type=F32) + bs_ref[g]
            cols.append(s)
        rows.append(jnp.concatenate(cols, axis=1))
    s_all = jnp.concatenate(rows, axis=0)
    delta = jnp.dot((u * s_all).astype(BF), wout_ref[...], preferred_element_type=F32)
    _residual_and_route(delta, h_ref, mod, n2g_ref, wr_ref, br_ref, outs)


def _gmlp_layer(h, mod, n1g, w_in, v_g, w_s, b_s, w_out, n2g, wr, br):
    batch, t_all, _ = h.shape
    win = w_in.astype(BF)
    wout = w_out.astype(BF)
    ws = w_s.astype(BF)
    bs = jnp.broadcast_to(b_s[:, :, None], (GM_GROUPS, GM_CHUNK, LANES)).astype(F32)
    vg = v_g.reshape(1, D)
    out_specs, out_shapes = _tail_out(batch, t_all)
    return pl.pallas_call(
        _gmlp_kernel,
        grid=(batch, t_all // TM),
        in_specs=[_tok_spec(D), _mod_spec(), _full_spec(n1g), _full_spec(win), _full_spec(vg), _full_spec(ws),
                  _full_spec(bs), _full_spec(wout)] + _tail_in_specs(n2g, wr, br),
        out_specs=out_specs,
        out_shape=out_shapes,
        compiler_params=_cparams(("parallel", "parallel")),
        name="gmlp_layer",
    )(h, mod, n1g, win, vg, ws, bs, wout, n2g, wr, br)


SSM_DT_PAD = LANES


def _ssm_proj_kernel(h_ref, mod_ref, n1g_ref, w_ref, dtb_ref, z_ref, xbc_ref, dt_ref):
    mod = mod_ref[...]
    xn = _norm_mod(h_ref[...], n1g_ref[...], mod[1:2], mod[0:1]).astype(BF)
    z_ref[...] = jnp.dot(xn, w_ref[:, :SSM_INNER], preferred_element_type=F32).astype(BF)
    c0 = SSM_INNER
    for j in range(SSM_CONV_DIM // 1024):
        xbc_ref[:, j * 1024:(j + 1) * 1024] = jnp.dot(
            xn, w_ref[:, c0 + j * 1024:c0 + (j + 1) * 1024], preferred_element_type=F32).astype(BF)
    c1 = SSM_INNER + SSM_CONV_DIM
    raw = jnp.dot(xn, w_ref[:, c1:c1 + SSM_DT_PAD], preferred_element_type=F32) + dtb_ref[...]
    dt_ref[...] = jnp.maximum(raw, 0.0) + jnp.log1p(jnp.exp(-jnp.abs(raw)))


CONV_PAD = 8
CONV_W = 512


def _conv_kernel(x_ref, w_ref, b_ref, o_ref, pad_ref):
    t_all = x_ref.shape[0]
    t_lat = t_all - CTX
    lat0 = CTX + 2 * CONV_PAD
    zeros = jnp.zeros((CONV_PAD, CONV_W), F32)
    pad_ref[0:CONV_PAD, :] = zeros
    pad_ref[CONV_PAD + CTX:lat0, :] = jnp.zeros((CONV_PAD, CONV_W), F32)
    pad_ref[lat0 + t_lat:lat0 + t_lat + CONV_PAD, :] = zeros
    pad_ref[CONV_PAD:CONV_PAD + CTX, :] = x_ref[0:CTX, :].astype(F32)
    pad_ref[lat0:lat0 + t_lat, :] = x_ref[CTX:t_all, :].astype(F32)
    half = (SSM_CONV - 1) // 2
    padded = pad_ref[...]
    rows = padded.shape[0]
    taps = [padded if kk == half else pltpu.roll(padded, (half - kk) % rows, 0) for kk in range(SSM_CONV)]
    for start, n, dst in ((CONV_PAD, CTX, 0), (lat0, t_lat, CTX)):
        acc = jnp.zeros((n, CONV_W), F32) + b_ref[...]
        for kk in range(SSM_CONV):
            acc = acc + taps[kk][start:start + n, :] * w_ref[kk:kk + 1, :]
        o_ref[dst:dst + n, :] = _silu(acc).astype(BF)


def _ssd_kernel(dt_ref, a_ref, x_ref, b_ref, c_ref, y_ref, s_ref):
    direction = pl.program_id(1)
    step = pl.program_id(2)

    @pl.when(step == 0)
    def _():
        s_ref[...] = jnp.zeros_like(s_ref)

    q = SSM_CHUNK
    row = lax.broadcasted_iota(jnp.int32, (q, q), 0)
    col = lax.broadcasted_iota(jnp.int32, (q, q), 1)
    sign = 1 - 2 * direction
    tri_b = (col - row) * sign <= 0
    tri = tri_b.astype(F32)
    tri_t = ((row - col) * sign <= 0).astype(F32)
    eye = (row == col).astype(F32)
    lane_lo = col < 64

    dt_t = dt_ref[...]
    a_t = dt_t * a_ref[...]
    cs_rows = jnp.dot(a_t, tri_t, preferred_element_type=F32, precision=HIGHEST)
    tot = jnp.sum(a_t, axis=1, keepdims=True)
    heads_per_group = SSM_HEADS // SSM_GROUPS

    for g in range(SSM_GROUPS):
        bm = b_ref[:, g * SSM_STATE:(g + 1) * SSM_STATE]
        cm = c_ref[:, g * SSM_STATE:(g + 1) * SSM_STATE]
        cb = lax.dot_general(cm, bm, (((1,), (1,)), ((), ())), preferred_element_type=F32)
        bm_t = bm.astype(F32).T.astype(BF)
        for pp in range(heads_per_group // 2):
            p = g * (heads_per_group // 2) + pp
            xp = x_ref[:, p * LANES:(p + 1) * LANES].astype(F32)
            cs_col, dt_col, mats = [], [], []
            for hh in range(2):
                e = 2 * p + hh
                cc = jnp.sum(tri * a_t[e:e + 1, :], axis=1, keepdims=True)
                cs_col.append(cc)
                dt_col.append(jnp.sum(eye * dt_t[e:e + 1, :], axis=1, keepdims=True))
                decay = jnp.exp(jnp.where(tri_b, cc - cs_rows[e:e + 1, :], -jnp.inf))
                mats.append((cb * decay).astype(BF))
            xdt = xp * jnp.where(lane_lo, dt_col[0], dt_col[1])
            xb = xdt.astype(BF)
            zb = jnp.zeros_like(xb)
            y_diag = jnp.dot(jnp.concatenate(mats, axis=1),
                             jnp.concatenate([jnp.where(lane_lo, xb, zb), jnp.where(lane_lo, zb, xb)], axis=0),
                             preferred_element_type=F32)
            s_prev = s_ref[p]
            y_off = jnp.dot(cm, s_prev.astype(BF), preferred_element_type=F32)
            y_off = y_off * jnp.exp(jnp.where(lane_lo, cs_col[0], cs_col[1]))
            y_ref[:, p * LANES:(p + 1) * LANES] = (y_diag + y_off).astype(BF)
            t0 = tot[2 * p:2 * p + 1, :]
            t1 = tot[2 * p + 1:2 * p + 2, :]
            to_end = jnp.exp(jnp.where(lane_lo, t0 - cs_col[0], t1 - cs_col[1]))
            upd = jnp.dot(bm_t, (xdt * to_end).astype(BF), preferred_element_type=F32)
            s_ref[p] = s_prev * jnp.exp(jnp.where(lane_lo[0:1, :], t0, t1)) + upd


def _ssm_finish_kernel(yf_ref, yb_ref, xs_ref, z_ref, d_ref, og_ref, w_ref, h_ref, mod_ref,
                       n2g_ref, wr_ref, br_ref, *outs):
    y = yf_ref[...].astype(F32) + yb_ref[...].astype(F32) + d_ref[...] * xs_ref[...].astype(F32)
    y = y * _silu(z_ref[...].astype(F32))
    y = y * lax.rsqrt(jnp.mean(y * y, axis=-1, keepdims=True) + EPS) * og_ref[...]
    delta = jnp.dot(y.astype(BF), w_ref[...], preferred_element_type=F32)
    _residual_and_route(delta, h_ref, mod_ref[...], n2g_ref, wr_ref, br_ref, outs)


def _mamba_layer(h, mod, n1g, w_in, conv_w, conv_b, dt_bias, a_log, d_skip, out_g, w_out, n2g, wr, br):
    batch, t_all, _ = h.shape
    n_dt = 2 * SSM_HEADS
    c1 = SSM_INNER + SSM_CONV_DIM
    w = jnp.concatenate([w_in, jnp.zeros((D, SSM_DT_PAD - n_dt), F32)], axis=1).astype(BF)
    dtb = jnp.concatenate([dt_bias.reshape(n_dt), jnp.zeros((SSM_DT_PAD - n_dt,), F32)]).reshape(1, SSM_DT_PAD)
    z, xbc, dt = pl.pallas_call(
        _ssm_proj_kernel,
        grid=(batch, t_all // TM),
        in_specs=[_tok_spec(D), _mod_spec(), _full_spec(n1g), _full_spec(w), _full_spec(dtb)],
        out_specs=[_tok_spec(SSM_INNER), _tok_spec(SSM_CONV_DIM), _tok_spec(SSM_DT_PAD)],
        out_shape=[jax.ShapeDtypeStruct((batch, t_all, SSM_INNER), BF),
                   jax.ShapeDtypeStruct((batch, t_all, SSM_CONV_DIM), BF),
                   jax.ShapeDtypeStruct((batch, t_all, SSM_DT_PAD), F32)],
        compiler_params=_cparams(("parallel", "parallel")),
        name="ssm_proj",
    )(h, mod, n1g, w, dtb)
    assert c1 + n_dt == w_in.shape[1]

    cw = jnp.concatenate([conv_w, jnp.zeros((8 - SSM_CONV, SSM_CONV_DIM), F32)], axis=0)
    cbias = conv_b.reshape(1, SSM_CONV_DIM)
    xbc = pl.pallas_call(
        _conv_kernel,
        grid=(batch, SSM_CONV_DIM // CONV_W),
        in_specs=[pl.BlockSpec((None, t_all, CONV_W), lambda b, j: (b, 0, j)),
                  pl.BlockSpec((8, CONV_W), lambda b, j: (0, j)),
                  pl.BlockSpec((1, CONV_W), lambda b, j: (0, j))],
        out_specs=pl.BlockSpec((None, t_all, CONV_W), lambda b, j: (b, 0, j)),
        out_shape=jax.ShapeDtypeStruct((batch, t_all, SSM_CONV_DIM), BF),
        scratch_shapes=[pltpu.VMEM((t_all + 3 * CONV_PAD, CONV_W), F32)],
        compiler_params=_cparams(("parallel", "parallel")),
        name="ssm_conv",
    )(xbc, cw, cbias)

    n_chunks = t_all // SSM_CHUNK
    ctx_chunks = CTX // SSM_CHUNK
    dt_t = jnp.swapaxes(dt[:, :, :n_dt], 1, 2)
    a_neg = jnp.broadcast_to(-jnp.exp(a_log.astype(F32)).reshape(2, SSM_HEADS, 1), (2, SSM_HEADS, SSM_CHUNK))

    def chunk_of(dr, s):
        rev = jnp.where(s < ctx_chunks, ctx_chunks - 1 - s, n_chunks - 1 + ctx_chunks - s)
        return jnp.where(dr == 0, s, rev)

    gs = SSM_GROUPS * SSM_STATE
    y = pl.pallas_call(
        _ssd_kernel,
        grid=(batch, 2, n_chunks),
        in_specs=[pl.BlockSpec((None, SSM_HEADS, SSM_CHUNK), lambda b, dr, s: (b, dr, chunk_of(dr, s))),
                  pl.BlockSpec((None, SSM_HEADS, SSM_CHUNK), lambda b, dr, s: (dr, 0, 0)),
                  pl.BlockSpec((None, SSM_CHUNK, SSM_INNER), lambda b, dr, s: (b, chunk_of(dr, s), 0)),
                  pl.BlockSpec((None, SSM_CHUNK, gs), lambda b, dr, s: (b, chunk_of(dr, s), SSM_INNER // gs)),
                  pl.BlockSpec((None, SSM_CHUNK, gs), lambda b, dr, s: (b, chunk_of(dr, s), SSM_INNER // gs + 1))],
        out_specs=pl.BlockSpec((None, None, SSM_CHUNK, SSM_INNER), lambda b, dr, s: (dr, b, chunk_of(dr, s), 0)),
        out_shape=jax.ShapeDtypeStruct((2, batch, t_all, SSM_INNER), BF),
        scratch_shapes=[pltpu.VMEM((SSM_HEADS // 2, SSM_STATE, LANES), F32)],
        compiler_params=_cparams(("parallel", "parallel", "arbitrary")),
        name="ssd_scan",
    )(dt_t, a_neg, xbc, xbc, xbc)

    dvec = jnp.repeat(d_skip, SSM_INNER // SSM_HEADS).reshape(1, SSM_INNER)
    og = out_g.reshape(1, SSM_INNER)
    wo = w_out.astype(BF)
    out_specs, out_shapes = _tail_out(batch, t_all)
    y_spec = lambda dr: pl.BlockSpec((None, None, TM, SSM_INNER), lambda b, t: (dr, b, t, 0))
    return pl.pallas_call(
        _ssm_finish_kernel,
        grid=(batch, t_all // TM),
        in_specs=[y_spec(0), y_spec(1), _tok_spec(SSM_INNER), _tok_spec(SSM_INNER), _full_spec(dvec),
                  _full_spec(og), _full_spec(wo), _tok_spec(D), _mod_spec()] + _tail_in_specs(n2g, wr, br),
        out_specs=out_specs,
        out_shape=out_shapes,
        compiler_params=_cparams(("parallel", "parallel")),
        name="ssm_finish",
    )(y, y, xbc, z, dvec, og, wo, h, mod, n2g, wr, br)


def _mla_lane_src():
    src = np.full(LANES, -1, np.int32)
    src[0:16] = MLA_NOPE + np.arange(16)
    src[16:64] = np.arange(48)
    src[64:80] = MLA_NOPE + 16 + np.arange(16)
    src[80:96] = 48 + np.arange(16)
    return src


def _mla_proj_kernel(h_ref, mod_ref, n1g_ref, win_ref, qng_ref, kvng_ref, wuq_ref, wuk_ref, wuv_ref,
                     gq_ref, gk_ref, cos_ref, sin_ref, avg_ref, q_ref, k_ref, v_ref):
    mod = mod_ref[...]
    xn = _norm_mod(h_ref[...], n1g_ref[...], mod[1:2], mod[0:1]).astype(BF)
    lat = jnp.dot(xn, win_ref[...], preferred_element_type=F32)
    cq = lat[:, :MLA_Q_RANK]
    ckv = lat[:, MLA_Q_RANK:MLA_Q_RANK + MLA_KV_RANK]
    kpe = lat[:, MLA_Q_RANK + MLA_KV_RANK:]
    cq = (cq * lax.rsqrt(jnp.mean(cq * cq, axis=-1, keepdims=True) + EPS) * qng_ref[...]).astype(BF)
    ckv = (ckv * lax.rsqrt(jnp.mean(ckv * ckv, axis=-1, keepdims=True) + EPS) * kvng_ref[...]).astype(BF)
    cos = cos_ref[...]
    sin = sin_ref[...]
    avg = avg_ref[...]
    scale = MLA_QK ** -0.5 * LOG2E
    qa = jnp.dot(cq, wuq_ref[...], preferred_element_type=F32)
    ka = jnp.dot(ckv, wuk_ref[...], preferred_element_type=F32)
    kpe2 = jnp.concatenate([kpe, kpe], axis=1)
    for jj in range(MLA_HEADS // 2):
        for dst, src, add, g_ref, mult in ((q_ref, qa, None, gq_ref, scale), (k_ref, ka, kpe2, gk_ref, 1.0)):
            yy = src[:, jj * 2 * LANES:(jj + 1) * 2 * LANES]
            if add is not None:
                yy = yy + add
            ms2 = jnp.dot((yy * yy).astype(BF), avg, preferred_element_type=F32) * (1.0 / MLA_QK)
            for hh in range(2):
                j = 2 * jj + hh
                y = yy[:, hh * LANES:(hh + 1) * LANES]
                y = y * lax.rsqrt(ms2[:, hh * LANES:(hh + 1) * LANES] + EPS) * g_ref[...]
                y = y * cos + pltpu.roll(y, 64, 1) * sin
                dst[:, j * LANES:(j + 1) * LANES] = (y * mult).astype(BF)
    v_ref[...] = jnp.dot(ckv, wuv_ref[...], preferred_element_type=F32).astype(BF)


def _mla_attn_kernel(q_ref, k_ref, v_ref, o_ref):
    t_all = q_ref.shape[0]
    lane = lax.broadcasted_iota(jnp.int32, (TM, LANES), 1)

    def tile(row0, n_keys):
        v = v_ref[0:n_keys, :]
        outs = [_softmax_pv(q_ref[pl.ds(row0, TM), hh * LANES:(hh + 1) * LANES],
                            k_ref[0:n_keys, hh * LANES:(hh + 1) * LANES], v)
                for hh in range(2)]
        o_ref[pl.ds(row0, TM), :] = jnp.where(lane < MLA_V, outs[0], outs[1]).astype(BF)

    tile(0, CTX)

    _for_each_latent_tile(tile, t_all)


def _mla_attention(h, mod, n1g, w_in, q_norm_g, kv_norm_g, w_uq, w_ukv, q_g, k_g):
    batch, t_all, _ = h.shape
    src = _mla_lane_src()
    used = src >= 0
    srcc = np.where(used, src, 0)
    nh = MLA_HEADS * LANES
    kpe_cols = np.where(srcc >= MLA_NOPE, MLA_Q_RANK + MLA_KV_RANK + srcc - MLA_NOPE, 0)
    kpe_used = np.logical_and(used, src >= MLA_NOPE)
    w_kpe = jnp.where(jnp.asarray(kpe_used)[None, :], w_in[:, kpe_cols], 0.0)
    win = jnp.concatenate([w_in[:, :MLA_Q_RANK + MLA_KV_RANK], w_kpe], axis=1).astype(BF)
    q_cols = (np.arange(MLA_HEADS)[:, None] * MLA_QK + srcc[None, :]).reshape(-1)
    q_mask = jnp.asarray(np.tile(used, MLA_HEADS))[None, :]
    wuq = jnp.where(q_mask, w_uq[:, q_cols], 0.0).astype(BF)
    nope_used = np.logical_and(used, src < MLA_NOPE)
    k_cols = (np.arange(MLA_HEADS)[:, None] * (MLA_NOPE + MLA_V) + np.where(nope_used, srcc, 0)[None, :]).reshape(-1)
    k_mask = jnp.asarray(np.tile(nope_used, MLA_HEADS))[None, :]
    wuk = jnp.where(k_mask, w_ukv[:, k_cols], 0.0).astype(BF)
    v_cols = (np.arange(MLA_HEADS)[:, None] * (MLA_NOPE + MLA_V) + MLA_NOPE + np.arange(MLA_V)[None, :]).reshape(-1)
    wuv = w_ukv[:, v_cols].astype(BF)
    gq = jnp.where(jnp.asarray(used), q_g[srcc], 0.0).reshape(1, LANES)
    gk = jnp.where(jnp.asarray(used), k_g[srcc], 0.0).reshape(1, LANES)
    cos16, sin16 = _rope_tables(t_all - CTX, MLA_ROPE)
    t_lat = t_all - CTX
    cos_l = jnp.ones((t_lat, LANES), F32).at[:, 0:16].set(cos16).at[:, 64:80].set(cos16)
    sin_l = jnp.zeros((t_lat, LANES), F32).at[:, 0:16].set(-sin16).at[:, 64:80].set(sin16)
    cos, sin = _with_ctx_identity(cos_l, sin_l)
    head_of_lane = np.arange(2 * LANES) // LANES
    avg = jnp.asarray((head_of_lane[:, None] == head_of_lane[None, :]).astype(np.float32)).astype(BF)
    qng = q_norm_g.reshape(1, MLA_Q_RANK)
    kvng = kv_norm_g.reshape(1, MLA_KV_RANK)
    tab_spec = pl.BlockSpec((TM, LANES), lambda b, t: (t, 0))
    q, k, v = pl.pallas_call(
        _mla_proj_kernel,
        grid=(batch, t_all // TM),
        in_specs=[_tok_spec(D), _mod_spec(), _full_spec(n1g), _full_spec(win), _full_spec(qng), _full_spec(kvng),
                  _full_spec(wuq), _full_spec(wuk), _full_spec(wuv), _full_spec(gq), _full_spec(gk),
                  tab_spec, tab_spec, _full_spec(avg)],
        out_specs=[_tok_spec(nh), _tok_spec(nh), _tok_spec(MLA_HEADS * MLA_V)],
        out_shape=[jax.ShapeDtypeStruct((batch, t_all, nh), BF), jax.ShapeDtypeStruct((batch, t_all, nh), BF),
                   jax.ShapeDtypeStruct((batch, t_all, MLA_HEADS * MLA_V), BF)],
        compiler_params=_cparams(("parallel", "parallel")),
        name="mla_proj",
    )(h, mod, n1g, win, qng, kvng, wuq, wuk, wuv, gq, gk, cos, sin, avg)

    return pl.pallas_call(
        _mla_attn_kernel,
        grid=(batch, MLA_HEADS // 2),
        in_specs=[pl.BlockSpec((None, t_all, 2 * LANES), lambda b, p: (b, 0, p)),
                  pl.BlockSpec((None, t_all, 2 * LANES), lambda b, p: (b, 0, p)),
                  pl.BlockSpec((None, t_all, LANES), lambda b, p: (b, 0, p))],
        out_specs=pl.BlockSpec((None, t_all, LANES), lambda b, p: (b, 0, p)),
        out_shape=jax.ShapeDtypeStruct((batch, t_all, MLA_HEADS * MLA_V), BF),
        compiler_params=_cparams(("parallel", "parallel")),
        name="mla_attn",
    )(q, k, v)


ROWS_PER_ISSUE = 8


def _row_copy(src, src_row, dst, dst_row, sem):
    return pltpu.make_async_copy(src.at[pl.ds(src_row, 1), :], dst.at[pl.ds(dst_row, 1), :], sem)


def _dispatch_kernel(ztile_ref, dest_ref, x_ref, xs_hbm, zbuf, zsem, sem):
    def zero_copy(e):
        start = pl.multiple_of(ztile_ref[e], MOE_TM)
        return pltpu.make_async_copy(zbuf, xs_hbm.at[pl.ds(start, MOE_TM), :], zsem)

    @pl.when(pl.program_id(0) == 0)
    def _():
        zbuf[...] = jnp.zeros_like(zbuf)
        for e in range(2 * MOE_EXPERTS):
            @pl.when(ztile_ref[e] >= 0)
            def _():
                zero_copy(e).start()
        for e in range(2 * MOE_EXPERTS):
            @pl.when(ztile_ref[e] >= 0)
            def _():
                zero_copy(e).wait()

    def issue(blk, carry):
        for u in range(ROWS_PER_ISSUE):
            r = blk * ROWS_PER_ISSUE + u
            for k in range(2):
                _row_copy(x_ref, r, xs_hbm, dest_ref[0, 2 * r + k], sem).start(priority=k)
        return carry

    lax.fori_loop(0, TM // ROWS_PER_ISSUE, issue, 0)
    for k in range(2):
        pltpu.make_async_copy(x_ref, xs_hbm.at[pl.ds(0, TM), :], sem).wait()


def _moe_kernel(te_ref, nt_ref, x_ref, w1_ref, w3_ref, w2_ref, o_ref, w1b, w3b, w2b):
    i = pl.program_id(0)
    prev = te_ref[jnp.maximum(i - 1, 0)]

    @pl.when(jnp.logical_or(i == 0, te_ref[i] != prev))
    def _():
        w1b[...] = w1_ref[...].astype(BF)
        w3b[...] = w3_ref[...].astype(BF)
        w2b[...] = w2_ref[...].astype(BF)

    @pl.when(i < nt_ref[0])
    def _():
        x = x_ref[...].astype(BF)
        h1 = jnp.dot(x, w1b[...], preferred_element_type=F32)
        h3 = jnp.dot(x, w3b[...], preferred_element_type=F32)
        hid = (_silu(h1) * h3).astype(BF)
        o_ref[...] = jnp.dot(hid, w2b[...], preferred_element_type=F32)

    @pl.when(i >= nt_ref[0])
    def _():
        o_ref[...] = jnp.zeros_like(o_ref)


def _combine_kernel(dest_ref, dnext_ref, out_hbm, h_ref, mod_ref, wt_ref, ho_ref, buf, sem, *, n_steps):
    i = pl.program_id(0)
    slot = i % 2

    def fetch(d_ref, s):
        def issue(blk, carry):
            for u in range(ROWS_PER_ISSUE):
                r = blk * ROWS_PER_ISSUE + u
                for k in range(2):
                    _row_copy(out_hbm, d_ref[0, 2 * r + k], buf.at[s], k * TM + r, sem.at[s]).start(priority=k)
            return carry

        lax.fori_loop(0, TM // ROWS_PER_ISSUE, issue, 0)

    @pl.when(i == 0)
    def _():
        fetch(dest_ref, 0)

    @pl.when(i + 1 < n_steps)
    def _():
        fetch(dnext_ref, 1 - slot)

    pltpu.make_async_copy(out_hbm.at[pl.ds(0, 2 * TM), :], buf.at[slot], sem.at[slot]).wait()
    w = wt_ref[...]
    y = buf[slot, 0:TM, :] * w[:, 0:1] + buf[slot, TM:2 * TM, :] * w[:, 1:2]
    ho_ref[...] = h_ref[...] + mod_ref[...][5:6] * y


def _moe(h, mod, xn, ids, wts, cnt, w1, w3, w2, layer, latent_only):
    batch, t_all, _ = xn.shape
    n_tok = batch * t_all
    tiles_b = t_all // TM
    n_tt = n_tok // TM

    lat_b = tiles_b - 1
    n_steps = batch * lat_b if latent_only else n_tt
    n_pair = 2 * n_steps * TM
    n_tiles = (n_pair + MOE_EXPERTS * (MOE_TM - 1) + MOE_TM - 1) // MOE_TM

    def tile_of(i):
        return (i // lat_b) * tiles_b + 1 + i % lat_b if latent_only else i

    def next_tile_of(i):
        return tile_of(jnp.minimum(i + 1, n_steps - 1))

    cnt = cnt.reshape(n_tt, LANES)[:, :MOE_EXPERTS].astype(jnp.int32)
    if latent_only:
        cnt = jnp.where((jnp.arange(n_tt, dtype=jnp.int32) % tiles_b == 0)[:, None], 0, cnt)
    tiles_e = (jnp.sum(cnt, axis=0) + MOE_TM - 1) // MOE_TM
    tile_end = jnp.cumsum(tiles_e)
    tile_start = tile_end - tiles_e
    before = jnp.cumsum(cnt, axis=0) - cnt
    base = tile_start[None, :] * MOE_TM + before
    n_valid = tile_end[-1]
    tidx = jnp.arange(n_tiles, dtype=jnp.int32)
    te = jnp.minimum(jnp.sum((tile_end[None, :] <= tidx[:, None]).astype(jnp.int32), axis=1), MOE_EXPERTS - 1)
    te = jnp.where(tidx < n_valid, te, jnp.max(jnp.where(tiles_e > 0, jnp.arange(MOE_EXPERTS), 0))).astype(jnp.int32)
    assert n_tiles * MOE_TM - n_pair <= MOE_EXPERTS * MOE_TM
    tail = n_valid + jnp.arange(MOE_EXPERTS, dtype=jnp.int32)
    ztile = jnp.concatenate([jnp.where(tiles_e > 0, (tile_end - 1) * MOE_TM, -1),
                             jnp.where(tail < n_tiles, tail * MOE_TM, -1)]).astype(jnp.int32)

    picks = ids.reshape(n_tt, TM, LANES)
    expert = picks[:, :, 0:2].reshape(n_tt, 1, 2 * TM)
    rank = picks[:, :, 2:4].reshape(n_tt, 1, 2 * TM)
    is_pick = expert == jnp.arange(MOE_EXPERTS, dtype=jnp.int32)[None, :, None]
    dest = jnp.sum(jnp.where(is_pick, base[:, :, None], 0), axis=1, keepdims=True) + rank
    dest_spec = pl.BlockSpec((None, 1, 2 * TM), lambda i, *_: (tile_of(i), 0, 0), memory_space=pltpu.SMEM)

    xs = pl.pallas_call(
        _dispatch_kernel,
        grid_spec=pltpu.PrefetchScalarGridSpec(
            num_scalar_prefetch=1,
            grid=(n_steps,),
            in_specs=[dest_spec, pl.BlockSpec((TM, D), lambda i, zt: (tile_of(i), 0))],
            out_specs=pl.BlockSpec(memory_space=pl.ANY),
            scratch_shapes=[pltpu.VMEM((MOE_TM, D), F32), pltpu.SemaphoreType.DMA(()), pltpu.SemaphoreType.DMA(())],
        ),
        out_shape=jax.ShapeDtypeStruct((n_tiles * MOE_TM, D), F32),
        compiler_params=_cparams(("arbitrary",)),
        name="moe_dispatch",
    )(ztile, dest, xn.reshape(n_tok, D))

    out = pl.pallas_call(
        _moe_kernel,
        grid_spec=pltpu.PrefetchScalarGridSpec(
            num_scalar_prefetch=2,
            grid=(n_tiles,),
            in_specs=[pl.BlockSpec((MOE_TM, D), lambda i, te, nt: (i, 0)),
                      pl.BlockSpec((None, None, D, MOE_FF), lambda i, te, nt: (layer, te[i], 0, 0)),
                      pl.BlockSpec((None, None, D, MOE_FF), lambda i, te, nt: (layer, te[i], 0, 0)),
                      pl.BlockSpec((None, None, MOE_FF, D), lambda i, te, nt: (layer, te[i], 0, 0))],
            out_specs=pl.BlockSpec((MOE_TM, D), lambda i, te, nt: (i, 0)),
            scratch_shapes=[pltpu.VMEM((D, MOE_FF), BF), pltpu.VMEM((D, MOE_FF), BF), pltpu.VMEM((MOE_FF, D), BF)],
        ),
        out_shape=jax.ShapeDtypeStruct((n_tiles * MOE_TM, D), F32),
        compiler_params=_cparams(("arbitrary",)),
        name="moe_experts",
    )(te, n_valid.reshape(1).astype(jnp.int32), xs, w1, w3, w2)

    h_new = pl.pallas_call(
        functools.partial(_combine_kernel, n_steps=n_steps),
        grid=(n_steps,),
        in_specs=[pl.BlockSpec((None, 1, 2 * TM), lambda i: (tile_of(i), 0, 0), memory_space=pltpu.SMEM),
                  pl.BlockSpec((None, 1, 2 * TM), lambda i: (next_tile_of(i), 0, 0), memory_space=pltpu.SMEM),
                  pl.BlockSpec(memory_space=pl.ANY),
                  pl.BlockSpec((TM, D), lambda i: (tile_of(i), 0)),
                  pl.BlockSpec((None, None, 8, D),
                               lambda i: (tile_of(i) // tiles_b, jnp.minimum(tile_of(i) % tiles_b, 1), 0, 0)),
                  pl.BlockSpec((TM, LANES), lambda i: (tile_of(i), 0))],
        out_specs=pl.BlockSpec((TM, D), lambda i: (i, 0)),
        out_shape=jax.ShapeDtypeStruct((n_steps * TM, D), F32),
        scratch_shapes=[pltpu.VMEM((2, 2 * TM, D), F32), pltpu.SemaphoreType.DMA((2,))],
        compiler_params=_cparams(("arbitrary",)),
        name="moe_combine",
    )(dest, dest, out, h.reshape(n_tok, D), mod, wts.reshape(n_tok, LANES))
    return h_new.reshape(batch, n_steps * TM // batch, D)


def kernel(x, c, ctx, c_ctx, ada_w, ada_b, norm1_g, norm2_g, da_w_in, da_w_out, da_q_g, da_k_g, da_lam_q1, da_lam_k1, da_lam_q2, da_lam_k2, da_sub_g, gm_w_in, gm_v_g, gm_w_s, gm_b_s, gm_w_out, ssm_w_in, ssm_conv_w, ssm_conv_b, ssm_dt_bias, ssm_a_log, ssm_d, ssm_out_g, ssm_w_out, mla_w_in, mla_q_norm_g, mla_kv_norm_g, mla_w_uq, mla_w_ukv, mla_q_g, mla_k_g, mla_w_out, moe_w_group, moe_b_group, moe_w_router, moe_b_router, moe_w1, moe_w3, moe_w2):
    batch, seq, _ = x.shape
    assert ctx.shape[1] == CTX and seq % TM == 0 and seq % GRID_W == 0
    h = jnp.concatenate([ctx, x], axis=1)

    rows = ((batch + 1 + 7) // 8) * 8
    cc = jnp.concatenate([c, c_ctx[None, :], jnp.zeros((rows - batch - 1, D), F32)], axis=0)
    ada = _ada_all(cc, ada_w, ada_b)
    mod_l = ada[:, :batch].reshape(DEPTH, batch, 6, D)
    mod_c = jnp.broadcast_to(ada[:, batch].reshape(DEPTH, 1, 6, D), (DEPTH, batch, 6, D))
    mod_all = jnp.stack([mod_c, mod_l], axis=2)
    mod_all = jnp.concatenate([mod_all, jnp.zeros((DEPTH, batch, 2, 2, D), F32)], axis=3)

    pad = LANES - MOE_EXPERTS - MOE_GROUPS
    for i in range(DEPTH):
        kind = i % 4
        mod = mod_all[i]
        n1g = norm1_g[i].reshape(1, D)
        n2g = norm2_g[i].reshape(1, D)
        wr = jnp.concatenate([moe_w_router[i], moe_w_group[i], jnp.zeros((D, pad), F32)], axis=1).astype(BF)
        br = jnp.concatenate([moe_b_router[i], moe_b_group[i], jnp.zeros((pad,), F32)]).reshape(1, LANES)
        if kind == 0:
            a = _diff_attention(h, mod, n1g, da_w_in[0], da_q_g[0], da_k_g[0], da_lam_q1[0], da_lam_k1[0],
                                da_lam_q2[0], da_lam_k2[0], da_sub_g[0], i)
            h, xn, ids, wts, cnt =_outproj(a, da_w_out[0].astype(BF), h, mod, n2g, wr, br)
        elif kind == 1:
            h, xn, ids, wts, cnt =_gmlp_layer(h, mod, n1g, gm_w_in[0], gm_v_g[0], gm_w_s[0], gm_b_s[0], gm_w_out[0],
                                          n2g, wr, br)
        elif kind == 2:
            h, xn, ids, wts, cnt =_mamba_layer(h, mod, n1g, ssm_w_in[0], ssm_conv_w[0], ssm_conv_b[0], ssm_dt_bias[0],
                                           ssm_a_log[0], ssm_d[0], ssm_out_g[0], ssm_w_out[0], n2g, wr, br)
        else:
            a = _mla_attention(h, mod, n1g, mla_w_in[0], mla_q_norm_g[0], mla_kv_norm_g[0], mla_w_uq[0],
                               mla_w_ukv[0], mla_q_g[0], mla_k_g[0])
            h, xn, ids, wts, cnt =_outproj(a, mla_w_out[0].astype(BF), h, mod, n2g, wr, br)
        h = _moe(h, mod, xn, ids, wts, cnt, moe_w1, moe_w3, moe_w2, i, latent_only=(i == DEPTH - 1))
    return h
```

```python
import functools
import math

import numpy as np
import jax
import jax.numpy as jnp
from jax import lax
from jax.experimental import pallas as pl
from jax.experimental.pallas import tpu as pltpu

F32 = jnp.float32
BF = jnp.bfloat16
HIGHEST = lax.Precision.HIGHEST

D = 1024
CTX = 256
GRID_W = 64
EPS = 1e-6
ROPE_BASE = 10000.0
DEPTH = 4
LANES = 128
TM = 256

DA_HEADS = 8
DA_HEAD_DIM = 64

GM_CHUNK = 128
GM_GROUPS = 8

SSM_INNER = 2048
SSM_HEADS = 32
SSM_GROUPS = 4
SSM_STATE = 128
SSM_CONV = 5
SSM_CHUNK = 128
SSM_CONV_DIM = SSM_INNER + 2 * SSM_GROUPS * SSM_STATE

MLA_HEADS = 16
MLA_Q_RANK = 384
MLA_KV_RANK = 256
MLA_NOPE = 64
MLA_ROPE = 32
MLA_V = 64
MLA_QK = MLA_NOPE + MLA_ROPE

MOE_GROUPS = 4
MOE_PER_GROUP = 8
MOE_EXPERTS = 32
MOE_FF = 512
MOE_TM = 256

VMEM_LIMIT = 56 * 1024 * 1024


def _cparams(sem):
    return pltpu.CompilerParams(dimension_semantics=sem, vmem_limit_bytes=VMEM_LIMIT)


def _full_spec(arr):
    nd = arr.ndim
    return pl.BlockSpec(arr.shape, lambda *_: (0,) * nd)


def _tok_spec(width, col=0):
    return pl.BlockSpec((None, TM, width), lambda b, t: (b, t, col))


def _mod_spec():
    return pl.BlockSpec((None, None, 8, D), lambda b, t: (b, jnp.minimum(t, 1), 0, 0))


def _silu(x):
    return x * (1.0 / (1.0 + jnp.exp(-x)))


def _norm_mod(h, g, scale, shift):
    ms = jnp.mean(h * h, axis=-1, keepdims=True)
    y = h * lax.rsqrt(ms + EPS) * g
    return y * (1.0 + scale) + shift


def _ada_kernel(c_ref, w_ref, b_ref, o_ref):
    a = _silu(c_ref[...]).astype(BF)
    o_ref[...] = jnp.dot(a, w_ref[...].astype(BF), preferred_element_type=F32) + b_ref[...]


def _ada_all(cc, ada_w, ada_b):
    rows = cc.shape[0]
    tn = 1536
    return pl.pallas_call(
        _ada_kernel,
        grid=(DEPTH, 6 * D // tn),
        in_specs=[pl.BlockSpec((rows, D), lambda l, j: (0, 0)),
                  pl.BlockSpec((None, D, tn), lambda l, j: (l, 0, j)),
                  pl.BlockSpec((None, 1, tn), lambda l, j: (l, 0, j))],
        out_specs=pl.BlockSpec((None, rows, tn), lambda l, j: (l, 0, j)),
        out_shape=jax.ShapeDtypeStruct((DEPTH, rows, 6 * D), F32),
        compiler_params=_cparams(("arbitrary", "arbitrary")),
        name="ada_mod",
    )(cc, ada_w, ada_b.reshape(DEPTH, 1, 6 * D))


def _residual_and_route(delta, h_ref, mod, n2g_ref, wr_ref, br_ref, outs):
    ho_ref, xo_ref, id_ref, wt_ref, cnt_ref = outs
    hn = h_ref[...] + mod[2:3] * delta
    ho_ref[...] = hn
    x2 = _norm_mod(hn, n2g_ref[...], mod[4:5], mod[3:4])
    xo_ref[...] = x2
    lg = jnp.dot(x2.astype(BF), wr_ref[...], preferred_element_type=F32) + br_ref[...]
    lane = lax.broadcasted_iota(jnp.int32, lg.shape, 1)
    neg = jnp.float32(-jnp.inf)
    big = jnp.int32(1 << 20)
    is_g = jnp.logical_and(lane >= MOE_EXPERTS, lane < MOE_EXPERTS + MOE_GROUPS)
    gl = jnp.where(is_g, lg, neg)
    gmax = jnp.max(gl, axis=-1, keepdims=True)
    gidx = jnp.min(jnp.where(gl == gmax, lane, big), axis=-1, keepdims=True) - MOE_EXPERTS
    pg = 1.0 / jnp.sum(jnp.where(is_g, jnp.exp(gl - gmax), 0.0), axis=-1, keepdims=True)
    lo = gidx * MOE_PER_GROUP
    in_grp = jnp.logical_and(lane >= lo, lane < lo + MOE_PER_GROUP)
    el = jnp.where(in_grp, lg, neg)
    e1 = jnp.max(el, axis=-1, keepdims=True)
    i1 = jnp.min(jnp.where(el == e1, lane, big), axis=-1, keepdims=True)
    el2 = jnp.where(lane == i1, neg, el)
    e2 = jnp.max(el2, axis=-1, keepdims=True)
    i2 = jnp.min(jnp.where(el2 == e2, lane, big), axis=-1, keepdims=True)
    t = jnp.exp(e2 - e1)
    w1 = pg / (1.0 + t)
    w2 = pg * t / (1.0 + t)
    wt_ref[...] = jnp.where(lane == 0, w1, jnp.where(lane == 1, w2, 0.0))
    sel1 = lane == i1
    sel2 = lane == i2
    chosen = jnp.where(jnp.logical_or(sel1, sel2), 1.0, 0.0)
    cnt_ref[...] = jnp.sum(chosen, axis=0, keepdims=True)
    rows = lg.shape[0]
    r = lax.broadcasted_iota(jnp.int32, (rows, rows), 0)
    c = lax.broadcasted_iota(jnp.int32, (rows, rows), 1)
    earlier = jnp.where(c < r, 1.0, 0.0).astype(BF)
    before = jnp.dot(earlier, chosen.astype(BF), preferred_element_type=F32)
    rank1 = jnp.sum(jnp.where(sel1, before, 0.0), axis=-1, keepdims=True).astype(jnp.int32)
    rank2 = jnp.sum(jnp.where(sel2, before, 0.0), axis=-1, keepdims=True).astype(jnp.int32)
    id_ref[...] = jnp.where(lane == 0, i1, jnp.where(lane == 1, i2,
                            jnp.where(lane == 2, rank1, jnp.where(lane == 3, rank2, 0))))


def _tail_in_specs(n2g, wr, br):
    return [_full_spec(n2g), _full_spec(wr), _full_spec(br)]


def _tail_out(batch, t_all):
    specs = [_tok_spec(D), _tok_spec(D), _tok_spec(LANES), _tok_spec(LANES),
             pl.BlockSpec((None, None, 1, LANES), lambda b, t: (b, t, 0, 0))]
    shapes = [jax.ShapeDtypeStruct((batch, t_all, D), F32),
              jax.ShapeDtypeStruct((batch, t_all, D), F32),
              jax.ShapeDtypeStruct((batch, t_all, LANES), jnp.int32),
              jax.ShapeDtypeStruct((batch, t_all, LANES), F32),
              jax.ShapeDtypeStruct((batch, t_all // TM, 1, LANES), F32)]
    return specs, shapes


def _outproj_kernel(a_ref, w_ref, h_ref, mod_ref, n2g_ref, wr_ref, br_ref, *outs):
    delta = jnp.dot(a_ref[...], w_ref[...], preferred_element_type=F32)
    _residual_and_route(delta, h_ref, mod_ref[...], n2g_ref, wr_ref, br_ref, outs)


def _outproj(a, w, h, mod, n2g, wr, br):
    batch, t_all, k = a.shape
    out_specs, out_shapes = _tail_out(batch, t_all)
    return pl.pallas_call(
        _outproj_kernel,
        grid=(batch, t_all // TM),
        in_specs=[_tok_spec(k), _full_spec(w), _tok_spec(D), _mod_spec()] + _tail_in_specs(n2g, wr, br),
        out_specs=out_specs,
        out_shape=out_shapes,
        compiler_params=_cparams(("parallel", "parallel")),
        name="outproj_route",
    )(a, w, h, mod, n2g, wr, br)


def _rope_tables(t_lat, rot_dim):
    rows = t_lat // GRID_W
    row = np.repeat(np.arange(rows, dtype=np.float32), GRID_W)
    col = np.tile(np.arange(GRID_W, dtype=np.float32), rows)
    n_freq = rot_dim // 4
    inv_freq = jnp.asarray(ROPE_BASE, F32) ** (-jnp.arange(n_freq, dtype=F32) / n_freq)
    ang = jnp.concatenate([jnp.asarray(row)[:, None] * inv_freq, jnp.asarray(col)[:, None] * inv_freq], axis=-1)
    return jnp.cos(ang), jnp.sin(ang)


def _with_ctx_identity(cos_l, sin_l):
    cos = jnp.concatenate([jnp.ones((CTX, LANES), F32), cos_l], axis=0)
    sin = jnp.concatenate([jnp.zeros((CTX, LANES), F32), sin_l], axis=0)
    return cos, sin


def _da_perm():
    perm = np.zeros(2 * DA_HEADS * DA_HEAD_DIM, np.int32)
    for h in range(DA_HEADS):
        for m in range(2):
            for i in range(DA_HEAD_DIM):
                new = h * LANES + (i // 32) * 64 + m * 32 + (i % 32)
                perm[new] = (2 * h + m) * DA_HEAD_DIM + i
    return perm


def _da_lane_of_dim():
    lane = np.arange(LANES)
    return (lane // 64) * 32 + lane % 32


def _da_proj_kernel(c_ref, x_ref, mod_ref, n1g_ref, w_ref, gq_ref, gk_ref, cos_ref, sin_ref, gavg_ref,
                    q_ref, k_ref, v_ref, h_ref):
    @pl.when(pl.program_id(1) == 0)
    def _():
        h_ref[...] = c_ref[...]

    @pl.when(pl.program_id(1) > 0)
    def _():
        h_ref[...] = x_ref[...]

    mod = mod_ref[...]
    xn = _norm_mod(h_ref[...], n1g_ref[...], mod[1:2], mod[0:1]).astype(BF)
    cos = cos_ref[...]
    sin = sin_ref[...]
    gavg = gavg_ref[...]
    nq = DA_HEADS * LANES
    scale = DA_HEAD_DIM ** -0.5 * LOG2E
    for dst, off, g_ref, mult in ((q_ref, 0, gq_ref, scale), (k_ref, nq, gk_ref, 1.0)):
        y_all = jnp.dot(xn, w_ref[:, off:off + nq], preferred_element_type=F32)
        for jj in range(DA_HEADS // 2):
            yy = y_all[:, jj * 2 * LANES:(jj + 1) * 2 * LANES]
            ms2 = jnp.dot((yy * yy).astype(BF), gavg, preferred_element_type=F32)
            for j in (2 * jj, 2 * jj + 1):
                y = y_all[:, j * LANES:(j + 1) * LANES]
                ms = ms2[:, (j % 2) * LANES:(j % 2 + 1) * LANES]
                y = y * lax.rsqrt(ms + EPS) * g_ref[...]
                y = y * cos + pltpu.roll(y, 64, 1) * sin
                dst[:, j * LANES:(j + 1) * LANES] = (y * mult).astype(BF)
    v_ref[...] = jnp.dot(xn, w_ref[:, 2 * nq:3 * nq], preferred_element_type=F32).astype(BF)


LOG2E = math.log2(math.e)


def _softmax_pv(q, k, v):
    s = lax.dot_general(q, k, (((1,), (1,)), ((), ())), preferred_element_type=F32)
    e = jnp.exp2(s - jnp.max(s, axis=-1, keepdims=True))
    l = jnp.sum(e, axis=-1, keepdims=True)
    return jnp.dot(e.astype(BF), v, preferred_element_type=F32) / l


ATTN_TILES_PER_ITER = 8


def _for_each_latent_tile(tile_fn, t_all):
    n_lat = (t_all - CTX) // TM
    per = ATTN_TILES_PER_ITER if n_lat % ATTN_TILES_PER_ITER == 0 else 1

    def body(j, carry):
        for u in range(per):
            tile_fn(pl.multiple_of(CTX + (j * per + u) * TM, TM), t_all)
        return carry

    lax.fori_loop(0, n_lat // per, body, 0)


def _da_attn_kernel(lam_ref, q_ref, k_ref, v_ref, subg_ref, o_ref, *, lam_init):
    t_all = q_ref.shape[0]
    lane = lax.broadcasted_iota(jnp.int32, (TM, LANES), 1)
    comp0 = ((lane // 32) % 2) == 0
    zero = jnp.zeros((TM, LANES), BF)

    def tile(row0, n_keys):
        q = q_ref[pl.ds(row0, TM), :]
        k = k_ref[0:n_keys, :]
        v = v_ref[0:n_keys, :]
        o = _softmax_pv(jnp.where(comp0, q, zero), k, v) - lam_ref[0] * _softmax_pv(jnp.where(comp0, zero, q), k, v)
        ms = jnp.mean(o * o, axis=-1, keepdims=True)
        o = o * lax.rsqrt(ms + EPS) * subg_ref[...] * (1.0 - lam_init)
        o_ref[pl.ds(row0, TM), :] = o.astype(BF)

    tile(0, CTX)

    _for_each_latent_tile(tile, t_all)


def _diff_attention(ctx, x, mod, n1g, w_in, q_g, k_g, lam_q1, lam_k1, lam_q2, lam_k2, sub_g, layer_idx):
    batch = x.shape[0]
    t_all = CTX + x.shape[1]
    nq = DA_HEADS * LANES
    perm = _da_perm()
    w = jnp.concatenate([w_in[:, :nq][:, perm], w_in[:, nq:2 * nq][:, perm], w_in[:, 2 * nq:]], axis=1).astype(BF)
    ldim = _da_lane_of_dim()
    gq = q_g[ldim].reshape(1, LANES)
    gk = k_g[ldim].reshape(1, LANES)
    cos, sin = _rope_tables(t_all - CTX, DA_HEAD_DIM)
    cos, sin = _with_ctx_identity(jnp.tile(cos, (1, 4)), jnp.concatenate([-sin, -sin, sin, sin], axis=1))
    unit = np.arange(2 * LANES) // LANES * 2 + (np.arange(2 * LANES) // 32) % 2
    gavg = jnp.asarray((unit[:, None] == unit[None, :]).astype(np.float32) / DA_HEAD_DIM).astype(BF)
    tab_spec = pl.BlockSpec((TM, LANES), lambda b, t: (t, 0))
    qkv_shape = jax.ShapeDtypeStruct((batch, t_all, nq), BF)
    q, k, v, h = pl.pallas_call(
        _da_proj_kernel,
        grid=(batch, t_all // TM),
        in_specs=[pl.BlockSpec((None, CTX, D), lambda b, t: (b, 0, 0)),
                  pl.BlockSpec((None, TM, D), lambda b, t: (b, jnp.maximum(t - 1, 0), 0)),
                  _mod_spec(), _full_spec(n1g), _full_spec(w), _full_spec(gq), _full_spec(gk),
                  tab_spec, tab_spec, _full_spec(gavg)],
        out_specs=[_tok_spec(nq)] * 3 + [_tok_spec(D)],
        out_shape=[qkv_shape] * 3 + [jax.ShapeDtypeStruct((batch, t_all, D), F32)],
        compiler_params=_cparams(("parallel", "parallel")),
        name="da_proj",
    )(ctx, x, mod, n1g, w, gq, gk, cos, sin, gavg)

    lam_init = 0.8 - 0.6 * math.exp(-0.3 * layer_idx)
    lam = (jnp.exp(jnp.sum(lam_q1 * lam_k1)) - jnp.exp(jnp.sum(lam_q2 * lam_k2)) + lam_init).reshape(1)
    seq_spec = pl.BlockSpec((None, t_all, LANES), lambda b, hh: (b, 0, hh))
    subg = sub_g.reshape(1, LANES)
    return h, pl.pallas_call(
        functools.partial(_da_attn_kernel, lam_init=lam_init),
        grid=(batch, DA_HEADS),
        in_specs=[pl.BlockSpec(memory_space=pltpu.SMEM), seq_spec, seq_spec, seq_spec,
                  pl.BlockSpec((1, LANES), lambda b, hh: (0, 0))],
        out_specs=seq_spec,
        out_shape=jax.ShapeDtypeStruct((batch, t_all, nq), BF),
        compiler_params=_cparams(("parallel", "parallel")),
        name="da_attn",
    )(lam, q, k, v, subg)


def _gmlp_kernel(h_ref, mod_ref, n1g_ref, win_ref, vg_ref, ws_ref, bs_ref, wout_ref,
                 n2g_ref, wr_ref, br_ref, *outs):
    mod = mod_ref[...]
    xn = _norm_mod(h_ref[...], n1g_ref[...], mod[1:2], mod[0:1]).astype(BF)
    uv =jnp.dot(xn, win_ref[...], preferred_element_type=F32)
    uv = 0.5 * uv * (1.0 + lax.erf(uv * (2.0 ** -0.5)))
    u = uv[:, :D]
    v = uv[:, D:]
    v = (v * lax.rsqrt(jnp.mean(v * v, axis=-1, keepdims=True) + EPS) * vg_ref[...]).astype(BF)
    rows = []
    for c in range(TM // GM_CHUNK):
        cols = []
        for g in range(GM_GROUPS):
            vb = v[c * GM_CHUNK:(c + 1) * GM_CHUNK, g * LANES:(g + 1) * LANES]
            s = jnp.dot(ws_ref[g], vb, preferred_element_type=F32) + bs_ref[g]
            cols.append(s)
        rows.append(jnp.concatenate(cols, axis=1))
    s_all = jnp.concatenate(rows, axis=0)
    delta = jnp.dot((u * s_all).astype(BF), wout_ref[...], preferred_element_type=F32)
    _residual_and_route(delta, h_ref, mod, n2g_ref, wr_ref, br_ref, outs)


def _gmlp_layer(h, mod, n1g, w_in, v_g, w_s, b_s, w_out, n2g, wr, br):
    batch, t_all, _ = h.shape
    win = w_in.astype(BF)
    wout = w_out.astype(BF)
    ws = w_s.astype(BF)
    bs = jnp.broadcast_to(b_s[:, :, None], (GM_GROUPS, GM_CHUNK, LANES)).astype(F32)
    vg = v_g.reshape(1, D)
    out_specs, out_shapes = _tail_out(batch, t_all)
    return pl.pallas_call(
        _gmlp_kernel,
        grid=(batch, t_all // TM),
        in_specs=[_tok_spec(D), _mod_spec(), _full_spec(n1g), _full_spec(win), _full_spec(vg), _full_spec(ws),
                  _full_spec(bs), _full_spec(wout)] + _tail_in_specs(n2g, wr, br),
        out_specs=out_specs,
        out_shape=out_shapes,
        compiler_params=_cparams(("parallel", "parallel")),
        name="gmlp_layer",
    )(h, mod, n1g, win, vg, ws, bs, wout, n2g, wr, br)


SSM_DT_PAD = LANES


def _ssm_proj_kernel(h_ref, mod_ref, n1g_ref, w_ref, dtb_ref, z_ref, xbc_ref, dt_ref):
    mod = mod_ref[...]
    xn = _norm_mod(h_ref[...], n1g_ref[...], mod[1:2], mod[0:1]).astype(BF)
    z_ref[...] = jnp.dot(xn, w_ref[:, :SSM_INNER], preferred_element_type=F32).astype(BF)
    c0 = SSM_INNER
    for j in range(SSM_CONV_DIM // 1024):
        xbc_ref[:, j * 1024:(j + 1) * 1024] = jnp.dot(
            xn, w_ref[:, c0 + j * 1024:c0 + (j + 1) * 1024], preferred_element_type=F32).astype(BF)
    c1 = SSM_INNER + SSM_CONV_DIM
    raw = jnp.dot(xn, w_ref[:, c1:c1 + SSM_DT_PAD], preferred_element_type=F32) + dtb_ref[...]
    dt_ref[...] = jnp.maximum(raw, 0.0) + jnp.log1p(jnp.exp(-jnp.abs(raw)))


CONV_PAD = 8
CONV_W = 512


def _conv_kernel(x_ref, w_ref, b_ref, o_ref, pad_ref):
    t_all = x_ref.shape[0]
    t_lat = t_all - CTX
    lat0 = CTX + 2 * CONV_PAD
    zeros = jnp.zeros((CONV_PAD, CONV_W), F32)
    pad_ref[0:CONV_PAD, :] = zeros
    pad_ref[CONV_PAD + CTX:lat0, :] = jnp.zeros((CONV_PAD, CONV_W), F32)
    pad_ref[lat0 + t_lat:lat0 + t_lat + CONV_PAD, :] = zeros
    pad_ref[CONV_PAD:CONV_PAD + CTX, :] = x_ref[0:CTX, :].astype(F32)
    pad_ref[lat0:lat0 + t_lat, :] = x_ref[CTX:t_all, :].astype(F32)
    half = (SSM_CONV - 1) // 2
    padded = pad_ref[...]
    rows = padded.shape[0]
    taps = [padded if kk == half else pltpu.roll(padded, (half - kk) % rows, 0) for kk in range(SSM_CONV)]
    for start, n, dst in ((CONV_PAD, CTX, 0), (lat0, t_lat, CTX)):
        acc = jnp.zeros((n, CONV_W), F32) + b_ref[...]
        for kk in range(SSM_CONV):
            acc = acc + taps[kk][start:start + n, :] * w_ref[kk:kk + 1, :]
        o_ref[dst:dst + n, :] = _silu(acc).astype(BF)


def _ssd_kernel(dt_ref, a_ref, x_ref, b_ref, c_ref, y_ref, s_ref):
    direction = pl.program_id(1)
    step = pl.program_id(2)

    @pl.when(step == 0)
    def _():
        s_ref[...] = jnp.zeros_like(s_ref)

    q = SSM_CHUNK
    row = lax.broadcasted_iota(jnp.int32, (q, q), 0)
    col = lax.broadcasted_iota(jnp.int32, (q, q), 1)
    sign = 1 - 2 * direction
    tri_b = (col - row) * sign <= 0
    tri = tri_b.astype(F32)
    tri_t = ((row - col) * sign <= 0).astype(F32)
    eye = (row == col).astype(F32)
    lane_lo = col < 64

    dt_t = dt_ref[...]
    a_t = dt_t * a_ref[...]
    cs_rows = jnp.dot(a_t, tri_t, preferred_element_type=F32, precision=HIGHEST)
    tot = jnp.sum(a_t, axis=1, keepdims=True)
    heads_per_group = SSM_HEADS // SSM_GROUPS

    for g in range(SSM_GROUPS):
        bm = b_ref[:, g * SSM_STATE:(g + 1) * SSM_STATE]
        cm = c_ref[:, g * SSM_STATE:(g + 1) * SSM_STATE]
        cb = lax.dot_general(cm, bm, (((1,), (1,)), ((), ())), preferred_element_type=F32)
        bm_t = bm.astype(F32).T.astype(BF)
        for pp in range(heads_per_group // 2):
            p = g * (heads_per_group // 2) + pp
            xp = x_ref[:, p * LANES:(p + 1) * LANES].astype(F32)
            cs_col, dt_col, mats = [], [], []
            for hh in range(2):
                e = 2 * p + hh
                cc = jnp.sum(tri * a_t[e:e + 1, :], axis=1, keepdims=True)
                cs_col.append(cc)
                dt_col.append(jnp.sum(eye * dt_t[e:e + 1, :], axis=1, keepdims=True))
                decay = jnp.exp(jnp.where(tri_b, cc - cs_rows[e:e + 1, :], -jnp.inf))
                mats.append((cb * decay).astype(BF))
            xdt = xp * jnp.where(lane_lo, dt_col[0], dt_col[1])
            xb = xdt.astype(BF)
            zb = jnp.zeros_like(xb)
            y_diag = jnp.dot(jnp.concatenate(mats, axis=1),
                             jnp.concatenate([jnp.where(lane_lo, xb, zb), jnp.where(lane_lo, zb, xb)], axis=0),
                             preferred_element_type=F32)
            s_prev = s_ref[p]
            y_off = jnp.dot(cm, s_prev.astype(BF), preferred_element_type=F32)
            y_off = y_off * jnp.exp(jnp.where(lane_lo, cs_col[0], cs_col[1]))
            y_ref[:, p * LANES:(p + 1) * LANES] = (y_diag + y_off).astype(BF)
            t0 = tot[2 * p:2 * p + 1, :]
            t1 = tot[2 * p + 1:2 * p + 2, :]
            to_end = jnp.exp(jnp.where(lane_lo, t0 - cs_col[0], t1 - cs_col[1]))
            upd = jnp.dot(bm_t, (xdt * to_end).astype(BF), preferred_element_type=F32)
            s_ref[p] = s_prev * jnp.exp(jnp.where(lane_lo[0:1, :], t0, t1)) + upd


def _ssm_finish_kernel(yf_ref, yb_ref, xs_ref, z_ref, d_ref, og_ref, w_ref, h_ref, mod_ref,
                       n2g_ref, wr_ref, br_ref, *outs):
    y = yf_ref[...].astype(F32) + yb_ref[...].astype(F32) + d_ref[...] * xs_ref[...].astype(F32)
    y = y * _silu(z_ref[...].astype(F32))
    y = y * lax.rsqrt(jnp.mean(y * y, axis=-1, keepdims=True) + EPS) * og_ref[...]
    delta = jnp.dot(y.astype(BF), w_ref[...], preferred_element_type=F32)
    _residual_and_route(delta, h_ref, mod_ref[...], n2g_ref, wr_ref, br_ref, outs)


def _mamba_layer(h, mod, n1g, w_in, conv_w, conv_b, dt_bias, a_log, d_skip, out_g, w_out, n2g, wr, br):
    batch, t_all, _ = h.shape
    n_dt = 2 * SSM_HEADS
    c1 = SSM_INNER + SSM_CONV_DIM
    w = jnp.concatenate([w_in, jnp.zeros((D, SSM_DT_PAD - n_dt), F32)], axis=1).astype(BF)
    dtb = jnp.concatenate([dt_bias.reshape(n_dt), jnp.zeros((SSM_DT_PAD - n_dt,), F32)]).reshape(1, SSM_DT_PAD)
    z, xbc, dt = pl.pallas_call(
        _ssm_proj_kernel,
        grid=(batch, t_all // TM),
        in_specs=[_tok_spec(D), _mod_spec(), _full_spec(n1g), _full_spec(w), _full_spec(dtb)],
        out_specs=[_tok_spec(SSM_INNER), _tok_spec(SSM_CONV_DIM), _tok_spec(SSM_DT_PAD)],
        out_shape=[jax.ShapeDtypeStruct((batch, t_all, SSM_INNER), BF),
                   jax.ShapeDtypeStruct((batch, t_all, SSM_CONV_DIM), BF),
                   jax.ShapeDtypeStruct((batch, t_all, SSM_DT_PAD), F32)],
        compiler_params=_cparams(("parallel", "parallel")),
        name="ssm_proj",
    )(h, mod, n1g, w, dtb)
    assert c1 + n_dt == w_in.shape[1]

    cw = jnp.concatenate([conv_w, jnp.zeros((8 - SSM_CONV, SSM_CONV_DIM), F32)], axis=0)
    cbias = conv_b.reshape(1, SSM_CONV_DIM)
    xbc = pl.pallas_call(
        _conv_kernel,
        grid=(batch, SSM_CONV_DIM // CONV_W),
        in_specs=[pl.BlockSpec((None, t_all, CONV_W), lambda b, j: (b, 0, j)),
                  pl.BlockSpec((8, CONV_W), lambda b, j: (0, j)),
                  pl.BlockSpec((1, CONV_W), lambda b, j: (0, j))],
        out_specs=pl.BlockSpec((None, t_all, CONV_W), lambda b, j: (b, 0, j)),
        out_shape=jax.ShapeDtypeStruct((batch, t_all, SSM_CONV_DIM), BF),
        scratch_shapes=[pltpu.VMEM((t_all + 3 * CONV_PAD, CONV_W), F32)],
        compiler_params=_cparams(("parallel", "parallel")),
        name="ssm_conv",
    )(xbc, cw, cbias)

    n_chunks = t_all // SSM_CHUNK
    ctx_chunks = CTX // SSM_CHUNK
    dt_t = jnp.swapaxes(dt[:, :, :n_dt], 1, 2)
    a_neg = jnp.broadcast_to(-jnp.exp(a_log.astype(F32)).reshape(2, SSM_HEADS, 1), (2, SSM_HEADS, SSM_CHUNK))

    def chunk_of(dr, s):
        rev = jnp.where(s < ctx_chunks, ctx_chunks - 1 - s, n_chunks - 1 + ctx_chunks - s)
        return jnp.where(dr == 0, s, rev)

    gs = SSM_GROUPS * SSM_STATE
    y = pl.pallas_call(
        _ssd_kernel,
        grid=(batch, 2, n_chunks),
        in_specs=[pl.BlockSpec((None, SSM_HEADS, SSM_CHUNK), lambda b, dr, s: (b, dr, chunk_of(dr, s))),
                  pl.BlockSpec((None, SSM_HEADS, SSM_CHUNK), lambda b, dr, s: (dr, 0, 0)),
                  pl.BlockSpec((None, SSM_CHUNK, SSM_INNER), lambda b, dr, s: (b, chunk_of(dr, s), 0)),
                  pl.BlockSpec((None, SSM_CHUNK, gs), lambda b, dr, s: (b, chunk_of(dr, s), SSM_INNER // gs)),
                  pl.BlockSpec((None, SSM_CHUNK, gs), lambda b, dr, s: (b, chunk_of(dr, s), SSM_INNER // gs + 1))],
        out_specs=pl.BlockSpec((None, None, SSM_CHUNK, SSM_INNER), lambda b, dr, s: (dr, b, chunk_of(dr, s), 0)),
        out_shape=jax.ShapeDtypeStruct((2, batch, t_all, SSM_INNER), BF),
        scratch_shapes=[pltpu.VMEM((SSM_HEADS // 2, SSM_STATE, LANES), F32)],
        compiler_params=_cparams(("parallel", "parallel", "arbitrary")),
        name="ssd_scan",
    )(dt_t, a_neg, xbc, xbc, xbc)

    dvec = jnp.repeat(d_skip, SSM_INNER // SSM_HEADS).reshape(1, SSM_INNER)
    og = out_g.reshape(1, SSM_INNER)
    wo = w_out.astype(BF)
    out_specs, out_shapes = _tail_out(batch, t_all)
    y_spec = lambda dr: pl.BlockSpec((None, None, TM, SSM_INNER), lambda b, t: (dr, b, t, 0))
    return pl.pallas_call(
        _ssm_finish_kernel,
        grid=(batch, t_all // TM),
        in_specs=[y_spec(0), y_spec(1), _tok_spec(SSM_INNER), _tok_spec(SSM_INNER), _full_spec(dvec),
                  _full_spec(og), _full_spec(wo), _tok_spec(D), _mod_spec()] + _tail_in_specs(n2g, wr, br),
        out_specs=out_specs,
        out_shape=out_shapes,
        compiler_params=_cparams(("parallel", "parallel")),
        name="ssm_finish",
    )(y, y, xbc, z, dvec, og, wo, h, mod, n2g, wr, br)


def _mla_lane_src():
    src = np.full(LANES, -1, np.int32)
    src[0:16] = MLA_NOPE + np.arange(16)
    src[16:64] = np.arange(48)
    src[64:80] = MLA_NOPE + 16 + np.arange(16)
    src[80:96] = 48 + np.arange(16)
    return src


def _mla_proj_kernel(h_ref, mod_ref, n1g_ref, win_ref, qng_ref, kvng_ref, wuq_ref, wuk_ref, wuv_ref,
                     gq_ref, gk_ref, cos_ref, sin_ref, avg_ref, q_ref, k_ref, v_ref):
    mod = mod_ref[...]
    xn = _norm_mod(h_ref[...], n1g_ref[...], mod[1:2], mod[0:1]).astype(BF)
    lat = jnp.dot(xn, win_ref[...], preferred_element_type=F32)
    cq = lat[:, :MLA_Q_RANK]
    ckv = lat[:, MLA_Q_RANK:MLA_Q_RANK + MLA_KV_RANK]
    kpe = lat[:, MLA_Q_RANK + MLA_KV_RANK:]
    cq = (cq * lax.rsqrt(jnp.mean(cq * cq, axis=-1, keepdims=True) + EPS) * qng_ref[...]).astype(BF)
    ckv = (ckv * lax.rsqrt(jnp.mean(ckv * ckv, axis=-1, keepdims=True) + EPS) * kvng_ref[...]).astype(BF)
    cos = cos_ref[...]
    sin = sin_ref[...]
    avg = avg_ref[...]
    scale = MLA_QK ** -0.5 * LOG2E
    qa = jnp.dot(cq, wuq_ref[...], preferred_element_type=F32)
    ka = jnp.dot(ckv, wuk_ref[...], preferred_element_type=F32)
    kpe2 = jnp.concatenate([kpe, kpe], axis=1)
    for jj in range(MLA_HEADS // 2):
        for dst, src, add, g_ref, mult in ((q_ref, qa, None, gq_ref, scale), (k_ref, ka, kpe2, gk_ref, 1.0)):
            yy = src[:, jj * 2 * LANES:(jj + 1) * 2 * LANES]
            if add is not None:
                yy = yy + add
            ms2 = jnp.dot((yy * yy).astype(BF), avg, preferred_element_type=F32) * (1.0 / MLA_QK)
            for hh in range(2):
                j = 2 * jj + hh
                y = yy[:, hh * LANES:(hh + 1) * LANES]
                y = y * lax.rsqrt(ms2[:, hh * LANES:(hh + 1) * LANES] + EPS) * g_ref[...]
                y = y * cos + pltpu.roll(y, 64, 1) * sin
                dst[:, j * LANES:(j + 1) * LANES] = (y * mult).astype(BF)
    v_ref[...] = jnp.dot(ckv, wuv_ref[...], preferred_element_type=F32).astype(BF)


def _mla_attn_kernel(q_ref, k_ref, v_ref, o_ref):
    t_all = q_ref.shape[0]
    lane = lax.broadcasted_iota(jnp.int32, (TM, LANES), 1)

    def tile(row0, n_keys):
        v = v_ref[0:n_keys, :]
        outs = [_softmax_pv(q_ref[pl.ds(row0, TM), hh * LANES:(hh + 1) * LANES],
                            k_ref[0:n_keys, hh * LANES:(hh + 1) * LANES], v)
                for hh in range(2)]
        o_ref[pl.ds(row0, TM), :] = jnp.where(lane < MLA_V, outs[0], outs[1]).astype(BF)

    tile(0, CTX)

    _for_each_latent_tile(tile, t_all)


def _mla_attention(h, mod, n1g, w_in, q_norm_g, kv_norm_g, w_uq, w_ukv, q_g, k_g):
    batch, t_all, _ = h.shape
    src = _mla_lane_src()
    used = src >= 0
    srcc = np.where(used, src, 0)
    nh = MLA_HEADS * LANES
    kpe_cols = np.where(srcc >= MLA_NOPE, MLA_Q_RANK + MLA_KV_RANK + srcc - MLA_NOPE, 0)
    kpe_used = np.logical_and(used, src >= MLA_NOPE)
    w_kpe = jnp.where(jnp.asarray(kpe_used)[None, :], w_in[:, kpe_cols], 0.0)
    win = jnp.concatenate([w_in[:, :MLA_Q_RANK + MLA_KV_RANK], w_kpe], axis=1).astype(BF)
    q_cols = (np.arange(MLA_HEADS)[:, None] * MLA_QK + srcc[None, :]).reshape(-1)
    q_mask = jnp.asarray(np.tile(used, MLA_HEADS))[None, :]
    wuq = jnp.where(q_mask, w_uq[:, q_cols], 0.0).astype(BF)
    nope_used = np.logical_and(used, src < MLA_NOPE)
    k_cols = (np.arange(MLA_HEADS)[:, None] * (MLA_NOPE + MLA_V) + np.where(nope_used, srcc, 0)[None, :]).reshape(-1)
    k_mask = jnp.asarray(np.tile(nope_used, MLA_HEADS))[None, :]
    wuk = jnp.where(k_mask, w_ukv[:, k_cols], 0.0).astype(BF)
    v_cols = (np.arange(MLA_HEADS)[:, None] * (MLA_NOPE + MLA_V) + MLA_NOPE + np.arange(MLA_V)[None, :]).reshape(-1)
    wuv = w_ukv[:, v_cols].astype(BF)
    gq = jnp.where(jnp.asarray(used), q_g[srcc], 0.0).reshape(1, LANES)
    gk = jnp.where(jnp.asarray(used), k_g[srcc], 0.0).reshape(1, LANES)
    cos16, sin16 = _rope_tables(t_all - CTX, MLA_ROPE)
    t_lat = t_all - CTX
    cos_l = jnp.ones((t_lat, LANES), F32).at[:, 0:16].set(cos16).at[:, 64:80].set(cos16)
    sin_l = jnp.zeros((t_lat, LANES), F32).at[:, 0:16].set(-sin16).at[:, 64:80].set(sin16)
    cos, sin = _with_ctx_identity(cos_l, sin_l)
    head_of_lane = np.arange(2 * LANES) // LANES
    avg = jnp.asarray((head_of_lane[:, None] == head_of_lane[None, :]).astype(np.float32)).astype(BF)
    qng = q_norm_g.reshape(1, MLA_Q_RANK)
    kvng = kv_norm_g.reshape(1, MLA_KV_RANK)
    tab_spec = pl.BlockSpec((TM, LANES), lambda b, t: (t, 0))
    q, k, v = pl.pallas_call(
        _mla_proj_kernel,
        grid=(batch, t_all // TM),
        in_specs=[_tok_spec(D), _mod_spec(), _full_spec(n1g), _full_spec(win), _full_spec(qng), _full_spec(kvng),
                  _full_spec(wuq), _full_spec(wuk), _full_spec(wuv), _full_spec(gq), _full_spec(gk),
                  tab_spec, tab_spec, _full_spec(avg)],
        out_specs=[_tok_spec(nh), _tok_spec(nh), _tok_spec(MLA_HEADS * MLA_V)],
        out_shape=[jax.ShapeDtypeStruct((batch, t_all, nh), BF), jax.ShapeDtypeStruct((batch, t_all, nh), BF),
                   jax.ShapeDtypeStruct((batch, t_all, MLA_HEADS * MLA_V), BF)],
        compiler_params=_cparams(("parallel", "parallel")),
        name="mla_proj",
    )(h, mod, n1g, win, qng, kvng, wuq, wuk, wuv, gq, gk, cos, sin, avg)

    return pl.pallas_call(
        _mla_attn_kernel,
        grid=(batch, MLA_HEADS // 2),
        in_specs=[pl.BlockSpec((None, t_all, 2 * LANES), lambda b, p: (b, 0, p)),
                  pl.BlockSpec((None, t_all, 2 * LANES), lambda b, p: (b, 0, p)),
                  pl.BlockSpec((None, t_all, LANES), lambda b, p: (b, 0, p))],
        out_specs=pl.BlockSpec((None, t_all, LANES), lambda b, p: (b, 0, p)),
        out_shape=jax.ShapeDtypeStruct((batch, t_all, MLA_HEADS * MLA_V), BF),
        compiler_params=_cparams(("parallel", "parallel")),
        name="mla_attn",
    )(q, k, v)


ROWS_PER_ISSUE = 8


def _row_copy(src, src_row, dst, dst_row, sem):
    return pltpu.make_async_copy(src.at[pl.ds(src_row, 1), :], dst.at[pl.ds(dst_row, 1), :], sem)


def _dispatch_kernel(ztile_ref, dest_ref, x_ref, xs_hbm, zbuf, zsem, sem):
    def zero_copy(e):
        start = pl.multiple_of(ztile_ref[e], MOE_TM)
        return pltpu.make_async_copy(zbuf, xs_hbm.at[pl.ds(start, MOE_TM), :], zsem)

    @pl.when(pl.program_id(0) == 0)
    def _():
        zbuf[...] = jnp.zeros_like(zbuf)
        for e in range(2 * MOE_EXPERTS):
            @pl.when(ztile_ref[e] >= 0)
            def _():
                zero_copy(e).start()
        for e in range(2 * MOE_EXPERTS):
            @pl.when(ztile_ref[e] >= 0)
            def _():
                zero_copy(e).wait()

    def issue(blk, carry):
        for u in range(ROWS_PER_ISSUE):
            r = blk * ROWS_PER_ISSUE + u
            for k in range(2):
                _row_copy(x_ref, r, xs_hbm, dest_ref[0, 2 * r + k], sem).start(priority=k)
        return carry

    lax.fori_loop(0, TM // ROWS_PER_ISSUE, issue, 0)
    for k in range(2):
        pltpu.make_async_copy(x_ref, xs_hbm.at[pl.ds(0, TM), :], sem).wait()


def _moe_kernel(te_ref, nt_ref, x_ref, w1_ref, w3_ref, w2_ref, o_ref, w1b, w3b, w2b):
    i = pl.program_id(0)
    prev = te_ref[jnp.maximum(i - 1, 0)]

    @pl.when(jnp.logical_or(i == 0, te_ref[i] != prev))
    def _():
        w1b[...] = w1_ref[...].astype(BF)
        w3b[...] = w3_ref[...].astype(BF)
        w2b[...] = w2_ref[...].astype(BF)

    @pl.when(i < nt_ref[0])
    def _():
        x = x_ref[...].astype(BF)
        h1 = jnp.dot(x, w1b[...], preferred_element_type=F32)
        h3 = jnp.dot(x, w3b[...], preferred_element_type=F32)
        hid = (_silu(h1) * h3).astype(BF)
        o_ref[...] = jnp.dot(hid, w2b[...], preferred_element_type=F32)

    @pl.when(i >= nt_ref[0])
    def _():
        o_ref[...] = jnp.zeros_like(o_ref)


def _combine_kernel(dest_ref, dnext_ref, out_hbm, h_ref, mod_ref, wt_ref, ho_ref, buf, sem, *, n_steps):
    i = pl.program_id(0)
    slot = i % 2

    def fetch(d_ref, s):
        def issue(blk, carry):
            for u in range(ROWS_PER_ISSUE):
                r = blk * ROWS_PER_ISSUE + u
                for k in range(2):
                    _row_copy(out_hbm, d_ref[0, 2 * r + k], buf.at[s], k * TM + r, sem.at[s]).start(priority=k)
            return carry

        lax.fori_loop(0, TM // ROWS_PER_ISSUE, issue, 0)

    @pl.when(i == 0)
    def _():
        fetch(dest_ref, 0)

    @pl.when(i + 1 < n_steps)
    def _():
        fetch(dnext_ref, 1 - slot)

    pltpu.make_async_copy(out_hbm.at[pl.ds(0, 2 * TM), :], buf.at[slot], sem.at[slot]).wait()
    w = wt_ref[...]
    y = buf[slot, 0:TM, :] * w[:, 0:1] + buf[slot, TM:2 * TM, :] * w[:, 1:2]
    ho_ref[...] = h_ref[...] + mod_ref[...][5:6] * y


def _moe(h, mod, xn, ids, wts, cnt, w1, w3, w2, layer, latent_only):
    batch, t_all, _ = xn.shape
    n_tok = batch * t_all
    tiles_b = t_all // TM
    n_tt = n_tok // TM

    lat_b = tiles_b - 1
    n_steps = batch * lat_b if latent_only else n_tt
    n_pair = 2 * n_steps * TM
    n_tiles = (n_pair + MOE_EXPERTS * (MOE_TM - 1) + MOE_TM - 1) // MOE_TM

    def tile_of(i):
        return (i // lat_b) * tiles_b + 1 + i % lat_b if latent_only else i

    def next_tile_of(i):
        return tile_of(jnp.minimum(i + 1, n_steps - 1))

    cnt = cnt.reshape(n_tt, LANES)[:, :MOE_EXPERTS].astype(jnp.int32)
    if latent_only:
        cnt = jnp.where((jnp.arange(n_tt, dtype=jnp.int32) % tiles_b == 0)[:, None], 0, cnt)
    tiles_e = (jnp.sum(cnt, axis=0) + MOE_TM - 1) // MOE_TM
    tile_end = jnp.cumsum(tiles_e)
    tile_start = tile_end - tiles_e
    before = jnp.cumsum(cnt, axis=0) - cnt
    base = tile_start[None, :] * MOE_TM + before
    n_valid = tile_end[-1]
    tidx = jnp.arange(n_tiles, dtype=jnp.int32)
    te = jnp.minimum(jnp.sum((tile_end[None, :] <= tidx[:, None]).astype(jnp.int32), axis=1), MOE_EXPERTS - 1)
    te = jnp.where(tidx < n_valid, te, jnp.max(jnp.where(tiles_e > 0, jnp.arange(MOE_EXPERTS), 0))).astype(jnp.int32)
    assert n_tiles * MOE_TM - n_pair <= MOE_EXPERTS * MOE_TM
    tail = n_valid + jnp.arange(MOE_EXPERTS, dtype=jnp.int32)
    ztile = jnp.concatenate([jnp.where(tiles_e > 0, (tile_end - 1) * MOE_TM, -1),
                             jnp.where(tail < n_tiles, tail * MOE_TM, -1)]).astype(jnp.int32)

    picks = ids.reshape(n_tt, TM, LANES)
    expert = picks[:, :, 0:2].reshape(n_tt, 1, 2 * TM)
    rank = picks[:, :, 2:4].reshape(n_tt, 1, 2 * TM)
    is_pick = expert == jnp.arange(MOE_EXPERTS, dtype=jnp.int32)[None, :, None]
    dest = jnp.sum(jnp.where(is_pick, base[:, :, None], 0), axis=1, keepdims=True) + rank
    dest_spec = pl.BlockSpec((None, 1, 2 * TM), lambda i, *_: (tile_of(i), 0, 0), memory_space=pltpu.SMEM)

    xs = pl.pallas_call(
        _dispatch_kernel,
        grid_spec=pltpu.PrefetchScalarGridSpec(
            num_scalar_prefetch=1,
            grid=(n_steps,),
            in_specs=[dest_spec, pl.BlockSpec((TM, D), lambda i, zt: (tile_of(i), 0))],
            out_specs=pl.BlockSpec(memory_space=pl.ANY),
            scratch_shapes=[pltpu.VMEM((MOE_TM, D), F32), pltpu.SemaphoreType.DMA(()), pltpu.SemaphoreType.DMA(())],
        ),
        out_shape=jax.ShapeDtypeStruct((n_tiles * MOE_TM, D), F32),
        compiler_params=_cparams(("arbitrary",)),
        name="moe_dispatch",
    )(ztile, dest, xn.reshape(n_tok, D))

    out = pl.pallas_call(
        _moe_kernel,
        grid_spec=pltpu.PrefetchScalarGridSpec(
            num_scalar_prefetch=2,
            grid=(n_tiles,),
            in_specs=[pl.BlockSpec((MOE_TM, D), lambda i, te, nt: (i, 0)),
                      pl.BlockSpec((None, None, D, MOE_FF), lambda i, te, nt: (layer, te[i], 0, 0)),
                      pl.BlockSpec((None, None, D, MOE_FF), lambda i, te, nt: (layer, te[i], 0, 0)),
                      pl.BlockSpec((None, None, MOE_FF, D), lambda i, te, nt: (layer, te[i], 0, 0))],
            out_specs=pl.BlockSpec((MOE_TM, D), lambda i, te, nt: (i, 0)),
            scratch_shapes=[pltpu.VMEM((D, MOE_FF), BF), pltpu.VMEM((D, MOE_FF), BF), pltpu.VMEM((MOE_FF, D), BF)],
        ),
        out_shape=jax.ShapeDtypeStruct((n_tiles * MOE_TM, D), F32),
        compiler_params=_cparams(("arbitrary",)),
        name="moe_experts",
    )(te, n_valid.reshape(1).astype(jnp.int32), xs, w1, w3, w2)

    h_new = pl.pallas_call(
        functools.partial(_combine_kernel, n_steps=n_steps),
        grid=(n_steps,),
        in_specs=[pl.BlockSpec((None, 1, 2 * TM), lambda i: (tile_of(i), 0, 0), memory_space=pltpu.SMEM),
                  pl.BlockSpec((None, 1, 2 * TM), lambda i: (next_tile_of(i), 0, 0), memory_space=pltpu.SMEM),
                  pl.BlockSpec(memory_space=pl.ANY),
                  pl.BlockSpec((TM, D), lambda i: (tile_of(i), 0)),
                  pl.BlockSpec((None, None, 8, D),
                               lambda i: (tile_of(i) // tiles_b, jnp.minimum(tile_of(i) % tiles_b, 1), 0, 0)),
                  pl.BlockSpec((TM, LANES), lambda i: (tile_of(i), 0))],
        out_specs=pl.BlockSpec((TM, D), lambda i: (i, 0)),
        out_shape=jax.ShapeDtypeStruct((n_steps * TM, D), F32),
        scratch_shapes=[pltpu.VMEM((2, 2 * TM, D), F32), pltpu.SemaphoreType.DMA((2,))],
        compiler_params=_cparams(("arbitrary",)),
        name="moe_combine",
    )(dest, dest, out, h.reshape(n_tok, D), mod, wts.reshape(n_tok, LANES))
    return h_new.reshape(batch, n_steps * TM // batch, D)


def kernel(x, c, ctx, c_ctx, ada_w, ada_b, norm1_g, norm2_g, da_w_in, da_w_out, da_q_g, da_k_g, da_lam_q1, da_lam_k1, da_lam_q2, da_lam_k2, da_sub_g, gm_w_in, gm_v_g, gm_w_s, gm_b_s, gm_w_out, ssm_w_in, ssm_conv_w, ssm_conv_b, ssm_dt_bias, ssm_a_log, ssm_d, ssm_out_g, ssm_w_out, mla_w_in, mla_q_norm_g, mla_kv_norm_g, mla_w_uq, mla_w_ukv, mla_q_g, mla_k_g, mla_w_out, moe_w_group, moe_b_group, moe_w_router, moe_b_router, moe_w1, moe_w3, moe_w2):
    batch, seq, _ = x.shape
    assert ctx.shape[1] == CTX and seq % TM == 0 and seq % GRID_W == 0
    assert DEPTH >= 1

    rows = ((batch + 1 + 7) // 8) * 8
    cc = jnp.concatenate([c, c_ctx[None, :], jnp.zeros((rows - batch - 1, D), F32)], axis=0)
    ada = _ada_all(cc, ada_w, ada_b)
    mod_l = ada[:, :batch].reshape(DEPTH, batch, 6, D)
    mod_c = jnp.broadcast_to(ada[:, batch].reshape(DEPTH, 1, 6, D), (DEPTH, batch, 6, D))
    mod_all = jnp.stack([mod_c, mod_l], axis=2)
    mod_all = jnp.concatenate([mod_all, jnp.zeros((DEPTH, batch, 2, 2, D), F32)], axis=3)

    pad = LANES - MOE_EXPERTS - MOE_GROUPS
    for i in range(DEPTH):
        kind = i % 4
        mod = mod_all[i]
        n1g = norm1_g[i].reshape(1, D)
        n2g = norm2_g[i].reshape(1, D)
        wr = jnp.concatenate([moe_w_router[i], moe_w_group[i], jnp.zeros((D, pad), F32)], axis=1).astype(BF)
        br = jnp.concatenate([moe_b_router[i], moe_b_group[i], jnp.zeros((pad,), F32)]).reshape(1, LANES)
        if kind == 0:
            h, a = _diff_attention(ctx, x, mod, n1g, da_w_in[0], da_q_g[0], da_k_g[0], da_lam_q1[0], da_lam_k1[0],
                                   da_lam_q2[0], da_lam_k2[0], da_sub_g[0], i)
            h, xn, ids, wts, cnt =_outproj(a, da_w_out[0].astype(BF), h, mod, n2g, wr, br)
        elif kind == 1:
            h, xn, ids, wts, cnt =_gmlp_layer(h, mod, n1g, gm_w_in[0], gm_v_g[0], gm_w_s[0], gm_b_s[0], gm_w_out[0],
                                          n2g, wr, br)
        elif kind == 2:
            h, xn, ids, wts, cnt =_mamba_layer(h, mod, n1g, ssm_w_in[0], ssm_conv_w[0], ssm_conv_b[0], ssm_dt_bias[0],
                                           ssm_a_log[0], ssm_d[0], ssm_out_g[0], ssm_w_out[0], n2g, wr, br)
        else:
            a = _mla_attention(h, mod, n1g, mla_w_in[0], mla_q_norm_g[0], mla_kv_norm_g[0], mla_w_uq[0],
                               mla_w_ukv[0], mla_q_g[0], mla_k_g[0])
            h, xn, ids, wts, cnt =_outproj(a, mla_w_out[0].astype(BF), h, mod, n2g, wr, br)
        h = _moe(h, mod, xn, ids, wts, cnt, moe_w1, moe_w3, moe_w2, i, latent_only=(i == DEPTH - 1))
    return h
```
